```python
import math
import jax, jax.numpy as jnp
from jax import lax
import numpy as np

D_MODEL = 1024
BATCH = 8
SEQ = 4096
DEPTH = 1

NSA_HEADS = 8
NSA_KV_HEADS = 2
NSA_GROUP = NSA_HEADS // NSA_KV_HEADS
HEAD_DIM = 64
CMP_LEN = 32
CMP_STRIDE = 16
CMP_HIDDEN = 64
SLC_BLOCK = 64
SLC_TOPN = 16
WINDOW = 512
Q_CHUNK = 64
FORCED_BONUS = 1.0e4
DN_HEADS = 8
DN_KDIM = 64
DN_VDIM = 64
DN_CONV = 4
DN_CHUNK = 64
DT_MIN = 0.001
DT_MAX = 0.1
PEER_HEADS = 8
PEER_NKEYS = 128
PEER_EXPERTS = PEER_NKEYS * PEER_NKEYS
PEER_DKEY = 128
PEER_HALF = PEER_DKEY // 2
PEER_TOPK = 16
PEER_TOK_CHUNK = 128
REL_BUCKETS = 32
REL_MAX_DIST = 128
EPS = 1e-6
NEG_INF = -1e30

NSA_Q_W = NSA_HEADS * HEAD_DIM
NSA_KV_W = NSA_KV_HEADS * HEAD_DIM
DN_QK_W = DN_HEADS * DN_KDIM
DN_V_W = DN_HEADS * DN_VDIM
DN_CONV_W = 2 * DN_QK_W + DN_V_W
IN_SPLIT_SIZES = (NSA_Q_W, NSA_KV_W, NSA_KV_W, NSA_KV_W, NSA_KV_W, NSA_KV_W, NSA_KV_W, NSA_HEADS * 3, DN_QK_W, DN_QK_W, DN_V_W, DN_V_W, DN_HEADS, DN_HEADS, D_MODEL, D_MODEL)
IN_COLS = sum(IN_SPLIT_SIZES)
IN_SPLIT_POINTS = tuple(int(v) for v in np.cumsum(IN_SPLIT_SIZES)[:-1])

kernel_name = 'hybrid_nsa_gdn_peer_block'


def rms_norm(x, g):
    xf = x.astype(jnp.float32)
    y = xf * lax.rsqrt(jnp.mean(xf * xf, axis=-1, keepdims=True) + EPS)
    return (y * g.astype(jnp.float32)).astype(x.dtype)


def l2_normalize(x):
    xf = x.astype(jnp.float32)
    return (xf * lax.rsqrt(jnp.sum(xf * xf, axis=-1, keepdims=True) + EPS)).astype(x.dtype)


def masked_softmax(logits, mask):
    l = jnp.where(mask, logits.astype(jnp.float32), NEG_INF)
    m = jnp.max(l, axis=-1, keepdims=True)
    e = jnp.where(mask, jnp.exp(l - m), 0.0)
    return e / jnp.maximum(jnp.sum(e, axis=-1, keepdims=True), 1e-30)


def rel_bucket(dist):
    dist = jnp.maximum(dist, 0)
    max_exact = REL_BUCKETS // 2
    scaled = jnp.log(jnp.maximum(dist, 1).astype(jnp.float32) / max_exact) / math.log(REL_MAX_DIST / max_exact)
    large = jnp.minimum(max_exact + (scaled * (REL_BUCKETS - max_exact)).astype(jnp.int32), REL_BUCKETS - 1)
    return jnp.where(dist < max_exact, dist, large)


def causal_depthwise_conv(x, w):
    return lax.conv_general_dilated(x, w[:, None, :].astype(x.dtype), (1,), [(DN_CONV - 1, 0)], dimension_numbers=('NWC', 'WIO', 'NWC'), feature_group_count=x.shape[-1])


def compress_blocks(kv, blk_idx, pos_emb, w1, w2):
    B = kv.shape[0]
    n_cmp = blk_idx.shape[0]
    blk = kv[:, blk_idx] + pos_emb[None, None, :, None, :]
    blk = blk.transpose(0, 1, 3, 2, 4).reshape(B, n_cmp, NSA_KV_HEADS, CMP_LEN * HEAD_DIM)
    return jnp.einsum('bnhe,ed->bhnd', jax.nn.gelu(blk @ w1), w2)


def nsa_attention(q, k_cmp, v_cmp, k_slc, v_slc, k_win, v_win, gate_logits, rel_bias, cmp_pos_emb, w_ck1, w_ck2, w_cv1, w_cv2):
    B, S, _ = q.shape
    KV, G, DH = NSA_KV_HEADS, NSA_GROUP, HEAD_DIM
    q = q.reshape(B, S, KV, G, DH) * (DH ** -0.5)
    k_cmp, v_cmp, k_slc, v_slc, k_win, v_win = (t.reshape(B, S, KV, DH) for t in (k_cmp, v_cmp, k_slc, v_slc, k_win, v_win))
    dist_bias = rel_bias[rel_bucket(jnp.arange(S))].T.reshape(KV, G, S)
    n_cmp = (S - CMP_LEN) // CMP_STRIDE + 1
    cmp_start = jnp.arange(n_cmp) * CMP_STRIDE
    cmp_end = cmp_start + CMP_LEN - 1
    blk_idx = cmp_start[:, None] + jnp.arange(CMP_LEN)[None, :]
    kc = compress_blocks(k_cmp, blk_idx, cmp_pos_emb, w_ck1, w_ck2)
    vc = compress_blocks(v_cmp, blk_idx, cmp_pos_emb, w_cv1, w_cv2)
    n_slc = S // SLC_BLOCK
    top_n = min(SLC_TOPN, n_slc)
    slc_ids = jnp.arange(n_slc)
    slc_start = slc_ids * SLC_BLOCK
    overlap = ((cmp_start[:, None] <= slc_start[None, :] + SLC_BLOCK - 1) & (cmp_end[:, None] >= slc_start[None, :])).astype(q.dtype)
    ks = k_slc.reshape(B, n_slc, SLC_BLOCK, KV, DH).transpose(0, 3, 1, 2, 4)
    vs = v_slc.reshape(B, n_slc, SLC_BLOCK, KV, DH).transpose(0, 3, 1, 2, 4)
    gather_blocks = jax.vmap(jax.vmap(lambda tab, ix: tab[ix]))
    bias_lookup = jax.vmap(lambda tb, d: tb[:, d], in_axes=(0, 1))
    kw = jnp.pad(k_win, ((0, 0), (WINDOW, 0), (0, 0), (0, 0)))
    vw = jnp.pad(v_win, ((0, 0), (WINDOW, 0), (0, 0), (0, 0)))

    def chunk(i):
        s0 = i * Q_CHUNK
        t = s0 + jnp.arange(Q_CHUNK)
        qc = lax.dynamic_slice_in_dim(q, s0, Q_CHUNK, axis=1)
        valid_c = cmp_end[None, :] <= t[:, None]
        bias_c = dist_bias[:, :, jnp.clip(t[:, None] - cmp_end[None, :], 0, S - 1)]
        p_c = masked_softmax(jnp.einsum('bqhgd,bhnd->bhgqn', qc, kc) + bias_c, valid_c).astype(q.dtype)
        o_c = jnp.einsum('bhgqn,bhnd->bqhgd', p_c, vc)
        imp = jnp.einsum('bhgqn,nj->bhqj', p_c, overlap).astype(jnp.float32)
        cur = t[:, None] // SLC_BLOCK
        forced = (slc_ids[None, :] == 0) | (slc_ids[None, :] == cur) | (slc_ids[None, :] == cur - 1)
        future = slc_start[None, :] > t[:, None]
        score = jnp.where(future, NEG_INF, imp + jnp.where(forced, FORCED_BONUS, 0.0))
        _, sel = lax.top_k(score, top_n)
        k_sel = gather_blocks(ks, sel)
        v_sel = gather_blocks(vs, sel)
        pos = sel[..., None] * SLC_BLOCK + jnp.arange(SLC_BLOCK)
        valid_s = (pos <= t[:, None, None]).reshape(B, KV, 1, Q_CHUNK, top_n * SLC_BLOCK)
        bias_s = bias_lookup(dist_bias, jnp.clip(t[:, None, None] - pos, 0, S - 1)).transpose(2, 0, 1, 3, 4, 5)
        logit_s = (jnp.einsum('bqhgd,bhqnkd->bhgqnk', qc, k_sel) + bias_s).reshape(B, KV, G, Q_CHUNK, top_n * SLC_BLOCK)
        p_s = masked_softmax(logit_s, valid_s).astype(q.dtype).reshape(B, KV, G, Q_CHUNK, top_n, SLC_BLOCK)
        o_s = jnp.einsum('bhgqnk,bhqnkd->bqhgd', p_s, v_sel)
        kpos = s0 - WINDOW + jnp.arange(Q_CHUNK + WINDOW)
        k_w = lax.dynamic_slice_in_dim(kw, s0, Q_CHUNK + WINDOW, axis=1)
        v_w = lax.dynamic_slice_in_dim(vw, s0, Q_CHUNK + WINDOW, axis=1)
        rel = t[:, None] - kpos[None, :]
        valid_w = (kpos[None, :] >= 0) & (rel >= 0) & (rel < WINDOW)
        bias_w = dist_bias[:, :, jnp.clip(rel, 0, S - 1)]
        p_w = masked_softmax(jnp.einsum('bqhgd,bkhd->bhgqk', qc, k_w) + bias_w, valid_w).astype(q.dtype)
        o_w = jnp.einsum('bhgqk,bkhd->bqhgd', p_w, v_w)
        return o_c, o_s, o_w

    o_c, o_s, o_w = lax.map(chunk, jnp.arange(S // Q_CHUNK))

    def unchunk(o):
        return jnp.moveaxis(o, 0, 1).reshape(B, S, KV, G, DH)

    gates = jax.nn.sigmoid(gate_logits.reshape(B, S, KV, G, 3))
    o = gates[..., 0:1] * unchunk(o_c) + gates[..., 1:2] * unchunk(o_s) + gates[..., 2:3] * unchunk(o_w)
    return o.reshape(B, S, NSA_Q_W)


def chunk_gated_delta_rule(q, k, v, g, beta):
    B, S, H, DK = q.shape
    DV = v.shape[-1]
    C = DN_CHUNK
    n = S // C
    f32 = jnp.float32

    def chunks(t):
        t = t.astype(f32).reshape((B, n, C, H) + t.shape[3:])
        return jnp.moveaxis(t, 3, 1)

    q, k, v, g, beta = (chunks(t) for t in (q, k, v, g, beta))
    gc = jnp.cumsum(g, axis=-1)
    idx = jnp.arange(C)
    tril = idx[:, None] >= idx[None, :]
    strict = idx[:, None] > idx[None, :]
    diff = gc[..., :, None] - gc[..., None, :]
    decay = jnp.where(tril, jnp.exp(jnp.where(tril, diff, 0.0)), 0.0)
    kb = k * beta[..., None]
    vb = v * beta[..., None]
    a_mat = jnp.where(strict, jnp.einsum('bhnid,bhnjd->bhnij', kb, k) * decay, 0.0) + jnp.eye(C, dtype=f32)
    rhs = jnp.concatenate([vb, kb * jnp.exp(gc)[..., None]], axis=-1)
    sol = lax.linalg.triangular_solve(a_mat, rhs, left_side=True, lower=True, unit_diagonal=True)
    u, w = sol[..., :DV], sol[..., DV:]
    attn = jnp.einsum('bhnid,bhnjd->bhnij', q, k) * decay
    q_dec = q * jnp.exp(gc)[..., None]
    k_dec = k * jnp.exp(gc[..., -1:] - gc)[..., None]
    g_last = jnp.exp(gc[..., -1])

    def step(state, inp):
        q_i, k_i, u_i, w_i, a_i, gl_i = inp
        v_new = u_i - jnp.einsum('bhck,bhkv->bhcv', w_i, state)
        o_i = jnp.einsum('bhck,bhkv->bhcv', q_i, state) + jnp.einsum('bhij,bhjv->bhiv', a_i, v_new)
        state = state * gl_i[..., None, None] + jnp.einsum('bhck,bhcv->bhkv', k_i, v_new)
        return state, o_i

    xs = tuple(jnp.moveaxis(t, 2, 0) for t in (q_dec, k_dec, u, w, attn, g_last))
    _, o = lax.scan(step, jnp.zeros((B, H, DK, DV), f32), xs)
    return jnp.transpose(o, (1, 0, 3, 2, 4)).reshape(B, S, H, DV)


def gated_deltanet(q, k, v, z, b_logits, a_logits, conv_w, A_log, dt_bias, norm_g):
    B, S, _ = q.shape
    f32 = jnp.float32
    qkv = jax.nn.silu(causal_depthwise_conv(jnp.concatenate([q, k, v], axis=-1), conv_w))
    q, k, v = jnp.split(qkv, [DN_QK_W, 2 * DN_QK_W], axis=-1)
    q = l2_normalize(q.reshape(B, S, DN_HEADS, DN_KDIM)) * (DN_KDIM ** -0.5)
    k = l2_normalize(k.reshape(B, S, DN_HEADS, DN_KDIM))
    v = v.reshape(B, S, DN_HEADS, DN_VDIM)
    beta = jax.nn.sigmoid(b_logits.astype(f32))
    g = -jnp.exp(A_log.astype(f32)) * jax.nn.softplus(a_logits.astype(f32) + dt_bias.astype(f32))
    o = chunk_gated_delta_rule(q, k, v, g, beta).astype(z.dtype)
    o = rms_norm(o, norm_g) * jax.nn.silu(z.reshape(B, S, DN_HEADS, DN_VDIM))
    return o.reshape(B, S, DN_V_W)


def token_mixer(h, rel_bias, w_in, cmp_pos_emb, w_ck1, w_ck2, w_cv1, w_cv2, conv_w, A_log, dt_bias, dn_norm_g, w_branch_nsa, w_branch_dn, w_out):
    (nsa_q, k_c, v_c, k_s, v_s, k_w, v_w, nsa_gates, dn_q, dn_k, dn_v, dn_z, dn_b, dn_a, gate_nsa, gate_dn) = jnp.split(h @ w_in, list(IN_SPLIT_POINTS), axis=-1)
    o_nsa = nsa_attention(nsa_q, k_c, v_c, k_s, v_s, k_w, v_w, nsa_gates, rel_bias, cmp_pos_emb, w_ck1, w_ck2, w_cv1, w_cv2)
    o_dn = gated_deltanet(dn_q, dn_k, dn_v, dn_z, dn_b, dn_a, conv_w, A_log, dt_bias, dn_norm_g)
    merged = jax.nn.sigmoid(gate_nsa) * (o_nsa @ w_branch_nsa) + jax.nn.sigmoid(gate_dn) * (o_dn @ w_branch_dn)
    return merged @ w_out


def peer_ffn(h, w_query, keys1, keys2, expert_u, expert_v):
    B, S, D = h.shape
    qry = (h @ w_query).reshape(B, S, PEER_HEADS, PEER_DKEY)
    s1 = jnp.einsum('bshd,hkd->bshk', qry[..., :PEER_HALF], keys1).astype(jnp.float32)
    s2 = jnp.einsum('bshd,hkd->bshk', qry[..., PEER_HALF:], keys2).astype(jnp.float32)
    v1, i1 = lax.top_k(s1, PEER_TOPK)
    v2, i2 = lax.top_k(s2, PEER_TOPK)
    n_cand = PEER_TOPK * PEER_TOPK
    cand_s = (v1[..., :, None] + v2[..., None, :]).reshape(B, S, PEER_HEADS, n_cand)
    cand_i = (i1[..., :, None] * PEER_NKEYS + i2[..., None, :]).reshape(B, S, PEER_HEADS, n_cand)
    top_s, top_pos = lax.top_k(cand_s, PEER_TOPK)
    expert_idx = jnp.take_along_axis(cand_i, top_pos, axis=-1)
    gate = jax.nn.softmax(top_s, axis=-1).astype(h.dtype)
    n_chunks = (B * S) // PEER_TOK_CHUNK
    E = PEER_HEADS * PEER_TOPK
    xs = (h.reshape(n_chunks, PEER_TOK_CHUNK, D), expert_idx.reshape(n_chunks, PEER_TOK_CHUNK, E), gate.reshape(n_chunks, PEER_TOK_CHUNK, E))

    def chunk(args):
        xc, ic, gc = args
        act = jax.nn.gelu(jnp.einsum('td,ted->te', xc, expert_u[ic]))
        return jnp.einsum('te,ted->td', gc * act, expert_v[ic])

    return lax.map(chunk, xs).reshape(B, S, D)


def setup_inputs(seed: int = 0) -> dict:
    key = jax.random.key(seed)
    ks = jax.random.split(key, 32)
    L = DEPTH
    f32 = jnp.float32

    def nrm(k, shape, scale):
        return jax.random.normal(k, shape, f32) * scale

    dt = jnp.exp(jax.random.uniform(ks[16], (L, DN_HEADS), f32, math.log(DT_MIN), math.log(DT_MAX)))
    return {
        'x': nrm(ks[0], (BATCH, SEQ, D_MODEL), 1.0),
        'c': nrm(ks[1], (BATCH, D_MODEL), 1.0),
        'rel_bias': nrm(ks[2], (REL_BUCKETS, NSA_HEADS), 0.2),
        'final_g': 1.0 + nrm(ks[3], (D_MODEL,), 0.02),
        'w_ada': nrm(ks[4], (L, D_MODEL, 6 * D_MODEL), 0.5 * D_MODEL ** -0.5),
        'b_ada': nrm(ks[5], (L, 6 * D_MODEL), 0.02),
        'norm1_g': 1.0 + nrm(ks[6], (L, D_MODEL), 0.02),
        'w_in': nrm(ks[7], (L, D_MODEL, IN_COLS), D_MODEL ** -0.5),
        'cmp_pos_emb': nrm(ks[8], (L, CMP_LEN, HEAD_DIM), 0.1),
        'w_cmp_k1': nrm(ks[9], (L, CMP_LEN * HEAD_DIM, CMP_HIDDEN), (CMP_LEN * HEAD_DIM) ** -0.5),
        'w_cmp_k2': nrm(ks[10], (L, CMP_HIDDEN, HEAD_DIM), CMP_HIDDEN ** -0.5),
        'w_cmp_v1': nrm(ks[11], (L, CMP_LEN * HEAD_DIM, CMP_HIDDEN), (CMP_LEN * HEAD_DIM) ** -0.5),
        'w_cmp_v2': nrm(ks[12], (L, CMP_HIDDEN, HEAD_DIM), CMP_HIDDEN ** -0.5),
        'dn_conv_w': nrm(ks[13], (L, DN_CONV, DN_CONV_W), DN_CONV ** -0.5),
        'dn_A_log': jnp.log(jax.random.uniform(ks[15], (L, DN_HEADS), f32, 1.0, 16.0)),
        'dn_dt_bias': dt + jnp.log(-jnp.expm1(-dt)),
        'dn_norm_g': 1.0 + nrm(ks[17], (L, DN_VDIM), 0.02),
        'w_branch_nsa': nrm(ks[18], (L, NSA_Q_W, D_MODEL), NSA_Q_W ** -0.5),
        'w_branch_dn': nrm(ks[19], (L, DN_V_W, D_MODEL), DN_V_W ** -0.5),
        'w_out': nrm(ks[20], (L, D_MODEL, D_MODEL), D_MODEL ** -0.5),
        'norm2_g': 1.0 + nrm(ks[21], (L, D_MODEL), 0.02),
        'peer_w_query': nrm(ks[22], (L, D_MODEL, PEER_HEADS * PEER_DKEY), D_MODEL ** -0.5),
        'peer_keys1': nrm(ks[23], (L, PEER_HEADS, PEER_NKEYS, PEER_HALF), PEER_HALF ** -0.5),
        'peer_keys2': nrm(ks[24], (L, PEER_HEADS, PEER_NKEYS, PEER_HALF), PEER_HALF ** -0.5),
        'peer_u': nrm(ks[25], (L, PEER_EXPERTS, D_MODEL), D_MODEL ** -0.5),
        'peer_v': nrm(ks[26], (L, PEER_EXPERTS, D_MODEL), 1.0),
    }


def reference(x, c, rel_bias, final_g, w_ada, b_ada, norm1_g, w_in, cmp_pos_emb, w_cmp_k1, w_cmp_k2, w_cmp_v1, w_cmp_v2, dn_conv_w, dn_A_log, dn_dt_bias, dn_norm_g, w_branch_nsa, w_branch_dn, w_out, norm2_g, peer_w_query, peer_keys1, peer_keys2, peer_u, peer_v):
    cond = jax.nn.silu(c)
    for l in range(DEPTH):
        mod = cond @ w_ada[l] + b_ada[l]
        sh1, sc1, g1, sh2, sc2, g2 = (m[:, None, :] for m in jnp.split(mod, 6, axis=-1))
        h = rms_norm(x, norm1_g[l]) * (1.0 + sc1) + sh1
        x = x + g1 * token_mixer(h, rel_bias, w_in[l], cmp_pos_emb[l], w_cmp_k1[l], w_cmp_k2[l], w_cmp_v1[l], w_cmp_v2[l], dn_conv_w[l], dn_A_log[l], dn_dt_bias[l], dn_norm_g[l], w_branch_nsa[l], w_branch_dn[l], w_out[l])
        h = rms_norm(x, norm2_g[l]) * (1.0 + sc2) + sh2
        x = x + g2 * peer_ffn(h, peer_w_query[l], peer_keys1[l], peer_keys2[l], peer_u[l], peer_v[l])
    return rms_norm(x, final_g)
```

```python
import functools
import math

import numpy as np
import jax
import jax.numpy as jnp
from jax import lax
from jax.experimental import pallas as pl
from jax.experimental.pallas import tpu as pltpu

F32 = jnp.float32
BF16 = jnp.bfloat16

NSA_HEADS = 8
NSA_KV_HEADS = 2
NSA_GROUP = NSA_HEADS // NSA_KV_HEADS
HEAD_DIM = 64
CMP_LEN = 32
CMP_STRIDE = 16
CMP_HIDDEN = 64
SLC_BLOCK = 64
SLC_TOPN = 16
WINDOW = 512
FORCED_BONUS = 1.0e4
DN_HEADS = 8
DN_KDIM = 64
DN_VDIM = 64
DN_CONV = 4
DN_CHUNK = 64
PEER_HEADS = 8
PEER_NKEYS = 128
PEER_DKEY = 128
PEER_HALF = PEER_DKEY // 2
PEER_TOPK = 16
REL_BUCKETS = 32
REL_MAX_DIST = 128
EPS = 1e-6
NEG_INF = -1e30

NSA_Q_W = NSA_HEADS * HEAD_DIM
NSA_KV_W = NSA_KV_HEADS * HEAD_DIM
DN_W = DN_HEADS * DN_KDIM
IN_SPLIT_SIZES = (NSA_Q_W, NSA_KV_W, NSA_KV_W, NSA_KV_W, NSA_KV_W, NSA_KV_W, NSA_KV_W, NSA_HEADS * 3,
                  DN_W, DN_W, DN_W, DN_W, DN_HEADS, DN_HEADS)

LANES = 128
ATT_TILE = 128
VMEM_LIMIT = 56 * 1024 * 1024

C_DNQ, C_DNK, C_DNV, C_DNZ = 0, 512, 1024, 1536
C_GNSA, C_GDN = 2048, 3072
C_KC, C_VC, C_SMALL = 4096, 4224, 4352
C32 = 4480
SM_GATES, SM_B, SM_A = 0, 24, 32
C16 = 1024


def _gelu_tanh(x):
    return 0.5 * x * (1.0 + jnp.tanh(math.sqrt(2.0 / math.pi) * (x + 0.044715 * (x * x * x))))


def _sigmoid(x):
    return 1.0 / (1.0 + jnp.exp(-x))


def _silu(x):
    return x * _sigmoid(x)


def _softplus(x):
    return jnp.maximum(x, 0.0) + jnp.log1p(jnp.exp(-jnp.abs(x)))


def _dot(a, b):
    return jnp.dot(a, b, preferred_element_type=F32)


def _dot_nt(a, b):
    return lax.dot_general(a, b, (((1,), (1,)), ((), ())), preferred_element_type=F32)


def _dot_tn(a, b):
    return lax.dot_general(a, b, (((0,), (0,)), ((), ())), preferred_element_type=F32)


def _split2(x):
    hi = x.astype(BF16)
    lo = (x - hi.astype(F32)).astype(BF16)
    return hi, lo


def _split3(x):
    hi = x.astype(BF16)
    r = x - hi.astype(F32)
    mid = r.astype(BF16)
    lo = (r - mid.astype(F32)).astype(BF16)
    return hi, mid, lo


def _dot_x3(a, b):
    ah, al = _split2(a)
    bh, bl = _split2(b)
    return _dot(ah, bh) + (_dot(ah, bl) + _dot(al, bh))


def _dot_01(m01, x):
    hi, mid, lo = _split3(x)
    return _dot(m01, hi) + (_dot(m01, mid) + _dot(m01, lo))


def _cparams(*sem):
    return pltpu.CompilerParams(dimension_semantics=sem, vmem_limit_bytes=VMEM_LIMIT)


def _ada_kernel(c_ref, w_ref, b_ref, o_ref):
    cond = _silu(c_ref[...])
    o_ref[...] = _dot(cond.astype(BF16), w_ref[...].astype(BF16)) + b_ref[...]


def _ada(c, w_ada, b_ada):
    bsz, d = c.shape
    n_out = w_ada.shape[1]
    blk = d
    return pl.pallas_call(
        _ada_kernel,
        grid=(n_out // blk,),
        in_specs=[pl.BlockSpec((bsz, d), lambda j: (0, 0)),
                  pl.BlockSpec((d, blk), lambda j: (0, j)),
                  pl.BlockSpec((1, blk), lambda j: (0, j))],
        out_specs=pl.BlockSpec((bsz, blk), lambda j: (0, j)),
        out_shape=jax.ShapeDtypeStruct((bsz, n_out), F32),
        compiler_params=_cparams("arbitrary"),
        name="ada",
    )(c, w_ada, b_ada.reshape(1, n_out))


def _modulated_norm(x, g, mod, shift_row, scale_row):
    var = jnp.mean(x * x, axis=-1, keepdims=True)
    y = x * lax.rsqrt(var + EPS) * g
    return y * (1.0 + mod[scale_row:scale_row + 1]) + mod[shift_row:shift_row + 1]


def _proj_in_kernel(x_ref, mod_ref, g_ref, w16_ref, w32_ref, o16_ref, o32_ref):
    h = _modulated_norm(x_ref[0], g_ref[...], mod_ref[0], 0, 1).astype(BF16)
    step = 512
    for c0 in range(0, o16_ref.shape[2], step):
        c1 = min(c0 + step, o16_ref.shape[2])
        o16_ref[0, :, c0:c1] = _dot(h, w16_ref[:, c0:c1]).astype(BF16)
    for c0 in range(0, o32_ref.shape[2], step):
        c1 = min(c0 + step, o32_ref.shape[2])
        o32_ref[0, :, c0:c1] = _dot(h, w32_ref[:, c0:c1])


def _proj_in(x, mod, g, w16, w32, tm=256):
    bsz, s, d = x.shape
    return pl.pallas_call(
        _proj_in_kernel,
        grid=(bsz, s // tm),
        in_specs=[pl.BlockSpec((1, tm, d), lambda b, i: (b, i, 0)),
                  pl.BlockSpec((1, 6, d), lambda b, i: (b, 0, 0)),
                  pl.BlockSpec((1, d), lambda b, i: (0, 0)),
                  pl.BlockSpec(w16.shape, lambda b, i: (0, 0)),
                  pl.BlockSpec(w32.shape, lambda b, i: (0, 0))],
        out_specs=[pl.BlockSpec((1, tm, w16.shape[1]), lambda b, i: (b, i, 0)),
                   pl.BlockSpec((1, tm, w32.shape[1]), lambda b, i: (b, i, 0))],
        out_shape=[jax.ShapeDtypeStruct((bsz, s, w16.shape[1]), BF16),
                   jax.ShapeDtypeStruct((bsz, s, w32.shape[1]), F32)],
        compiler_params=_cparams("arbitrary", "arbitrary"),
        name="proj_in",
    )(x, mod, g.reshape(1, d), w16, w32)


def _compress_kernel(k2_ref, v2_ref, posa_ref, posb_ref, ak_ref, bk_ref, w2k_ref, av_ref, bv_ref, w2v_ref,
                     kc_ref, vc_ref):
    nrow = k2_ref.shape[1]

    def one(x_ref, a_ref, b_ref, w2_ref, o_ref):
        x = x_ref[0]
        ya = _dot((x + posa_ref[...]).astype(BF16), a_ref[...])
        yb = _dot((x + posb_ref[...]).astype(BF16), b_ref[...])
        hid = _gelu_tanh(ya + pltpu.roll(yb, nrow - 1, 0))
        out = _dot(hid.astype(BF16), w2_ref[...])
        for h in range(NSA_KV_HEADS):
            o_ref[0, h] = out[:, h * HEAD_DIM:(h + 1) * HEAD_DIM]

    one(k2_ref, ak_ref, bk_ref, w2k_ref, kc_ref)
    one(v2_ref, av_ref, bv_ref, w2v_ref, vc_ref)


def _expand_cmp_weights(w1, w2):
    half = CMP_LEN // 2
    w1r = w1.reshape(2, half, HEAD_DIM, CMP_HIDDEN)
    eye = jnp.eye(NSA_KV_HEADS, dtype=w1.dtype)
    exp = jnp.einsum('pldj,hg->plhdgj', w1r, eye).reshape(2, half * NSA_KV_HEADS * HEAD_DIM,
                                                          NSA_KV_HEADS * CMP_HIDDEN)
    w2e = jnp.einsum('jd,hg->hjgd', w2, eye).reshape(NSA_KV_HEADS * CMP_HIDDEN, NSA_KV_HEADS * HEAD_DIM)
    return exp[0].astype(BF16), exp[1].astype(BF16), w2e.astype(BF16)


def _compress(k2, v2, pos_emb, w_ck1, w_ck2, w_cv1, w_cv2):
    bsz, nrow, width = k2.shape
    half = CMP_LEN // 2
    pos = jnp.broadcast_to(pos_emb.reshape(2, half, 1, HEAD_DIM), (2, half, NSA_KV_HEADS, HEAD_DIM))
    pos = pos.reshape(2, 1, width)
    ak, bk, w2k = _expand_cmp_weights(w_ck1, w_ck2)
    av, bv, w2v = _expand_cmp_weights(w_cv1, w_cv2)
    full2 = lambda a: pl.BlockSpec(a.shape, lambda b: (0, 0))
    out_sds = jax.ShapeDtypeStruct((bsz, NSA_KV_HEADS, nrow, HEAD_DIM), F32)
    out_spec = pl.BlockSpec((1, NSA_KV_HEADS, nrow, HEAD_DIM), lambda b: (b, 0, 0, 0))
    return pl.pallas_call(
        _compress_kernel,
        grid=(bsz,),
        in_specs=[pl.BlockSpec((1, nrow, width), lambda b: (b, 0, 0)),
                  pl.BlockSpec((1, nrow, width), lambda b: (b, 0, 0)),
                  full2(pos[0]), full2(pos[1]), full2(ak), full2(bk), full2(w2k), full2(av), full2(bv), full2(w2v)],
        out_specs=[out_spec, out_spec],
        out_shape=[out_sds, out_sds],
        compiler_params=_cparams("arbitrary"),
        name="compress",
    )(k2, v2, pos[0], pos[1], ak, bk, w2k, av, bv, w2v)


def _nsa_cmp_kernel(q_ref, kc_ref, vc_ref, bias_ref, ovt_ref, oc_ref, mask_ref, *, top_n):
    tq = q_ref.shape[1]
    ncmp = kc_ref.shape[2]
    nslc = ovt_ref.shape[0]
    s0 = pl.program_id(2) * tq
    kcb = kc_ref[0, 0].astype(BF16)
    vcb = vc_ref[0, 0].astype(BF16)
    t_q = s0 + lax.broadcasted_iota(jnp.int32, (tq, ncmp), 0)
    n_i = lax.broadcasted_iota(jnp.int32, (tq, ncmp), 1)
    valid = (n_i * CMP_STRIDE + (CMP_LEN - 1)) <= t_q
    ovt = ovt_ref[...]
    imp_t = jnp.zeros((nslc, tq), F32)
    outs = []
    for g in range(NSA_GROUP):
        qg = q_ref[0, :, g * HEAD_DIM:(g + 1) * HEAD_DIM]
        logit = _dot_nt(qg, kcb) + bias_ref[g]
        l = jnp.where(valid, logit, NEG_INF)
        m = jnp.max(l, axis=-1, keepdims=True)
        e = jnp.where(valid, jnp.exp(l - m), 0.0)
        p = e / jnp.maximum(jnp.sum(e, axis=-1, keepdims=True), 1e-30)
        pb = p.astype(BF16)
        outs.append(_dot(pb, vcb))
        imp_t = imp_t + _dot_nt(ovt, pb)
    oc_ref[0] = jnp.concatenate(outs, axis=1)

    t_l = s0 + lax.broadcasted_iota(jnp.int32, (nslc, tq), 1)
    j_s = lax.broadcasted_iota(jnp.int32, (nslc, tq), 0)
    cur = lax.shift_right_logical(t_l, int(math.log2(SLC_BLOCK)))
    forced = (j_s == 0) | (j_s == cur) | (j_s == cur - 1)
    future = (j_s * SLC_BLOCK) > t_l
    score = jnp.where(future, NEG_INF, imp_t + jnp.where(forced, FORCED_BONUS, 0.0))
    rank = jnp.zeros((nslc, tq), F32)
    for i in range(nslc):
        r = score[i:i + 1, :]
        beats = (r > score) | ((r == score) & (j_s > i))
        rank = rank + jnp.where(beats, 1.0, 0.0)
    sel_t = jnp.where(rank < float(top_n), 1.0, 0.0).astype(BF16)
    eye = (lax.broadcasted_iota(jnp.int32, (tq, tq), 0) == lax.broadcasted_iota(jnp.int32, (tq, tq), 1))
    mask_ref[0, 0] = _dot_nt(jnp.where(eye, 1.0, 0.0).astype(BF16), sel_t).astype(BF16)


def _nsa_cmp(p16, kc, vc, bias_c, ovt, top_n):
    bsz, s, _ = p16.shape
    tq = ATT_TILE
    ncmp = kc.shape[2]
    nslc = ovt.shape[0]
    qw = NSA_GROUP * HEAD_DIM
    return pl.pallas_call(
        functools.partial(_nsa_cmp_kernel, top_n=top_n),
        grid=(bsz, NSA_KV_HEADS, s // tq),
        in_specs=[pl.BlockSpec((1, tq, qw), lambda b, h, i: (b, i, h)),
                  pl.BlockSpec((1, 1, ncmp, HEAD_DIM), lambda b, h, i: (b, h, 0, 0)),
                  pl.BlockSpec((1, 1, ncmp, HEAD_DIM), lambda b, h, i: (b, h, 0, 0)),
                  pl.BlockSpec((NSA_GROUP, tq, ncmp), lambda b, h, i: (h, i, 0)),
                  pl.BlockSpec(ovt.shape, lambda b, h, i: (0, 0))],
        out_specs=[pl.BlockSpec((1, tq, qw), lambda b, h, i: (b, i, h)),
                   pl.BlockSpec((1, 1, tq, nslc), lambda b, h, i: (b, h, i, 0))],
        out_shape=[jax.ShapeDtypeStruct((bsz, s, NSA_Q_W), F32),
                   jax.ShapeDtypeStruct((bsz, NSA_KV_HEADS, s, nslc), BF16)],
        compiler_params=_cparams("arbitrary", "arbitrary", "arbitrary"),
        name="nsa_cmp",
    )(p16, kc, vc, bias_c, ovt)


def _nsa_sw_kernel(q_ref, ks_ref, vs_ref, kw_ref, vw_ref, mask_ref, exp_ref, toep_ref, oc_ref, sm_ref, o_ref):
    tq = q_ref.shape[1]
    tk = ATT_TILE
    i = pl.program_id(1)
    s0 = i * tq
    rows = NSA_GROUP * tq
    gate = _sigmoid(sm_ref[0])
    t_q = s0 + lax.broadcasted_iota(jnp.int32, (tq, tk), 0)
    k_l = lax.broadcasted_iota(jnp.int32, (tq, tk), 1)
    n_win = WINDOW // tk
    pieces = []
    for h in range(NSA_KV_HEADS):
        hs = slice(h * HEAD_DIM, (h + 1) * HEAD_DIM)
        q4 = jnp.concatenate([q_ref[0, :, (h * NSA_GROUP + g) * HEAD_DIM:(h * NSA_GROUP + g + 1) * HEAD_DIM]
                              for g in range(NSA_GROUP)], axis=0)
        mq = mask_ref[0, h]

        def tile_update(kt, carry, k_ref, v_ref, valid_fn):
            m, l, acc = carry
            k = k_ref[0, pl.ds(pl.multiple_of(kt * tk, tk), tk), hs]
            v = v_ref[0, pl.ds(pl.multiple_of(kt * tk, tk), tk), hs]
            mm = jnp.minimum(i - kt, 2)
            bias = jnp.concatenate([toep_ref[h * NSA_GROUP + g, mm] for g in range(NSA_GROUP)], axis=0)
            sc = _dot_nt(q4, k) + bias
            valid = valid_fn(kt)
            valid4 = jnp.concatenate([valid] * NSA_GROUP, axis=0)
            lg = jnp.where(valid4, sc, NEG_INF)
            m_new = jnp.maximum(m, jnp.max(lg, axis=-1, keepdims=True))
            alpha = jnp.exp(m - m_new)
            e = jnp.where(valid4, jnp.exp(lg - m_new), 0.0)
            l_new = alpha * l + jnp.sum(e, axis=-1, keepdims=True)
            acc_new = alpha * acc + _dot(e.astype(BF16), v)
            return m_new, l_new, acc_new

        def sel_valid(kt):
            mk = _dot(mq, exp_ref[kt])
            kpos = kt * tk + k_l
            return (mk > 0.5) & (kpos <= t_q)

        def win_valid(kt):
            rel = t_q - (kt * tk + k_l)
            return (rel >= 0) & (rel < WINDOW)

        init = (jnp.full((rows, 1), NEG_INF, F32), jnp.zeros((rows, 1), F32), jnp.zeros((rows, HEAD_DIM), F32))
        _, l_s, acc_s = lax.fori_loop(
            0, i + 1, lambda kt, c: tile_update(kt, c, ks_ref, vs_ref, sel_valid), init)
        _, l_w, acc_w = lax.fori_loop(
            jnp.maximum(i - n_win, 0), i + 1, lambda kt, c: tile_update(kt, c, kw_ref, vw_ref, win_valid), init)
        o_s = acc_s / jnp.maximum(l_s, 1e-30)
        o_w = acc_w / jnp.maximum(l_w, 1e-30)
        for g in range(NSA_GROUP):
            hd = h * NSA_GROUP + g
            c = SM_GATES + hd * 3
            o_c = oc_ref[0, :, hd * HEAD_DIM:(hd + 1) * HEAD_DIM]
            pieces.append(gate[:, c:c + 1] * o_c + gate[:, c + 1:c + 2] * o_s[g * tq:(g + 1) * tq]
                          + gate[:, c + 2:c + 3] * o_w[g * tq:(g + 1) * tq])
    o_ref[0] = jnp.concatenate(pieces, axis=1).astype(o_ref.dtype)


def _nsa_sw(p16, p32, mask, expand, toep, o_c):
    bsz, s, _ = p16.shape
    tq = ATT_TILE
    nslc = mask.shape[3]
    kvb = NSA_Q_W // NSA_KV_W
    return pl.pallas_call(
        _nsa_sw_kernel,
        grid=(bsz, s // tq),
        in_specs=[pl.BlockSpec((1, tq, NSA_Q_W), lambda b, i: (b, i, 0)),
                  pl.BlockSpec((1, s, NSA_KV_W), lambda b, i: (b, 0, kvb)),
                  pl.BlockSpec((1, s, NSA_KV_W), lambda b, i: (b, 0, kvb + 1)),
                  pl.BlockSpec((1, s, NSA_KV_W), lambda b, i: (b, 0, kvb + 2)),
                  pl.BlockSpec((1, s, NSA_KV_W), lambda b, i: (b, 0, kvb + 3)),
                  pl.BlockSpec((1, NSA_KV_HEADS, tq, nslc), lambda b, i: (b, 0, i, 0)),
                  pl.BlockSpec(expand.shape, lambda b, i: (0, 0, 0)),
                  pl.BlockSpec(toep.shape, lambda b, i: (0, 0, 0, 0)),
                  pl.BlockSpec((1, tq, NSA_Q_W), lambda b, i: (b, i, 0)),
                  pl.BlockSpec((1, tq, LANES), lambda b, i: (b, i, C_SMALL // LANES))],
        out_specs=pl.BlockSpec((1, tq, NSA_Q_W), lambda b, i: (b, i, 0)),
        out_shape=jax.ShapeDtypeStruct((bsz, s, NSA_Q_W), BF16),
        compiler_params=_cparams("arbitrary", "arbitrary"),
        name="nsa_sw",
    )(p16, p16, p16, p16, p16, mask, expand, toep, o_c, p32)


def _deltanet_kernel(q_ref, k_ref, v_ref, z_ref, sm_ref, cw_ref, alog_ref, dt_ref, ng_ref, o_ref, xp_ref, st_ref):
    c = q_ref.shape[1]
    pad = 8
    ch = pl.program_id(1)

    @pl.when(ch == 0)
    def _():
        xp_ref[:, 0:pad, :] = jnp.zeros((3, pad, DN_W), F32)
        st_ref[...] = jnp.zeros(st_ref.shape, F32)

    conv = []
    for idx, r in enumerate((q_ref, k_ref, v_ref)):
        xp_ref[idx, pad:pad + c, :] = r[0]
        acc = None
        for j in range(DN_CONV):
            off = pad - (DN_CONV - 1) + j
            term = cw_ref[j:j + 1, idx * DN_W:(idx + 1) * DN_W] * xp_ref[idx, off:off + c, :]
            acc = term if acc is None else acc + term
        xp_ref[idx, 0:pad, :] = xp_ref[idx, c:c + pad, :]
        conv.append(_silu(acc))
    qa, ka, va = conv
    za = _silu(z_ref[0])

    sm = sm_ref[0]
    beta_all = _sigmoid(sm)
    g_all = -jnp.exp(alog_ref[...]) * _softplus(sm + dt_ref[...])
    row = lax.broadcasted_iota(jnp.int32, (c, c), 0)
    col = lax.broadcasted_iota(jnp.int32, (c, c), 1)
    tril = row >= col
    strict = row > col
    eye = row == col
    tril01 = jnp.where(tril, 1.0, 0.0).astype(BF16)
    ones01 = jnp.ones((c, c), BF16)
    gc_all = _dot_01(tril01, g_all)
    eg_all = jnp.exp(gc_all)
    eye_f = jnp.where(eye, 1.0, 0.0)
    ng = ng_ref[...]

    outs = []
    for h in range(DN_HEADS):
        hs = slice(h * DN_KDIM, (h + 1) * DN_KDIM)
        q = qa[:, hs]
        k = ka[:, hs]
        v = va[:, hs]
        q = q * lax.rsqrt(jnp.sum(q * q, axis=-1, keepdims=True) + EPS) * (DN_KDIM ** -0.5)
        k = k * lax.rsqrt(jnp.sum(k * k, axis=-1, keepdims=True) + EPS)
        beta = beta_all[:, SM_B + h:SM_B + h + 1]
        gcol = gc_all[:, SM_A + h:SM_A + h + 1]
        egcol = eg_all[:, SM_A + h:SM_A + h + 1]
        grow = _dot_01(ones01, jnp.where(eye, gcol, 0.0))
        decay = jnp.where(tril, jnp.exp(jnp.where(tril, gcol - grow, 0.0)), 0.0)
        glast = grow[:, c - 1:c]
        kb = k * beta
        vb = v * beta
        kbf = k.astype(BF16)
        a = jnp.where(strict, _dot_nt(kb.astype(BF16), kbf) * decay, 0.0)
        t_inv = eye_f - a
        pw = a
        for _ in range(int(math.log2(c)) - 1):
            pw = _dot_x3(pw, pw)
            t_inv = t_inv + _dot_x3(t_inv, pw)
        rhs = jnp.concatenate([vb, kb * egcol], axis=1)
        sol = _dot_x3(t_inv, rhs)
        u = sol[:, :DN_VDIM]
        w = sol[:, DN_VDIM:]
        attn = _dot_nt(q.astype(BF16), kbf) * decay
        q_dec = q * egcol
        k_dec = k * jnp.exp(glast - gcol)
        state = st_ref[h]
        sb = state.astype(BF16)
        v_new = u - _dot(w.astype(BF16), sb)
        o = _dot(q_dec.astype(BF16), sb) + _dot(attn.astype(BF16), v_new.astype(BF16))
        st_ref[h] = state * jnp.exp(glast[0:1, :]) + _dot_tn(k_dec.astype(BF16), v_new.astype(BF16))
        on = o * lax.rsqrt(jnp.mean(o * o, axis=-1, keepdims=True) + EPS) * ng
        outs.append(on * za[:, hs])
    o_ref[0] = jnp.concatenate(outs, axis=1).astype(o_ref.dtype)


def _deltanet(p32, conv_w, alog_row, dt_row, norm_g):
    bsz, s, _ = p32.shape
    c = DN_CHUNK
    blk = lambda col: pl.BlockSpec((1, c, DN_W), lambda b, i: (b, i, col // DN_W))
    full2 = lambda a: pl.BlockSpec(a.shape, lambda b, i: (0, 0))
    ng = norm_g.reshape(1, DN_VDIM)
    return pl.pallas_call(
        _deltanet_kernel,
        grid=(bsz, s // c),
        in_specs=[blk(C_DNQ), blk(C_DNK), blk(C_DNV), blk(C_DNZ),
                  pl.BlockSpec((1, c, LANES), lambda b, i: (b, i, C_SMALL // LANES)),
                  full2(conv_w), full2(alog_row), full2(dt_row), full2(ng)],
        out_specs=pl.BlockSpec((1, c, DN_W), lambda b, i: (b, i, 0)),
        out_shape=jax.ShapeDtypeStruct((bsz, s, DN_W), BF16),
        scratch_shapes=[pltpu.VMEM((3, c + 8, DN_W), F32),
                        pltpu.VMEM((DN_HEADS, DN_KDIM, DN_VDIM), F32)],
        compiler_params=_cparams("arbitrary", "arbitrary"),
        name="deltanet",
    )(p32, p32, p32, p32, p32, conv_w, alog_row, dt_row, ng)


def _mix_out_kernel(x_ref, mod_ref, onsa_ref, odn_ref, gn_ref, gd_ref, wn_ref, wd_ref, wo_ref, o_ref):
    merged = (_sigmoid(gn_ref[0]) * _dot(onsa_ref[0], wn_ref[...])
              + _sigmoid(gd_ref[0]) * _dot(odn_ref[0], wd_ref[...]))
    y = _dot(merged.astype(BF16), wo_ref[...])
    o_ref[0] = x_ref[0] + mod_ref[0][2:3] * y


def _mix_out(x, mod, o_nsa, o_dn, p32, wn, wd, wo, tm=256):
    bsz, s, d = x.shape
    full2 = lambda a: pl.BlockSpec(a.shape, lambda b, i: (0, 0))
    return pl.pallas_call(
        _mix_out_kernel,
        grid=(bsz, s // tm),
        in_specs=[pl.BlockSpec((1, tm, d), lambda b, i: (b, i, 0)),
                  pl.BlockSpec((1, 6, d), lambda b, i: (b, 0, 0)),
                  pl.BlockSpec((1, tm, NSA_Q_W), lambda b, i: (b, i, 0)),
                  pl.BlockSpec((1, tm, DN_W), lambda b, i: (b, i, 0)),
                  pl.BlockSpec((1, tm, d), lambda b, i: (b, i, C_GNSA // d)),
                  pl.BlockSpec((1, tm, d), lambda b, i: (b, i, C_GDN // d)),
                  full2(wn), full2(wd), full2(wo)],
        out_specs=pl.BlockSpec((1, tm, d), lambda b, i: (b, i, 0)),
        out_shape=jax.ShapeDtypeStruct((bsz, s, d), F32),
        compiler_params=_cparams("arbitrary", "arbitrary"),
        name="mix_out",
    )(x, mod, o_nsa, o_dn, p32, p32, wn, wd, wo)


def _top_k_rows(s, idx, k, payload=None):
    n = s.shape[0]
    vals, picks = [], []
    for _ in range(k):
        mx = jnp.max(s, axis=0, keepdims=True)
        am = jnp.min(jnp.where(s == mx, idx, float(n)), axis=0, keepdims=True)
        hit = idx == am
        vals.append(mx)
        if payload is None:
            picks.append(am)
        else:
            picks.append(jnp.sum(jnp.where(hit, payload, 0.0), axis=0, keepdims=True))
        s = jnp.where(hit, -jnp.inf, s)
    return jnp.concatenate(vals, axis=0), jnp.concatenate(picks, axis=0)


def _peer_route_kernel(x_ref, mod_ref, g_ref, wq_ref, k1_ref, k2_ref, h_ref, idx_ref, gate_ref):
    tm = x_ref.shape[1]
    h2 = _modulated_norm(x_ref[0], g_ref[...], mod_ref[0], 3, 4)
    h_ref[0] = h2
    qry = _dot(h2.astype(BF16), wq_ref[...])
    kidx = lax.broadcasted_iota(jnp.int32, (PEER_NKEYS, tm), 0).astype(F32)
    ncand = PEER_TOPK * PEER_TOPK
    cidx = lax.broadcasted_iota(jnp.int32, (ncand, tm), 0).astype(F32)
    for h in range(PEER_HEADS):
        qh = qry[:, h * PEER_DKEY:(h + 1) * PEER_DKEY].astype(BF16)
        s1 = _dot_nt(k1_ref[h], qh[:, :PEER_HALF])
        s2 = _dot_nt(k2_ref[h], qh[:, PEER_HALF:])
        v1, i1 = _top_k_rows(s1, kidx, PEER_TOPK)
        v2, i2 = _top_k_rows(s2, kidx, PEER_TOPK)
        cand_s = jnp.concatenate([v1[a:a + 1] + v2 for a in range(PEER_TOPK)], axis=0)
        cand_i = jnp.concatenate([i1[a:a + 1] * float(PEER_NKEYS) + i2 for a in range(PEER_TOPK)], axis=0)
        top_s, top_e = _top_k_rows(cand_s, cidx, PEER_TOPK, payload=cand_i)
        e = jnp.exp(top_s - top_s[0:1])
        gate = e / jnp.sum(e, axis=0, keepdims=True)
        idx_ref[h * PEER_TOPK:(h + 1) * PEER_TOPK, :] = top_e.astype(jnp.int32)
        gate_ref[h * PEER_TOPK:(h + 1) * PEER_TOPK, :] = gate


def _peer_route(x1, mod, g, wq, k1, k2, tm=256):
    bsz, s, d = x1.shape
    nt = s // tm
    ne = PEER_HEADS * PEER_TOPK
    full = lambda a: pl.BlockSpec(a.shape, lambda b, i: (0,) * a.ndim)
    return pl.pallas_call(
        _peer_route_kernel,
        grid=(bsz, nt),
        in_specs=[pl.BlockSpec((1, tm, d), lambda b, i: (b, i, 0)),
                  pl.BlockSpec((1, 6, d), lambda b, i: (b, 0, 0)),
                  pl.BlockSpec((1, d), lambda b, i: (0, 0)),
                  full(wq), full(k1), full(k2)],
        out_specs=[pl.BlockSpec((1, tm, d), lambda b, i: (b, i, 0)),
                   pl.BlockSpec((ne, tm), lambda b, i: (0, b * nt + i)),
                   pl.BlockSpec((ne, tm), lambda b, i: (0, b * nt + i))],
        out_shape=[jax.ShapeDtypeStruct((bsz, s, d), F32),
                   jax.ShapeDtypeStruct((ne, bsz * s), jnp.int32),
                   jax.ShapeDtypeStruct((ne, bsz * s), F32)],
        compiler_params=_cparams("arbitrary", "arbitrary"),
        name="peer_route",
    )(x1, mod, g.reshape(1, d), wq, k1, k2)


PEER_TOK = 8


def _peer_ffn_kernel(idx_ref, idxn_ref, uv_hbm, h_ref, gate_ref, x_ref, mod_ref, fg_ref, o_ref, buf, sem,
                     *, rows_per_tok):
    step = pl.program_id(0)
    nstep = pl.num_programs(0)
    nrow = PEER_TOK * rows_per_tok
    d = h_ref.shape[1]
    slot = lax.rem(step, 2)

    def row_copy(i_ref, r, sl):
        return pltpu.make_async_copy(uv_hbm.at[pl.ds(i_ref[0, 0, r], 1)], buf.at[sl, pl.ds(r, 1)], sem.at[sl])

    def issue(i_ref, sl):
        def body(r, carry):
            row_copy(i_ref, r, sl).start()
            return carry
        lax.fori_loop(0, nrow, body, 0, unroll=8)

    @pl.when(step == 0)
    def _():
        issue(idx_ref, slot)

    @pl.when(step + 1 < nstep)
    def _():
        issue(idxn_ref, 1 - slot)

    pltpu.make_async_copy(uv_hbm.at[pl.ds(0, nrow)], buf.at[slot], sem.at[slot]).wait()

    g2 = mod_ref[0][5:6]
    fg = fg_ref[...]
    for t in range(PEER_TOK):
        rs = slice(t * rows_per_tok, (t + 1) * rows_per_tok)
        ug = buf[slot, rs, 0:d]
        act = _gelu_tanh(jnp.sum(ug * h_ref[t:t + 1, :], axis=1, keepdims=True))
        wgt = gate_ref[0, :, t:t + 1] * act
        y = jnp.sum(buf[slot, rs, d:2 * d] * wgt, axis=0, keepdims=True)
        xo = x_ref[t:t + 1, :] + g2 * y
        o_ref[t:t + 1, :] = xo * lax.rsqrt(jnp.mean(xo * xo, axis=-1, keepdims=True) + EPS) * fg


def _peer_ffn(idx3, uv, h2, gate3, x1, mod, final_g, s):
    n, d = h2.shape
    ne = gate3.shape[1]
    nstep = n // PEER_TOK
    nrow = PEER_TOK * ne
    tok_spec = pl.BlockSpec((PEER_TOK, d), lambda t: (t, 0))
    return pl.pallas_call(
        functools.partial(_peer_ffn_kernel, rows_per_tok=ne),
        grid=(nstep,),
        in_specs=[pl.BlockSpec((1, 1, nrow), lambda t: (t, 0, 0), memory_space=pltpu.SMEM),
                  pl.BlockSpec((1, 1, nrow), lambda t: (jnp.minimum(t + 1, nstep - 1), 0, 0),
                               memory_space=pltpu.SMEM),
                  pl.BlockSpec(memory_space=pl.ANY),
                  tok_spec,
                  pl.BlockSpec((1, ne, PEER_TOK), lambda t: (t, 0, 0)),
                  tok_spec,
                  pl.BlockSpec((1, 6, d), lambda t: ((t * PEER_TOK) // s, 0, 0)),
                  pl.BlockSpec((1, d), lambda t: (0, 0))],
        out_specs=tok_spec,
        out_shape=jax.ShapeDtypeStruct((n, d), F32),
        scratch_shapes=[pltpu.VMEM((2, nrow, 2 * d), F32), pltpu.SemaphoreType.DMA((2,))],
        compiler_params=_cparams("arbitrary"),
        name="peer_ffn",
    )(idx3, idx3, uv, h2, gate3, x1, mod, final_g.reshape(1, d))


def _rel_bucket(dist):
    dist = jnp.maximum(dist, 0)
    max_exact = REL_BUCKETS // 2
    scaled = jnp.log(jnp.maximum(dist, 1).astype(F32) / max_exact) / math.log(REL_MAX_DIST / max_exact)
    large = jnp.minimum(max_exact + (scaled * (REL_BUCKETS - max_exact)).astype(jnp.int32), REL_BUCKETS - 1)
    return jnp.where(dist < max_exact, dist, large)


def _bias_tables(rel_bias, s, ncmp):
    dist_bias = rel_bias[_rel_bucket(jnp.arange(s))].T
    t = jnp.arange(s)[:, None]
    cmp_end = jnp.arange(ncmp)[None, :] * CMP_STRIDE + CMP_LEN - 1
    bias_c = dist_bias[:, jnp.clip(t - cmp_end, 0, s - 1)]
    ii = jnp.arange(ATT_TILE)[:, None]
    jj = jnp.arange(ATT_TILE)[None, :]
    dist = jnp.stack([m * ATT_TILE + ii - jj for m in range(3)])
    toep = dist_bias[:, jnp.clip(dist, 0, s - 1)]
    return bias_c, toep


def _check_bucket_saturation():
    d = np.arange(ATT_TILE + 1, 1 << 16)
    max_exact = REL_BUCKETS // 2
    scaled = np.log(d.astype(np.float32) / max_exact) / math.log(REL_MAX_DIST / max_exact)
    large = np.minimum(max_exact + (scaled * (REL_BUCKETS - max_exact)).astype(np.int32), REL_BUCKETS - 1)
    assert (large == REL_BUCKETS - 1).all()


_check_bucket_saturation()


def _token_mixer(x, mod, rel_bias, norm_g, w_in, cmp_pos_emb, w_ck1, w_ck2, w_cv1, w_cv2, conv_w, a_log, dt_bias,
                 dn_norm_g, w_branch_nsa, w_branch_dn, w_out):
    bsz, s, d = x.shape
    assert s % ATT_TILE == 0 and s % CMP_STRIDE == 0 and WINDOW % ATT_TILE == 0 and d == C_GDN - C_GNSA
    pts = np.cumsum(IN_SPLIT_SIZES + (d, d))[:-1]
    assert w_in.shape[1] == sum(IN_SPLIT_SIZES) + 2 * d
    (w_q, w_kc, w_vc, w_ks, w_vs, w_kw, w_vw, w_gates, w_dq, w_dk, w_dv, w_dz, w_db, w_da, w_gn, w_gd) = (
        jnp.split(w_in, [int(p) for p in pts], axis=1))
    w16 = jnp.concatenate([w_q * (HEAD_DIM ** -0.5), w_ks, w_vs, w_kw, w_vw], axis=1).astype(BF16)
    small_pad = jnp.zeros((d, LANES - NSA_HEADS * 3 - 2 * DN_HEADS), w_in.dtype)
    w32 = jnp.concatenate([w_dq, w_dk, w_dv, w_dz, w_gn, w_gd, w_kc, w_vc, w_gates, w_db, w_da, small_pad],
                          axis=1).astype(BF16)
    assert w16.shape[1] == C16 and w32.shape[1] == C32
    p16, p32 = _proj_in(x, mod, norm_g, w16, w32)

    nrow = s // CMP_STRIDE
    k2 = p32[:, :, C_KC:C_KC + NSA_KV_W].reshape(bsz, nrow, CMP_STRIDE * NSA_KV_W)
    v2 = p32[:, :, C_VC:C_VC + NSA_KV_W].reshape(bsz, nrow, CMP_STRIDE * NSA_KV_W)
    kc, vc = _compress(k2, v2, cmp_pos_emb, w_ck1, w_ck2, w_cv1, w_cv2)
    nslc = s // SLC_BLOCK
    top_n = min(SLC_TOPN, nslc)
    bias_c, toep = _bias_tables(rel_bias, s, nrow)
    cmp_start = np.arange(nrow) * CMP_STRIDE
    slc_start = np.arange(nslc) * SLC_BLOCK
    ovt = ((cmp_start[None, :] <= slc_start[:, None] + SLC_BLOCK - 1)
           & (cmp_start[None, :] + CMP_LEN - 1 >= slc_start[:, None]))
    ovt = jnp.asarray(ovt, BF16)
    o_c, mask = _nsa_cmp(p16, kc, vc, bias_c, ovt, top_n)
    nkt = s // ATT_TILE
    key_blk = (np.arange(nkt)[:, None, None] * ATT_TILE + np.arange(ATT_TILE)[None, None, :]) // SLC_BLOCK
    expand = jnp.asarray(key_blk == np.arange(nslc)[None, :, None], BF16)
    o_nsa = _nsa_sw(p16, p32, mask, expand, toep, o_c)

    lane_row = lambda vec: jnp.zeros((1, LANES), F32).at[0, SM_A:SM_A + DN_HEADS].set(vec.astype(F32))
    o_dn = _deltanet(p32, conv_w, lane_row(a_log), lane_row(dt_bias), dn_norm_g)

    return _mix_out(x, mod, o_nsa, o_dn, p32, w_branch_nsa.astype(BF16), w_branch_dn.astype(BF16),
                    w_out.astype(BF16))


def _peer_block(x1, mod, norm_g, final_g, w_query, keys1, keys2, expert_u, expert_v):
    bsz, s, d = x1.shape
    n = bsz * s
    ne = PEER_HEADS * PEER_TOPK
    h2, idx_t, gate_t = _peer_route(x1, mod, norm_g, w_query.astype(BF16), keys1.astype(BF16), keys2.astype(BF16))
    nstep = n // PEER_TOK
    idx3 = idx_t.T.reshape(nstep, 1, PEER_TOK * ne)
    gate3 = gate_t.reshape(ne, nstep, PEER_TOK).transpose(1, 0, 2)
    uv = jnp.concatenate([expert_u, expert_v], axis=1)
    out = _peer_ffn(idx3, uv, h2.reshape(n, d), gate3, x1.reshape(n, d), mod, final_g, s)
    return out.reshape(bsz, s, d)


def kernel(x, c, rel_bias, final_g, w_ada, b_ada, norm1_g, w_in, cmp_pos_emb, w_cmp_k1, w_cmp_k2, w_cmp_v1,
           w_cmp_v2, dn_conv_w, dn_A_log, dn_dt_bias, dn_norm_g, w_branch_nsa, w_branch_dn, w_out, norm2_g,
           peer_w_query, peer_keys1, peer_keys2, peer_u, peer_v):
    depth = w_ada.shape[0]
    assert depth == 1, "the final norm is fused into the last PEER kernel; one layer is supported"
    bsz, s, d = x.shape
    mod = _ada(c, w_ada[0], b_ada[0]).reshape(bsz, 6, d)
    x1 = _token_mixer(x, mod, rel_bias, norm1_g[0], w_in[0], cmp_pos_emb[0], w_cmp_k1[0], w_cmp_k2[0], w_cmp_v1[0],
                      w_cmp_v2[0], dn_conv_w[0], dn_A_log[0], dn_dt_bias[0], dn_norm_g[0], w_branch_nsa[0],
                      w_branch_dn[0], w_out[0])
    return _peer_block(x1, mod, norm2_g[0], final_g, peer_w_query[0], peer_keys1[0], peer_keys2[0], peer_u[0],
                       peer_v[0])
```

```python
import functools
import math

import numpy as np
import jax
import jax.numpy as jnp
from jax import lax
from jax.experimental import pallas as pl
from jax.experimental.pallas import tpu as pltpu

F32 = jnp.float32
BF16 = jnp.bfloat16

NSA_HEADS = 8
NSA_KV_HEADS = 2
NSA_GROUP = NSA_HEADS // NSA_KV_HEADS
HEAD_DIM = 64
CMP_LEN = 32
CMP_STRIDE = 16
CMP_HIDDEN = 64
SLC_BLOCK = 64
SLC_TOPN = 16
WINDOW = 512
FORCED_BONUS = 1.0e4
DN_HEADS = 8
DN_KDIM = 64
DN_VDIM = 64
DN_CONV = 4
DN_CHUNK = 64
PEER_HEADS = 8
PEER_NKEYS = 128
PEER_DKEY = 128
PEER_HALF = PEER_DKEY // 2
PEER_TOPK = 16
REL_BUCKETS = 32
REL_MAX_DIST = 128
EPS = 1e-6
NEG_INF = -1e30

NSA_Q_W = NSA_HEADS * HEAD_DIM
NSA_KV_W = NSA_KV_HEADS * HEAD_DIM
DN_W = DN_HEADS * DN_KDIM
IN_SPLIT_SIZES = (NSA_Q_W, NSA_KV_W, NSA_KV_W, NSA_KV_W, NSA_KV_W, NSA_KV_W, NSA_KV_W, NSA_HEADS * 3,
                  DN_W, DN_W, DN_W, DN_W, DN_HEADS, DN_HEADS)

LANES = 128
ATT_TILE = 128
VMEM_LIMIT = 56 * 1024 * 1024

C_DNQ, C_DNK, C_DNV, C_DNZ = 0, 512, 1024, 1536
C_GNSA, C_GDN = 2048, 3072
C_KC, C_VC, C_SMALL = 4096, 4224, 4352
C32 = 4480
SM_GATES, SM_B, SM_A = 0, 24, 32
C16 = 1024


def _gelu_tanh(x):
    return 0.5 * x * (1.0 + jnp.tanh(math.sqrt(2.0 / math.pi) * (x + 0.044715 * (x * x * x))))


def _sigmoid(x):
    return 1.0 / (1.0 + jnp.exp(-x))


def _silu(x):
    return x * _sigmoid(x)


def _softplus(x):
    return jnp.maximum(x, 0.0) + jnp.log1p(jnp.exp(-jnp.abs(x)))


def _dot(a, b):
    return jnp.dot(a, b, preferred_element_type=F32)


def _dot_nt(a, b):
    return lax.dot_general(a, b, (((1,), (1,)), ((), ())), preferred_element_type=F32)


def _dot_tn(a, b):
    return lax.dot_general(a, b, (((0,), (0,)), ((), ())), preferred_element_type=F32)


def _split2(x):
    hi = x.astype(BF16)
    lo = (x - hi.astype(F32)).astype(BF16)
    return hi, lo


def _split3(x):
    hi = x.astype(BF16)
    r = x - hi.astype(F32)
    mid = r.astype(BF16)
    lo = (r - mid.astype(F32)).astype(BF16)
    return hi, mid, lo


def _dot_x3(a, b):
    ah, al = _split2(a)
    bh, bl = _split2(b)
    return _dot(ah, bh) + (_dot(ah, bl) + _dot(al, bh))


def _dot_01(m01, x):
    hi, mid, lo = _split3(x)
    return _dot(m01, hi) + (_dot(m01, mid) + _dot(m01, lo))


def _cparams(*sem):
    return pltpu.CompilerParams(dimension_semantics=sem, vmem_limit_bytes=VMEM_LIMIT)


def _ada_kernel(c_ref, w_ref, b_ref, o_ref):
    cond = _silu(c_ref[...])
    o_ref[...] = _dot(cond.astype(BF16), w_ref[...].astype(BF16)) + b_ref[...]


def _ada(c, w_ada, b_ada):
    bsz, d = c.shape
    n_out = w_ada.shape[1]
    blk = d
    return pl.pallas_call(
        _ada_kernel,
        grid=(n_out // blk,),
        in_specs=[pl.BlockSpec((bsz, d), lambda j: (0, 0)),
                  pl.BlockSpec((d, blk), lambda j: (0, j)),
                  pl.BlockSpec((1, blk), lambda j: (0, j))],
        out_specs=pl.BlockSpec((bsz, blk), lambda j: (0, j)),
        out_shape=jax.ShapeDtypeStruct((bsz, n_out), F32),
        compiler_params=_cparams("arbitrary"),
        name="ada",
    )(c, w_ada, b_ada.reshape(1, n_out))


def _modulated_norm(x, g, mod, shift_row, scale_row):
    var = jnp.mean(x * x, axis=-1, keepdims=True)
    y = x * lax.rsqrt(var + EPS) * g
    return y * (1.0 + mod[scale_row:scale_row + 1]) + mod[shift_row:shift_row + 1]


def _proj_in_kernel(x_ref, mod_ref, g_ref, w16_ref, w32_ref, o16_ref, o32_ref):
    h = _modulated_norm(x_ref[0], g_ref[...], mod_ref[0], 0, 1).astype(BF16)
    step = 512
    for c0 in range(0, o16_ref.shape[2], step):
        c1 = min(c0 + step, o16_ref.shape[2])
        o16_ref[0, :, c0:c1] = _dot(h, w16_ref[:, c0:c1]).astype(BF16)
    for c0 in range(0, o32_ref.shape[2], step):
        c1 = min(c0 + step, o32_ref.shape[2])
        o32_ref[0, :, c0:c1] = _dot(h, w32_ref[:, c0:c1])


def _proj_in(x, mod, g, w16, w32, tm=256):
    bsz, s, d = x.shape
    return pl.pallas_call(
        _proj_in_kernel,
        grid=(bsz, s // tm),
        in_specs=[pl.BlockSpec((1, tm, d), lambda b, i: (b, i, 0)),
                  pl.BlockSpec((1, 6, d), lambda b, i: (b, 0, 0)),
                  pl.BlockSpec((1, d), lambda b, i: (0, 0)),
                  pl.BlockSpec(w16.shape, lambda b, i: (0, 0)),
                  pl.BlockSpec(w32.shape, lambda b, i: (0, 0))],
        out_specs=[pl.BlockSpec((1, tm, w16.shape[1]), lambda b, i: (b, i, 0)),
                   pl.BlockSpec((1, tm, w32.shape[1]), lambda b, i: (b, i, 0))],
        out_shape=[jax.ShapeDtypeStruct((bsz, s, w16.shape[1]), BF16),
                   jax.ShapeDtypeStruct((bsz, s, w32.shape[1]), F32)],
        compiler_params=_cparams("arbitrary", "arbitrary"),
        name="proj_in",
    )(x, mod, g.reshape(1, d), w16, w32)


def _compress_kernel(k2_ref, v2_ref, posa_ref, posb_ref, ak_ref, bk_ref, w2k_ref, av_ref, bv_ref, w2v_ref,
                     kc_ref, vc_ref):
    nrow = k2_ref.shape[1]

    def one(x_ref, a_ref, b_ref, w2_ref, o_ref):
        x = x_ref[0]
        ya = _dot((x + posa_ref[...]).astype(BF16), a_ref[...])
        yb = _dot((x + posb_ref[...]).astype(BF16), b_ref[...])
        hid = _gelu_tanh(ya + pltpu.roll(yb, nrow - 1, 0))
        out = _dot(hid.astype(BF16), w2_ref[...])
        for h in range(NSA_KV_HEADS):
            o_ref[0, h] = out[:, h * HEAD_DIM:(h + 1) * HEAD_DIM]

    one(k2_ref, ak_ref, bk_ref, w2k_ref, kc_ref)
    one(v2_ref, av_ref, bv_ref, w2v_ref, vc_ref)


def _expand_cmp_weights(w1, w2):
    half = CMP_LEN // 2
    w1r = w1.reshape(2, half, HEAD_DIM, CMP_HIDDEN)
    eye = jnp.eye(NSA_KV_HEADS, dtype=w1.dtype)
    exp = jnp.einsum('pldj,hg->plhdgj', w1r, eye).reshape(2, half * NSA_KV_HEADS * HEAD_DIM,
                                                          NSA_KV_HEADS * CMP_HIDDEN)
    w2e = jnp.einsum('jd,hg->hjgd', w2, eye).reshape(NSA_KV_HEADS * CMP_HIDDEN, NSA_KV_HEADS * HEAD_DIM)
    return exp[0].astype(BF16), exp[1].astype(BF16), w2e.astype(BF16)


def _compress(k2, v2, pos_emb, w_ck1, w_ck2, w_cv1, w_cv2):
    bsz, nrow, width = k2.shape
    half = CMP_LEN // 2
    pos = jnp.broadcast_to(pos_emb.reshape(2, half, 1, HEAD_DIM), (2, half, NSA_KV_HEADS, HEAD_DIM))
    pos = pos.reshape(2, 1, width)
    ak, bk, w2k = _expand_cmp_weights(w_ck1, w_ck2)
    av, bv, w2v = _expand_cmp_weights(w_cv1, w_cv2)
    full2 = lambda a: pl.BlockSpec(a.shape, lambda b: (0, 0))
    out_sds = jax.ShapeDtypeStruct((bsz, NSA_KV_HEADS, nrow, HEAD_DIM), F32)
    out_spec = pl.BlockSpec((1, NSA_KV_HEADS, nrow, HEAD_DIM), lambda b: (b, 0, 0, 0))
    return pl.pallas_call(
        _compress_kernel,
        grid=(bsz,),
        in_specs=[pl.BlockSpec((1, nrow, width), lambda b: (b, 0, 0)),
                  pl.BlockSpec((1, nrow, width), lambda b: (b, 0, 0)),
                  full2(pos[0]), full2(pos[1]), full2(ak), full2(bk), full2(w2k), full2(av), full2(bv), full2(w2v)],
        out_specs=[out_spec, out_spec],
        out_shape=[out_sds, out_sds],
        compiler_params=_cparams("arbitrary"),
        name="compress",
    )(k2, v2, pos[0], pos[1], ak, bk, w2k, av, bv, w2v)


def _nsa_cmp_kernel(q_ref, kc_ref, vc_ref, bias_ref, ovt_ref, oc_ref, mask_ref, *, top_n):
    tq = q_ref.shape[1]
    ncmp = kc_ref.shape[2]
    nslc = ovt_ref.shape[0]
    s0 = pl.program_id(2) * tq
    kcb = kc_ref[0, 0].astype(BF16)
    vcb = vc_ref[0, 0].astype(BF16)
    t_q = s0 + lax.broadcasted_iota(jnp.int32, (tq, ncmp), 0)
    n_i = lax.broadcasted_iota(jnp.int32, (tq, ncmp), 1)
    valid = (n_i * CMP_STRIDE + (CMP_LEN - 1)) <= t_q
    ovt = ovt_ref[...]
    imp_t = jnp.zeros((nslc, tq), F32)
    outs = []
    for g in range(NSA_GROUP):
        qg = q_ref[0, :, g * HEAD_DIM:(g + 1) * HEAD_DIM]
        logit = _dot_nt(qg, kcb) + bias_ref[g]
        l = jnp.where(valid, logit, NEG_INF)
        m = jnp.max(l, axis=-1, keepdims=True)
        e = jnp.where(valid, jnp.exp(l - m), 0.0)
        p = e / jnp.maximum(jnp.sum(e, axis=-1, keepdims=True), 1e-30)
        pb = p.astype(BF16)
        outs.append(_dot(pb, vcb))
        imp_t = imp_t + _dot_nt(ovt, pb)
    oc_ref[0] = jnp.concatenate(outs, axis=1)

    t_l = s0 + lax.broadcasted_iota(jnp.int32, (nslc, tq), 1)
    j_s = lax.broadcasted_iota(jnp.int32, (nslc, tq), 0)
    cur = lax.shift_right_logical(t_l, int(math.log2(SLC_BLOCK)))
    forced = (j_s == 0) | (j_s == cur) | (j_s == cur - 1)
    future = (j_s * SLC_BLOCK) > t_l
    score = jnp.where(future, NEG_INF, imp_t + jnp.where(forced, FORCED_BONUS, 0.0))
    rank = jnp.zeros((nslc, tq), F32)
    for i in range(nslc):
        r = score[i:i + 1, :]
        beats = (r > score) | ((r == score) & (j_s > i))
        rank = rank + jnp.where(beats, 1.0, 0.0)
    sel_t = jnp.where(rank < float(top_n), 1.0, 0.0).astype(BF16)
    eye = (lax.broadcasted_iota(jnp.int32, (tq, tq), 0) == lax.broadcasted_iota(jnp.int32, (tq, tq), 1))
    mask_ref[0, 0] = _dot_nt(jnp.where(eye, 1.0, 0.0).astype(BF16), sel_t).astype(BF16)


def _nsa_cmp(p16, kc, vc, bias_c, ovt, top_n):
    bsz, s, _ = p16.shape
    tq = ATT_TILE
    ncmp = kc.shape[2]
    nslc = ovt.shape[0]
    qw = NSA_GROUP * HEAD_DIM
    return pl.pallas_call(
        functools.partial(_nsa_cmp_kernel, top_n=top_n),
        grid=(bsz, NSA_KV_HEADS, s // tq),
        in_specs=[pl.BlockSpec((1, tq, qw), lambda b, h, i: (b, i, h)),
                  pl.BlockSpec((1, 1, ncmp, HEAD_DIM), lambda b, h, i: (b, h, 0, 0)),
                  pl.BlockSpec((1, 1, ncmp, HEAD_DIM), lambda b, h, i: (b, h, 0, 0)),
                  pl.BlockSpec((NSA_GROUP, tq, ncmp), lambda b, h, i: (h, i, 0)),
                  pl.BlockSpec(ovt.shape, lambda b, h, i: (0, 0))],
        out_specs=[pl.BlockSpec((1, tq, qw), lambda b, h, i: (b, i, h)),
                   pl.BlockSpec((1, 1, tq, nslc), lambda b, h, i: (b, h, i, 0))],
        out_shape=[jax.ShapeDtypeStruct((bsz, s, NSA_Q_W), F32),
                   jax.ShapeDtypeStruct((bsz, NSA_KV_HEADS, s, nslc), BF16)],
        compiler_params=_cparams("arbitrary", "arbitrary", "arbitrary"),
        name="nsa_cmp",
    )(p16, kc, vc, bias_c, ovt)


def _nsa_sw_kernel(q_ref, ks_ref, vs_ref, kw_ref, vw_ref, mask_ref, exp_ref, toep_ref, oc_ref, sm_ref, o_ref):
    tq = q_ref.shape[1]
    tk = ATT_TILE
    i = pl.program_id(1)
    s0 = i * tq
    rows = NSA_GROUP * tq
    gate = _sigmoid(sm_ref[0])
    t_q = s0 + lax.broadcasted_iota(jnp.int32, (tq, tk), 0)
    k_l = lax.broadcasted_iota(jnp.int32, (tq, tk), 1)
    n_win = WINDOW // tk
    pieces = []
    for h in range(NSA_KV_HEADS):
        hs = slice(h * HEAD_DIM, (h + 1) * HEAD_DIM)
        q4 = jnp.concatenate([q_ref[0, :, (h * NSA_GROUP + g) * HEAD_DIM:(h * NSA_GROUP + g + 1) * HEAD_DIM]
                              for g in range(NSA_GROUP)], axis=0)
        mq = mask_ref[0, h]

        def tile_update(kt, carry, k_ref, v_ref, valid_fn):
            m, l, acc = carry
            k = k_ref[0, pl.ds(pl.multiple_of(kt * tk, tk), tk), hs]
            v = v_ref[0, pl.ds(pl.multiple_of(kt * tk, tk), tk), hs]
            mm = jnp.minimum(i - kt, 2)
            bias = jnp.concatenate([toep_ref[h * NSA_GROUP + g, mm] for g in range(NSA_GROUP)], axis=0)
            sc = _dot_nt(q4, k) + bias
            valid = valid_fn(kt)
            valid4 = jnp.concatenate([valid] * NSA_GROUP, axis=0)
            lg = jnp.where(valid4, sc, NEG_INF)
            m_new = jnp.maximum(m, jnp.max(lg, axis=-1, keepdims=True))
            alpha = jnp.exp(m - m_new)
            e = jnp.where(valid4, jnp.exp(lg - m_new), 0.0)
            l_new = alpha * l + jnp.sum(e, axis=-1, keepdims=True)
            acc_new = alpha * acc + _dot(e.astype(BF16), v)
            return m_new, l_new, acc_new

        def sel_valid(kt):
            mk = _dot(mq, exp_ref[kt])
            kpos = kt * tk + k_l
            return (mk > 0.5) & (kpos <= t_q)

        def win_valid(kt):
            rel = t_q - (kt * tk + k_l)
            return (rel >= 0) & (rel < WINDOW)

        init = (jnp.full((rows, 1), NEG_INF, F32), jnp.zeros((rows, 1), F32), jnp.zeros((rows, HEAD_DIM), F32))
        _, l_s, acc_s = lax.fori_loop(
            0, i + 1, lambda kt, c: tile_update(kt, c, ks_ref, vs_ref, sel_valid), init)
        _, l_w, acc_w = lax.fori_loop(
            jnp.maximum(i - n_win, 0), i + 1, lambda kt, c: tile_update(kt, c, kw_ref, vw_ref, win_valid), init)
        o_s = acc_s / jnp.maximum(l_s, 1e-30)
        o_w = acc_w / jnp.maximum(l_w, 1e-30)
        for g in range(NSA_GROUP):
            hd = h * NSA_GROUP + g
            c = SM_GATES + hd * 3
            o_c = oc_ref[0, :, hd * HEAD_DIM:(hd + 1) * HEAD_DIM]
            pieces.append(gate[:, c:c + 1] * o_c + gate[:, c + 1:c + 2] * o_s[g * tq:(g + 1) * tq]
                          + gate[:, c + 2:c + 3] * o_w[g * tq:(g + 1) * tq])
    o_ref[0] = jnp.concatenate(pieces, axis=1).astype(o_ref.dtype)


def _nsa_sw(p16, p32, mask, expand, toep, o_c):
    bsz, s, _ = p16.shape
    tq = ATT_TILE
    nslc = mask.shape[3]
    kvb = NSA_Q_W // NSA_KV_W
    return pl.pallas_call(
        _nsa_sw_kernel,
        grid=(bsz, s // tq),
        in_specs=[pl.BlockSpec((1, tq, NSA_Q_W), lambda b, i: (b, i, 0)),
                  pl.BlockSpec((1, s, NSA_KV_W), lambda b, i: (b, 0, kvb)),
                  pl.BlockSpec((1, s, NSA_KV_W), lambda b, i: (b, 0, kvb + 1)),
                  pl.BlockSpec((1, s, NSA_KV_W), lambda b, i: (b, 0, kvb + 2)),
                  pl.BlockSpec((1, s, NSA_KV_W), lambda b, i: (b, 0, kvb + 3)),
                  pl.BlockSpec((1, NSA_KV_HEADS, tq, nslc), lambda b, i: (b, 0, i, 0)),
                  pl.BlockSpec(expand.shape, lambda b, i: (0, 0, 0)),
                  pl.BlockSpec(toep.shape, lambda b, i: (0, 0, 0, 0)),
                  pl.BlockSpec((1, tq, NSA_Q_W), lambda b, i: (b, i, 0)),
                  pl.BlockSpec((1, tq, LANES), lambda b, i: (b, i, C_SMALL // LANES))],
        out_specs=pl.BlockSpec((1, tq, NSA_Q_W), lambda b, i: (b, i, 0)),
        out_shape=jax.ShapeDtypeStruct((bsz, s, NSA_Q_W), BF16),
        compiler_params=_cparams("arbitrary", "arbitrary"),
        name="nsa_sw",
    )(p16, p16, p16, p16, p16, mask, expand, toep, o_c, p32)


def _deltanet_kernel(q_ref, k_ref, v_ref, z_ref, sm_ref, cw_ref, alog_ref, dt_ref, ng_ref, o_ref, xp_ref, st_ref):
    c = q_ref.shape[1]
    pad = 8
    ch = pl.program_id(1)

    @pl.when(ch == 0)
    def _():
        xp_ref[:, 0:pad, :] = jnp.zeros((3, pad, DN_W), F32)
        st_ref[...] = jnp.zeros(st_ref.shape, F32)

    conv = []
    for idx, r in enumerate((q_ref, k_ref, v_ref)):
        xp_ref[idx, pad:pad + c, :] = r[0]
        acc = None
        for j in range(DN_CONV):
            off = pad - (DN_CONV - 1) + j
            term = cw_ref[j:j + 1, idx * DN_W:(idx + 1) * DN_W] * xp_ref[idx, off:off + c, :]
            acc = term if acc is None else acc + term
        xp_ref[idx, 0:pad, :] = xp_ref[idx, c:c + pad, :]
        conv.append(_silu(acc))
    qa, ka, va = conv
    za = _silu(z_ref[0])

    sm = sm_ref[0]
    beta_all = _sigmoid(sm)
    g_all = -jnp.exp(alog_ref[...]) * _softplus(sm + dt_ref[...])
    row = lax.broadcasted_iota(jnp.int32, (c, c), 0)
    col = lax.broadcasted_iota(jnp.int32, (c, c), 1)
    tril = row >= col
    strict = row > col
    eye = row == col
    tril01 = jnp.where(tril, 1.0, 0.0).astype(BF16)
    ones01 = jnp.ones((c, c), BF16)
    gc_all = _dot_01(tril01, g_all)
    eg_all = jnp.exp(gc_all)
    eye_f = jnp.where(eye, 1.0, 0.0)
    ng = ng_ref[...]

    outs = []
    for h in range(DN_HEADS):
        hs = slice(h * DN_KDIM, (h + 1) * DN_KDIM)
        q = qa[:, hs]
        k = ka[:, hs]
        v = va[:, hs]
        q = q * lax.rsqrt(jnp.sum(q * q, axis=-1, keepdims=True) + EPS) * (DN_KDIM ** -0.5)
        k = k * lax.rsqrt(jnp.sum(k * k, axis=-1, keepdims=True) + EPS)
        beta = beta_all[:, SM_B + h:SM_B + h + 1]
        gcol = gc_all[:, SM_A + h:SM_A + h + 1]
        egcol = eg_all[:, SM_A + h:SM_A + h + 1]
        grow = _dot_01(ones01, jnp.where(eye, gcol, 0.0))
        decay = jnp.where(tril, jnp.exp(jnp.where(tril, gcol - grow, 0.0)), 0.0)
        glast = grow[:, c - 1:c]
        kb = k * beta
        vb = v * beta
        kbf = k.astype(BF16)
        a = jnp.where(strict, _dot_nt(kb.astype(BF16), kbf) * decay, 0.0)
        t_inv = eye_f - a
        pw = a
        for _ in range(int(math.log2(c)) - 1):
            pw = _dot_x3(pw, pw)
            t_inv = t_inv + _dot_x3(t_inv, pw)
        rhs = jnp.concatenate([vb, kb * egcol], axis=1)
        sol = _dot_x3(t_inv, rhs)
        u = sol[:, :DN_VDIM]
        w = sol[:, DN_VDIM:]
        attn = _dot_nt(q.astype(BF16), kbf) * decay
        q_dec = q * egcol
        k_dec = k * jnp.exp(glast - gcol)
        state = st_ref[h]
        sb = state.astype(BF16)
        v_new = u - _dot(w.astype(BF16), sb)
        o = _dot(q_dec.astype(BF16), sb) + _dot(attn.astype(BF16), v_new.astype(BF16))
        st_ref[h] = state * jnp.exp(glast[0:1, :]) + _dot_tn(k_dec.astype(BF16), v_new.astype(BF16))
        on = o * lax.rsqrt(jnp.mean(o * o, axis=-1, keepdims=True) + EPS) * ng
        outs.append(on * za[:, hs])
    o_ref[0] = jnp.concatenate(outs, axis=1).astype(o_ref.dtype)


def _deltanet(p32, conv_w, alog_row, dt_row, norm_g):
    bsz, s, _ = p32.shape
    c = DN_CHUNK
    blk = lambda col: pl.BlockSpec((1, c, DN_W), lambda b, i: (b, i, col // DN_W))
    full2 = lambda a: pl.BlockSpec(a.shape, lambda b, i: (0, 0))
    ng = norm_g.reshape(1, DN_VDIM)
    return pl.pallas_call(
        _deltanet_kernel,
        grid=(bsz, s // c),
        in_specs=[blk(C_DNQ), blk(C_DNK), blk(C_DNV), blk(C_DNZ),
                  pl.BlockSpec((1, c, LANES), lambda b, i: (b, i, C_SMALL // LANES)),
                  full2(conv_w), full2(alog_row), full2(dt_row), full2(ng)],
        out_specs=pl.BlockSpec((1, c, DN_W), lambda b, i: (b, i, 0)),
        out_shape=jax.ShapeDtypeStruct((bsz, s, DN_W), BF16),
        scratch_shapes=[pltpu.VMEM((3, c + 8, DN_W), F32),
                        pltpu.VMEM((DN_HEADS, DN_KDIM, DN_VDIM), F32)],
        compiler_params=_cparams("arbitrary", "arbitrary"),
        name="deltanet",
    )(p32, p32, p32, p32, p32, conv_w, alog_row, dt_row, ng)


def _mix_out_kernel(x_ref, mod_ref, onsa_ref, odn_ref, gn_ref, gd_ref, wn_ref, wd_ref, wo_ref, o_ref):
    merged = (_sigmoid(gn_ref[0]) * _dot(onsa_ref[0], wn_ref[...])
              + _sigmoid(gd_ref[0]) * _dot(odn_ref[0], wd_ref[...]))
    y = _dot(merged.astype(BF16), wo_ref[...])
    o_ref[0] = x_ref[0] + mod_ref[0][2:3] * y


def _mix_out(x, mod, o_nsa, o_dn, p32, wn, wd, wo, tm=256):
    bsz, s, d = x.shape
    full2 = lambda a: pl.BlockSpec(a.shape, lambda b, i: (0, 0))
    return pl.pallas_call(
        _mix_out_kernel,
        grid=(bsz, s // tm),
        in_specs=[pl.BlockSpec((1, tm, d), lambda b, i: (b, i, 0)),
                  pl.BlockSpec((1, 6, d), lambda b, i: (b, 0, 0)),
                  pl.BlockSpec((1, tm, NSA_Q_W), lambda b, i: (b, i, 0)),
                  pl.BlockSpec((1, tm, DN_W), lambda b, i: (b, i, 0)),
                  pl.BlockSpec((1, tm, d), lambda b, i: (b, i, C_GNSA // d)),
                  pl.BlockSpec((1, tm, d), lambda b, i: (b, i, C_GDN // d)),
                  full2(wn), full2(wd), full2(wo)],
        out_specs=pl.BlockSpec((1, tm, d), lambda b, i: (b, i, 0)),
        out_shape=jax.ShapeDtypeStruct((bsz, s, d), F32),
        compiler_params=_cparams("arbitrary", "arbitrary"),
        name="mix_out",
    )(x, mod, o_nsa, o_dn, p32, p32, wn, wd, wo)


def _top_k_rows(s, idx, k, payload=None):
    n = s.shape[0]
    vals, picks = [], []
    for _ in range(k):
        mx = jnp.max(s, axis=0, keepdims=True)
        am = jnp.min(jnp.where(s == mx, idx, float(n)), axis=0, keepdims=True)
        hit = idx == am
        vals.append(mx)
        if payload is None:
            picks.append(am)
        else:
            picks.append(jnp.sum(jnp.where(hit, payload, 0.0), axis=0, keepdims=True))
        s = jnp.where(hit, -jnp.inf, s)
    return jnp.concatenate(vals, axis=0), jnp.concatenate(picks, axis=0)


def _peer_route_kernel(x_ref, mod_ref, g_ref, wq_ref, k1_ref, k2_ref, h_ref, idx_ref, gate_ref):
    tm = x_ref.shape[1]
    h2 = _modulated_norm(x_ref[0], g_ref[...], mod_ref[0], 3, 4)
    hb = h2.astype(BF16)
    h_ref[0] = hb
    qry = _dot(hb, wq_ref[...])
    kidx = lax.broadcasted_iota(jnp.int32, (PEER_NKEYS, tm), 0).astype(F32)
    ncand = PEER_TOPK * PEER_TOPK
    cidx = lax.broadcasted_iota(jnp.int32, (ncand, tm), 0).astype(F32)
    for h in range(PEER_HEADS):
        qh = qry[:, h * PEER_DKEY:(h + 1) * PEER_DKEY].astype(BF16)
        s1 = _dot_nt(k1_ref[h], qh[:, :PEER_HALF])
        s2 = _dot_nt(k2_ref[h], qh[:, PEER_HALF:])
        v1, i1 = _top_k_rows(s1, kidx, PEER_TOPK)
        v2, i2 = _top_k_rows(s2, kidx, PEER_TOPK)
        cand_s = jnp.concatenate([v1[a:a + 1] + v2 for a in range(PEER_TOPK)], axis=0)
        cand_i = jnp.concatenate([i1[a:a + 1] * float(PEER_NKEYS) + i2 for a in range(PEER_TOPK)], axis=0)
        top_s, top_e = _top_k_rows(cand_s, cidx, PEER_TOPK, payload=cand_i)
        e = jnp.exp(top_s - top_s[0:1])
        gate = e / jnp.sum(e, axis=0, keepdims=True)
        idx_ref[h * PEER_TOPK:(h + 1) * PEER_TOPK, :] = (top_e * float(PACKED_ROWS)).astype(jnp.int32)
        gate_ref[h * PEER_TOPK:(h + 1) * PEER_TOPK, :] = gate


def _peer_route(x1, mod, g, wq, k1, k2, tm=256):
    bsz, s, d = x1.shape
    nt = s // tm
    ne = PEER_HEADS * PEER_TOPK
    full = lambda a: pl.BlockSpec(a.shape, lambda b, i: (0,) * a.ndim)
    return pl.pallas_call(
        _peer_route_kernel,
        grid=(bsz, nt),
        in_specs=[pl.BlockSpec((1, tm, d), lambda b, i: (b, i, 0)),
                  pl.BlockSpec((1, 6, d), lambda b, i: (b, 0, 0)),
                  pl.BlockSpec((1, d), lambda b, i: (0, 0)),
                  full(wq), full(k1), full(k2)],
        out_specs=[pl.BlockSpec((1, tm, d), lambda b, i: (b, i, 0)),
                   pl.BlockSpec((ne, tm), lambda b, i: (0, b * nt + i)),
                   pl.BlockSpec((ne, tm), lambda b, i: (0, b * nt + i))],
        out_shape=[jax.ShapeDtypeStruct((bsz, s, d), BF16),
                   jax.ShapeDtypeStruct((ne, bsz * s), jnp.int32),
                   jax.ShapeDtypeStruct((ne, bsz * s), F32)],
        compiler_params=_cparams("arbitrary", "arbitrary"),
        name="peer_route",
    )(x1, mod, g.reshape(1, d), wq, k1, k2)


PEER_TOK = 16
EXPERT_TILE_ROWS = 8
PACKED_ROWS = EXPERT_TILE_ROWS // 2


def _load_table_once(tab_hbm, tab_vmem, sem):
    @pl.when(pl.program_id(0) == 0)
    def _():
        cp = pltpu.make_async_copy(tab_hbm, tab_vmem, sem)
        cp.start()
        cp.wait()


def _pack_table(tab):
    e = tab.shape[0]
    t16 = tab.astype(BF16).reshape(e * PACKED_ROWS, 2, LANES).transpose(0, 2, 1)
    return lax.bitcast_convert_type(t16, jnp.int32)


def _stage_experts(idx_ref, tab_vmem, stage, nsel):
    for r in range(nsel):
        row = pl.multiple_of(idx_ref[0, 0, r], PACKED_ROWS)
        stage[r * PACKED_ROWS:(r + 1) * PACKED_ROWS, :] = tab_vmem[pl.ds(row, PACKED_ROWS), :]


def _staged_tiles(stage, t, ncol):
    return pltpu.bitcast(stage[t * (ncol // 2):(t + 1) * (ncol // 2), :], BF16)


def _diag_mask(ncol):
    row = lax.broadcasted_iota(jnp.int32, (EXPERT_TILE_ROWS, ncol), 0)
    col = lax.broadcasted_iota(jnp.int32, (EXPERT_TILE_ROWS, ncol), 1)
    return (col & (EXPERT_TILE_ROWS - 1)) == row


def _peer_act_kernel(idx_ref, tab_hbm, h_ref, gate_ref, grp_ref, w_ref, tab_vmem, stage, sem, *, rows_per_tok):
    _load_table_once(tab_hbm, tab_vmem, sem)
    _stage_experts(idx_ref, tab_vmem, stage, PEER_TOK * rows_per_tok)
    ncol = rows_per_tok * EXPERT_TILE_ROWS
    diag = _diag_mask(ncol)
    sums = []
    for t in range(PEER_TOK):
        m = _staged_tiles(stage, t, ncol)
        rt = _dot_nt(h_ref[t], m)
        sums.append(jnp.sum(jnp.where(diag, rt, 0.0), axis=0, keepdims=True))
    part = jnp.concatenate(sums, axis=0)
    hi, mid, lo = _split3(part)
    grp = grp_ref[...]
    act = _dot(hi, grp) + (_dot(mid, grp) + _dot(lo, grp))
    w_ref[...] = gate_ref[...] * _gelu_tanh(act)


def _peer_out_kernel(idx_ref, tab_hbm, w_ref, exp_ref, x_ref, mod_ref, fg_ref, o_ref, tab_vmem, stage, sem,
                     *, rows_per_tok):
    _load_table_once(tab_hbm, tab_vmem, sem)
    _stage_experts(idx_ref, tab_vmem, stage, PEER_TOK * rows_per_tok)
    ncol = rows_per_tok * EXPERT_TILE_ROWS
    diag = _diag_mask(ncol)
    wh, wl = _split2(w_ref[...])
    ex = exp_ref[...]
    weh = _dot(wh, ex)
    wel = _dot(wl, ex)
    g2 = mod_ref[0, 5]
    fg = fg_ref[...]
    nel = float(EXPERT_TILE_ROWS * LANES)
    for t in range(PEER_TOK):
        lhs = jnp.concatenate([jnp.where(diag, weh[t:t + 1], 0.0), jnp.where(diag, wel[t:t + 1], 0.0)], axis=0)
        o16 = _dot(lhs.astype(BF16), _staged_tiles(stage, t, ncol))
        y = o16[0:EXPERT_TILE_ROWS] + o16[EXPERT_TILE_ROWS:]
        xo = x_ref[t] + g2 * y
        ms = jnp.sum(jnp.sum(xo * xo, axis=1, keepdims=True), axis=0, keepdims=True) / nel
        o_ref[t] = xo * lax.rsqrt(ms + EPS) * fg


def _peer_experts(idx3, tab_u, tab_v, h3, gate, x3, mod4, fg3, s):
    n = h3.shape[0]
    ne = gate.shape[1]
    nstep = n // PEER_TOK
    nsel = PEER_TOK * ne
    ncol = ne * EXPERT_TILE_ROWS
    grp = jnp.asarray(np.arange(ncol)[:, None] // EXPERT_TILE_ROWS == np.arange(ne)[None, :], BF16)
    idx_spec = pl.BlockSpec((1, 1, nsel), lambda t: (t, 0, 0), memory_space=pltpu.SMEM)
    any_spec = pl.BlockSpec(memory_space=pl.ANY)
    tile_spec = pl.BlockSpec((PEER_TOK, EXPERT_TILE_ROWS, LANES), lambda t: (t, 0, 0))
    row_spec = pl.BlockSpec((PEER_TOK, ne), lambda t: (t, 0))
    scratch = [pltpu.VMEM(tab_u.shape, jnp.int32), pltpu.VMEM((PEER_TOK * ncol // 2, LANES), jnp.int32),
               pltpu.SemaphoreType.DMA]
    w = pl.pallas_call(
        functools.partial(_peer_act_kernel, rows_per_tok=ne),
        grid=(nstep,),
        in_specs=[idx_spec, any_spec, tile_spec, row_spec, pl.BlockSpec(grp.shape, lambda t: (0, 0))],
        out_specs=row_spec,
        out_shape=jax.ShapeDtypeStruct((n, ne), F32),
        scratch_shapes=scratch,
        compiler_params=_cparams("arbitrary"),
        name="peer_act",
    )(idx3, tab_u, h3, gate, grp)
    return pl.pallas_call(
        functools.partial(_peer_out_kernel, rows_per_tok=ne),
        grid=(nstep,),
        in_specs=[idx_spec, any_spec, row_spec, pl.BlockSpec((ne, ncol), lambda t: (0, 0)), tile_spec,
                  pl.BlockSpec((1, 6, EXPERT_TILE_ROWS, LANES), lambda t: ((t * PEER_TOK) // s, 0, 0, 0)),
                  pl.BlockSpec((EXPERT_TILE_ROWS, LANES), lambda t: (0, 0))],
        out_specs=tile_spec,
        out_shape=jax.ShapeDtypeStruct((n, EXPERT_TILE_ROWS, LANES), F32),
        scratch_shapes=scratch,
        compiler_params=_cparams("arbitrary"),
        name="peer_out",
    )(idx3, tab_v, w, grp.T, x3, mod4, fg3)


def _rel_bucket(dist):
    dist = jnp.maximum(dist, 0)
    max_exact = REL_BUCKETS // 2
    scaled = jnp.log(jnp.maximum(dist, 1).astype(F32) / max_exact) / math.log(REL_MAX_DIST / max_exact)
    large = jnp.minimum(max_exact + (scaled * (REL_BUCKETS - max_exact)).astype(jnp.int32), REL_BUCKETS - 1)
    return jnp.where(dist < max_exact, dist, large)


def _bias_tables(rel_bias, s, ncmp):
    dist_bias = rel_bias[_rel_bucket(jnp.arange(s))].T
    nq = s // ATT_TILE
    per_tile = ATT_TILE // CMP_STRIDE
    r = np.arange(ATT_TILE)[:, None]
    k = np.arange(per_tile * (nq - 1) + ncmp)[None, :]
    dist_g = r + CMP_STRIDE * (k - (ncmp - 1)) - (CMP_LEN - 1)
    g = dist_bias[:, np.clip(dist_g, 0, s - 1)]
    wins = jnp.stack([g[:, :, per_tile * i:per_tile * i + ncmp] for i in range(nq)], axis=1)
    bias_c = wins[..., ::-1].reshape(rel_bias.shape[1], s, ncmp)
    ii = jnp.arange(ATT_TILE)[:, None]
    jj = jnp.arange(ATT_TILE)[None, :]
    dist = jnp.stack([m * ATT_TILE + ii - jj for m in range(3)])
    toep = dist_bias[:, jnp.clip(dist, 0, s - 1)]
    return bias_c, toep


def _check_bucket_saturation():
    d = np.arange(ATT_TILE + 1, 1 << 16)
    max_exact = REL_BUCKETS // 2
    scaled = np.log(d.astype(np.float32) / max_exact) / math.log(REL_MAX_DIST / max_exact)
    large = np.minimum(max_exact + (scaled * (REL_BUCKETS - max_exact)).astype(np.int32), REL_BUCKETS - 1)
    assert (large == REL_BUCKETS - 1).all()


_check_bucket_saturation()


def _token_mixer(x, mod, rel_bias, norm_g, w_in, cmp_pos_emb, w_ck1, w_ck2, w_cv1, w_cv2, conv_w, a_log, dt_bias,
                 dn_norm_g, w_branch_nsa, w_branch_dn, w_out):
    bsz, s, d = x.shape
    assert s % ATT_TILE == 0 and s % CMP_STRIDE == 0 and WINDOW % ATT_TILE == 0 and d == C_GDN - C_GNSA
    pts = np.cumsum(IN_SPLIT_SIZES + (d, d))[:-1]
    assert w_in.shape[1] == sum(IN_SPLIT_SIZES) + 2 * d
    (w_q, w_kc, w_vc, w_ks, w_vs, w_kw, w_vw, w_gates, w_dq, w_dk, w_dv, w_dz, w_db, w_da, w_gn, w_gd) = (
        jnp.split(w_in, [int(p) for p in pts], axis=1))
    w16 = jnp.concatenate([w_q * (HEAD_DIM ** -0.5), w_ks, w_vs, w_kw, w_vw], axis=1).astype(BF16)
    small_pad = jnp.zeros((d, LANES - NSA_HEADS * 3 - 2 * DN_HEADS), w_in.dtype)
    w32 = jnp.concatenate([w_dq, w_dk, w_dv, w_dz, w_gn, w_gd, w_kc, w_vc, w_gates, w_db, w_da, small_pad],
                          axis=1).astype(BF16)
    assert w16.shape[1] == C16 and w32.shape[1] == C32
    p16, p32 = _proj_in(x, mod, norm_g, w16, w32)

    nrow = s // CMP_STRIDE
    k2 = p32[:, :, C_KC:C_KC + NSA_KV_W].reshape(bsz, nrow, CMP_STRIDE * NSA_KV_W)
    v2 = p32[:, :, C_VC:C_VC + NSA_KV_W].reshape(bsz, nrow, CMP_STRIDE * NSA_KV_W)
    kc, vc = _compress(k2, v2, cmp_pos_emb, w_ck1, w_ck2, w_cv1, w_cv2)
    nslc = s // SLC_BLOCK
    top_n = min(SLC_TOPN, nslc)
    bias_c, toep = _bias_tables(rel_bias, s, nrow)
    cmp_start = np.arange(nrow) * CMP_STRIDE
    slc_start = np.arange(nslc) * SLC_BLOCK
    ovt = ((cmp_start[None, :] <= slc_start[:, None] + SLC_BLOCK - 1)
           & (cmp_start[None, :] + CMP_LEN - 1 >= slc_start[:, None]))
    ovt = jnp.asarray(ovt, BF16)
    o_c, mask = _nsa_cmp(p16, kc, vc, bias_c, ovt, top_n)
    nkt = s // ATT_TILE
    key_blk = (np.arange(nkt)[:, None, None] * ATT_TILE + np.arange(ATT_TILE)[None, None, :]) // SLC_BLOCK
    expand = jnp.asarray(key_blk == np.arange(nslc)[None, :, None], BF16)
    o_nsa = _nsa_sw(p16, p32, mask, expand, toep, o_c)

    lane_row = lambda vec: jnp.zeros((1, LANES), F32).at[0, SM_A:SM_A + DN_HEADS].set(vec.astype(F32))
    o_dn = _deltanet(p32, conv_w, lane_row(a_log), lane_row(dt_bias), dn_norm_g)

    return _mix_out(x, mod, o_nsa, o_dn, p32, w_branch_nsa.astype(BF16), w_branch_dn.astype(BF16),
                    w_out.astype(BF16))


def _peer_block(x1, mod, norm_g, final_g, w_query, keys1, keys2, expert_u, expert_v):
    bsz, s, d = x1.shape
    n = bsz * s
    ne = PEER_HEADS * PEER_TOPK
    h2, idx_t, gate_t = _peer_route(x1, mod, norm_g, w_query.astype(BF16), keys1.astype(BF16), keys2.astype(BF16))
    assert d == EXPERT_TILE_ROWS * LANES
    nstep = n // PEER_TOK
    idx3 = idx_t.T.reshape(nstep, 1, PEER_TOK * ne)
    tile = (EXPERT_TILE_ROWS, LANES)
    tab_u = _pack_table(expert_u)
    tab_v = _pack_table(expert_v)
    out = _peer_experts(idx3, tab_u, tab_v, h2.reshape((n,) + tile), gate_t.T, x1.reshape((n,) + tile),
                        mod.reshape((bsz, 6) + tile), final_g.reshape(tile), s)
    return out.reshape(bsz, s, d)


def kernel(x, c, rel_bias, final_g, w_ada, b_ada, norm1_g, w_in, cmp_pos_emb, w_cmp_k1, w_cmp_k2, w_cmp_v1,
           w_cmp_v2, dn_conv_w, dn_A_log, dn_dt_bias, dn_norm_g, w_branch_nsa, w_branch_dn, w_out, norm2_g,
           peer_w_query, peer_keys1, peer_keys2, peer_u, peer_v):
    depth = w_ada.shape[0]
    assert depth == 1, "the final norm is fused into the last PEER kernel; one layer is supported"
    bsz, s, d = x.shape
    mod = _ada(c, w_ada[0], b_ada[0]).reshape(bsz, 6, d)
    x1 = _token_mixer(x, mod, rel_bias, norm1_g[0], w_in[0], cmp_pos_emb[0], w_cmp_k1[0], w_cmp_k2[0], w_cmp_v1[0],
                      w_cmp_v2[0], dn_conv_w[0], dn_A_log[0], dn_dt_bias[0], dn_norm_g[0], w_branch_nsa[0],
                      w_branch_dn[0], w_out[0])
    return _peer_block(x1, mod, norm2_g[0], final_g, peer_w_query[0], peer_keys1[0], peer_keys2[0], peer_u[0],
                       peer_v[0])
```

```python
import functools
import math

import numpy as np
import jax
import jax.numpy as jnp
from jax import lax
from jax.experimental import pallas as pl
from jax.experimental.pallas import tpu as pltpu

F32 = jnp.float32
BF16 = jnp.bfloat16

NSA_HEADS = 8
NSA_KV_HEADS = 2
NSA_GROUP = NSA_HEADS // NSA_KV_HEADS
HEAD_DIM = 64
CMP_LEN = 32
CMP_STRIDE = 16
CMP_HIDDEN = 64
SLC_BLOCK = 64
SLC_TOPN = 16
WINDOW = 512
FORCED_BONUS = 1.0e4
DN_HEADS = 8
DN_KDIM = 64
DN_VDIM = 64
DN_CONV = 4
DN_CHUNK = 64
PEER_HEADS = 8
PEER_NKEYS = 128
PEER_DKEY = 128
PEER_HALF = PEER_DKEY // 2
PEER_TOPK = 16
REL_BUCKETS = 32
REL_MAX_DIST = 128
EPS = 1e-6
NEG_INF = -1e30

NSA_Q_W = NSA_HEADS * HEAD_DIM
NSA_KV_W = NSA_KV_HEADS * HEAD_DIM
DN_W = DN_HEADS * DN_KDIM
IN_SPLIT_SIZES = (NSA_Q_W, NSA_KV_W, NSA_KV_W, NSA_KV_W, NSA_KV_W, NSA_KV_W, NSA_KV_W, NSA_HEADS * 3,
                  DN_W, DN_W, DN_W, DN_W, DN_HEADS, DN_HEADS)

LANES = 128
ATT_TILE = 128
VMEM_LIMIT = 56 * 1024 * 1024

C_DNQ, C_DNK, C_DNV, C_DNZ = 0, 512, 1024, 1536
C_GNSA, C_GDN = 2048, 3072
C_KC, C_VC, C_SMALL = 4096, 4224, 4352
C32 = 4480
SM_GATES, SM_B, SM_A = 0, 24, 32
C16 = 1024


def _gelu_tanh(x):
    return 0.5 * x * (1.0 + jnp.tanh(math.sqrt(2.0 / math.pi) * (x + 0.044715 * (x * x * x))))


def _sigmoid(x):
    return 1.0 / (1.0 + jnp.exp(-x))


def _silu(x):
    return x * _sigmoid(x)


def _softplus(x):
    return jnp.maximum(x, 0.0) + jnp.log1p(jnp.exp(-jnp.abs(x)))


def _dot(a, b):
    return jnp.dot(a, b, preferred_element_type=F32)


def _dot_nt(a, b):
    return lax.dot_general(a, b, (((1,), (1,)), ((), ())), preferred_element_type=F32)


def _dot_tn(a, b):
    return lax.dot_general(a, b, (((0,), (0,)), ((), ())), preferred_element_type=F32)


def _split2(x):
    hi = x.astype(BF16)
    lo = (x - hi.astype(F32)).astype(BF16)
    return hi, lo


def _split3(x):
    hi = x.astype(BF16)
    r = x - hi.astype(F32)
    mid = r.astype(BF16)
    lo = (r - mid.astype(F32)).astype(BF16)
    return hi, mid, lo


def _dot_x3(a, b):
    ah, al = _split2(a)
    bh, bl = _split2(b)
    return _dot(ah, bh) + (_dot(ah, bl) + _dot(al, bh))


def _dot_01(m01, x):
    hi, mid, lo = _split3(x)
    return _dot(m01, hi) + (_dot(m01, mid) + _dot(m01, lo))


def _dot_x3_many(a_list, b_list):
    n = range(len(a_list))
    asp = [_split2(a) for a in a_list]
    bsp = [_split2(b) for b in b_list]
    hh = [_dot(asp[i][0], bsp[i][0]) for i in n]
    hl = [_dot(asp[i][0], bsp[i][1]) for i in n]
    lh = [_dot(asp[i][1], bsp[i][0]) for i in n]
    return [hh[i] + (hl[i] + lh[i]) for i in n]


def _dot_01_many(m01, x_list):
    n = range(len(x_list))
    sp = [_split3(x) for x in x_list]
    d0 = [_dot(m01, sp[i][0]) for i in n]
    d1 = [_dot(m01, sp[i][1]) for i in n]
    d2 = [_dot(m01, sp[i][2]) for i in n]
    return [d0[i] + (d1[i] + d2[i]) for i in n]


def _cparams(*sem):
    return pltpu.CompilerParams(dimension_semantics=sem, vmem_limit_bytes=VMEM_LIMIT)


def _ada_kernel(c_ref, w_ref, b_ref, o_ref):
    cond = _silu(c_ref[...])
    o_ref[...] = _dot(cond.astype(BF16), w_ref[...].astype(BF16)) + b_ref[...]


def _ada(c, w_ada, b_ada):
    bsz, d = c.shape
    n_out = w_ada.shape[1]
    blk = d
    return pl.pallas_call(
        _ada_kernel,
        grid=(n_out // blk,),
        in_specs=[pl.BlockSpec((bsz, d), lambda j: (0, 0)),
                  pl.BlockSpec((d, blk), lambda j: (0, j)),
                  pl.BlockSpec((1, blk), lambda j: (0, j))],
        out_specs=pl.BlockSpec((bsz, blk), lambda j: (0, j)),
        out_shape=jax.ShapeDtypeStruct((bsz, n_out), F32),
        compiler_params=_cparams("arbitrary"),
        name="ada",
    )(c, w_ada, b_ada.reshape(1, n_out))


def _modulated_norm(x, g, mod, shift_row, scale_row):
    var = jnp.mean(x * x, axis=-1, keepdims=True)
    y = x * lax.rsqrt(var + EPS) * g
    return y * (1.0 + mod[scale_row:scale_row + 1]) + mod[shift_row:shift_row + 1]


def _proj_in_kernel(x_ref, mod_ref, g_ref, w16_ref, w32_ref, o16_ref, o32_ref):
    h = _modulated_norm(x_ref[0], g_ref[...], mod_ref[0], 0, 1).astype(BF16)
    step = 512
    for c0 in range(0, o16_ref.shape[2], step):
        c1 = min(c0 + step, o16_ref.shape[2])
        o16_ref[0, :, c0:c1] = _dot(h, w16_ref[:, c0:c1]).astype(BF16)
    for c0 in range(0, o32_ref.shape[2], step):
        c1 = min(c0 + step, o32_ref.shape[2])
        o32_ref[0, :, c0:c1] = _dot(h, w32_ref[:, c0:c1])


def _proj_in(x, mod, g, w16, w32, tm=256):
    bsz, s, d = x.shape
    return pl.pallas_call(
        _proj_in_kernel,
        grid=(bsz, s // tm),
        in_specs=[pl.BlockSpec((1, tm, d), lambda b, i: (b, i, 0)),
                  pl.BlockSpec((1, 6, d), lambda b, i: (b, 0, 0)),
                  pl.BlockSpec((1, d), lambda b, i: (0, 0)),
                  pl.BlockSpec(w16.shape, lambda b, i: (0, 0)),
                  pl.BlockSpec(w32.shape, lambda b, i: (0, 0))],
        out_specs=[pl.BlockSpec((1, tm, w16.shape[1]), lambda b, i: (b, i, 0)),
                   pl.BlockSpec((1, tm, w32.shape[1]), lambda b, i: (b, i, 0))],
        out_shape=[jax.ShapeDtypeStruct((bsz, s, w16.shape[1]), BF16),
                   jax.ShapeDtypeStruct((bsz, s, w32.shape[1]), F32)],
        compiler_params=_cparams("arbitrary", "arbitrary"),
        name="proj_in",
    )(x, mod, g.reshape(1, d), w16, w32)


def _compress_kernel(k2_ref, v2_ref, posa_ref, posb_ref, ak_ref, bk_ref, w2k_ref, av_ref, bv_ref, w2v_ref,
                     kc_ref, vc_ref):
    nrow = k2_ref.shape[1]

    def one(x_ref, a_ref, b_ref, w2_ref, o_ref):
        x = x_ref[0]
        ya = _dot((x + posa_ref[...]).astype(BF16), a_ref[...])
        yb = _dot((x + posb_ref[...]).astype(BF16), b_ref[...])
        hid = _gelu_tanh(ya + pltpu.roll(yb, nrow - 1, 0))
        out = _dot(hid.astype(BF16), w2_ref[...])
        for h in range(NSA_KV_HEADS):
            o_ref[0, h] = out[:, h * HEAD_DIM:(h + 1) * HEAD_DIM]

    one(k2_ref, ak_ref, bk_ref, w2k_ref, kc_ref)
    one(v2_ref, av_ref, bv_ref, w2v_ref, vc_ref)


def _expand_cmp_weights(w1, w2):
    half = CMP_LEN // 2
    w1r = w1.reshape(2, half, HEAD_DIM, CMP_HIDDEN)
    eye = jnp.eye(NSA_KV_HEADS, dtype=w1.dtype)
    exp = jnp.einsum('pldj,hg->plhdgj', w1r, eye).reshape(2, half * NSA_KV_HEADS * HEAD_DIM,
                                                          NSA_KV_HEADS * CMP_HIDDEN)
    w2e = jnp.einsum('jd,hg->hjgd', w2, eye).reshape(NSA_KV_HEADS * CMP_HIDDEN, NSA_KV_HEADS * HEAD_DIM)
    return exp[0].astype(BF16), exp[1].astype(BF16), w2e.astype(BF16)


def _compress(k2, v2, pos_emb, w_ck1, w_ck2, w_cv1, w_cv2):
    bsz, nrow, width = k2.shape
    half = CMP_LEN // 2
    pos = jnp.broadcast_to(pos_emb.reshape(2, half, 1, HEAD_DIM), (2, half, NSA_KV_HEADS, HEAD_DIM))
    pos = pos.reshape(2, 1, width)
    ak, bk, w2k = _expand_cmp_weights(w_ck1, w_ck2)
    av, bv, w2v = _expand_cmp_weights(w_cv1, w_cv2)
    full2 = lambda a: pl.BlockSpec(a.shape, lambda b: (0, 0))
    out_sds = jax.ShapeDtypeStruct((bsz, NSA_KV_HEADS, nrow, HEAD_DIM), F32)
    out_spec = pl.BlockSpec((1, NSA_KV_HEADS, nrow, HEAD_DIM), lambda b: (b, 0, 0, 0))
    return pl.pallas_call(
        _compress_kernel,
        grid=(bsz,),
        in_specs=[pl.BlockSpec((1, nrow, width), lambda b: (b, 0, 0)),
                  pl.BlockSpec((1, nrow, width), lambda b: (b, 0, 0)),
                  full2(pos[0]), full2(pos[1]), full2(ak), full2(bk), full2(w2k), full2(av), full2(bv), full2(w2v)],
        out_specs=[out_spec, out_spec],
        out_shape=[out_sds, out_sds],
        compiler_params=_cparams("arbitrary"),
        name="compress",
    )(k2, v2, pos[0], pos[1], ak, bk, w2k, av, bv, w2v)


def _nsa_cmp_kernel(q_ref, kc_ref, vc_ref, bias_ref, ovt_ref, oc_ref, mask_ref, *, top_n):
    tq = q_ref.shape[1]
    ncmp = kc_ref.shape[2]
    nslc = ovt_ref.shape[0]
    s0 = pl.program_id(2) * tq
    kcb = kc_ref[0, 0].astype(BF16)
    vcb = vc_ref[0, 0].astype(BF16)
    t_q = s0 + lax.broadcasted_iota(jnp.int32, (tq, ncmp), 0)
    n_i = lax.broadcasted_iota(jnp.int32, (tq, ncmp), 1)
    valid = (n_i * CMP_STRIDE + (CMP_LEN - 1)) <= t_q
    ovt = ovt_ref[...]
    imp_t = jnp.zeros((nslc, tq), F32)
    outs = []
    for g in range(NSA_GROUP):
        qg = q_ref[0, :, g * HEAD_DIM:(g + 1) * HEAD_DIM]
        logit = _dot_nt(qg, kcb) + bias_ref[g]
        l = jnp.where(valid, logit, NEG_INF)
        m = jnp.max(l, axis=-1, keepdims=True)
        e = jnp.where(valid, jnp.exp(l - m), 0.0)
        p = e / jnp.maximum(jnp.sum(e, axis=-1, keepdims=True), 1e-30)
        pb = p.astype(BF16)
        outs.append(_dot(pb, vcb))
        imp_t = imp_t + _dot_nt(ovt, pb)
    oc_ref[0] = jnp.concatenate(outs, axis=1)

    t_l = s0 + lax.broadcasted_iota(jnp.int32, (nslc, tq), 1)
    j_s = lax.broadcasted_iota(jnp.int32, (nslc, tq), 0)
    cur = lax.shift_right_logical(t_l, int(math.log2(SLC_BLOCK)))
    forced = (j_s == 0) | (j_s == cur) | (j_s == cur - 1)
    future = (j_s * SLC_BLOCK) > t_l
    score = jnp.where(future, NEG_INF, imp_t + jnp.where(forced, FORCED_BONUS, 0.0))
    rank = jnp.zeros((nslc, tq), F32)
    for i in range(nslc):
        r = score[i:i + 1, :]
        beats = (r > score) | ((r == score) & (j_s > i))
        rank = rank + jnp.where(beats, 1.0, 0.0)
    sel_t = jnp.where(rank < float(top_n), 1.0, 0.0).astype(BF16)
    eye = (lax.broadcasted_iota(jnp.int32, (tq, tq), 0) == lax.broadcasted_iota(jnp.int32, (tq, tq), 1))
    mask_ref[0, 0] = _dot_nt(jnp.where(eye, 1.0, 0.0).astype(BF16), sel_t).astype(BF16)


def _nsa_cmp(p16, kc, vc, bias_c, ovt, top_n):
    bsz, s, _ = p16.shape
    tq = ATT_TILE
    ncmp = kc.shape[2]
    nslc = ovt.shape[0]
    qw = NSA_GROUP * HEAD_DIM
    return pl.pallas_call(
        functools.partial(_nsa_cmp_kernel, top_n=top_n),
        grid=(bsz, NSA_KV_HEADS, s // tq),
        in_specs=[pl.BlockSpec((1, tq, qw), lambda b, h, i: (b, i, h)),
                  pl.BlockSpec((1, 1, ncmp, HEAD_DIM), lambda b, h, i: (b, h, 0, 0)),
                  pl.BlockSpec((1, 1, ncmp, HEAD_DIM), lambda b, h, i: (b, h, 0, 0)),
                  pl.BlockSpec((NSA_GROUP, tq, ncmp), lambda b, h, i: (h, i, 0)),
                  pl.BlockSpec(ovt.shape, lambda b, h, i: (0, 0))],
        out_specs=[pl.BlockSpec((1, tq, qw), lambda b, h, i: (b, i, h)),
                   pl.BlockSpec((1, 1, tq, nslc), lambda b, h, i: (b, h, i, 0))],
        out_shape=[jax.ShapeDtypeStruct((bsz, s, NSA_Q_W), F32),
                   jax.ShapeDtypeStruct((bsz, NSA_KV_HEADS, s, nslc), BF16)],
        compiler_params=_cparams("arbitrary", "arbitrary", "arbitrary"),
        name="nsa_cmp",
    )(p16, kc, vc, bias_c, ovt)


def _nsa_sw_kernel(q_ref, ks_ref, vs_ref, kw_ref, vw_ref, mask_ref, exp_ref, toep_ref, oc_ref, sm_ref, o_ref):
    tq = q_ref.shape[1]
    tk = ATT_TILE
    i = pl.program_id(1)
    s0 = i * tq
    gate = _sigmoid(sm_ref[0])
    t_q = s0 + lax.broadcasted_iota(jnp.int32, (tq, tk), 0)
    k_l = lax.broadcasted_iota(jnp.int32, (tq, tk), 1)
    n_win = WINDOW // tk
    pieces = []
    for h in range(NSA_KV_HEADS):
        hs = slice(h * HEAD_DIM, (h + 1) * HEAD_DIM)
        qs = [q_ref[0, :, (h * NSA_GROUP + g) * HEAD_DIM:(h * NSA_GROUP + g + 1) * HEAD_DIM]
              for g in range(NSA_GROUP)]
        mq = mask_ref[0, h]

        def tile_update(kt, carry, k_ref, v_ref, valid_fn):
            k = k_ref[0, pl.ds(pl.multiple_of(kt * tk, tk), tk), hs]
            v = v_ref[0, pl.ds(pl.multiple_of(kt * tk, tk), tk), hs]
            mm = jnp.minimum(i - kt, 2)
            valid = valid_fn(kt)
            gs = range(NSA_GROUP)
            sc = [_dot_nt(qs[g], k) for g in gs]
            lg = [jnp.where(valid, sc[g] + toep_ref[h * NSA_GROUP + g, mm], NEG_INF) for g in gs]
            m_new = [jnp.maximum(carry[g][0], jnp.max(lg[g], axis=-1, keepdims=True)) for g in gs]
            alpha = [jnp.exp(carry[g][0] - m_new[g]) for g in gs]
            e = [jnp.where(valid, jnp.exp(lg[g] - m_new[g]), 0.0) for g in gs]
            l_new = [alpha[g] * carry[g][1] + jnp.sum(e[g], axis=-1, keepdims=True) for g in gs]
            pv = [_dot(e[g].astype(BF16), v) for g in gs]
            return tuple((m_new[g], l_new[g], alpha[g] * carry[g][2] + pv[g]) for g in gs)

        def sel_valid(kt):
            mk = _dot(mq, exp_ref[kt])
            kpos = kt * tk + k_l
            return (mk > 0.5) & (kpos <= t_q)

        def win_valid(kt):
            rel = t_q - (kt * tk + k_l)
            return (rel >= 0) & (rel < WINDOW)

        init = tuple((jnp.full((tq, 1), NEG_INF, F32), jnp.zeros((tq, 1), F32), jnp.zeros((tq, HEAD_DIM), F32))
                     for _ in range(NSA_GROUP))
        sel = lax.fori_loop(0, i + 1, lambda kt, c: tile_update(kt, c, ks_ref, vs_ref, sel_valid), init)
        win = lax.fori_loop(jnp.maximum(i - n_win, 0), i + 1,
                            lambda kt, c: tile_update(kt, c, kw_ref, vw_ref, win_valid), init)
        for g in range(NSA_GROUP):
            hd = h * NSA_GROUP + g
            c = SM_GATES + hd * 3
            o_c = oc_ref[0, :, hd * HEAD_DIM:(hd + 1) * HEAD_DIM]
            o_s = sel[g][2] / jnp.maximum(sel[g][1], 1e-30)
            o_w = win[g][2] / jnp.maximum(win[g][1], 1e-30)
            pieces.append(gate[:, c:c + 1] * o_c + gate[:, c + 1:c + 2] * o_s + gate[:, c + 2:c + 3] * o_w)
    o_ref[0] = jnp.concatenate(pieces, axis=1).astype(o_ref.dtype)


def _nsa_sw(p16, p32, mask, expand, toep, o_c):
    bsz, s, _ = p16.shape
    tq = ATT_TILE
    nslc = mask.shape[3]
    kvb = NSA_Q_W // NSA_KV_W
    return pl.pallas_call(
        _nsa_sw_kernel,
        grid=(bsz, s // tq),
        in_specs=[pl.BlockSpec((1, tq, NSA_Q_W), lambda b, i: (b, i, 0)),
                  pl.BlockSpec((1, s, NSA_KV_W), lambda b, i: (b, 0, kvb)),
                  pl.BlockSpec((1, s, NSA_KV_W), lambda b, i: (b, 0, kvb + 1)),
                  pl.BlockSpec((1, s, NSA_KV_W), lambda b, i: (b, 0, kvb + 2)),
                  pl.BlockSpec((1, s, NSA_KV_W), lambda b, i: (b, 0, kvb + 3)),
                  pl.BlockSpec((1, NSA_KV_HEADS, tq, nslc), lambda b, i: (b, 0, i, 0)),
                  pl.BlockSpec(expand.shape, lambda b, i: (0, 0, 0)),
                  pl.BlockSpec(toep.shape, lambda b, i: (0, 0, 0, 0)),
                  pl.BlockSpec((1, tq, NSA_Q_W), lambda b, i: (b, i, 0)),
                  pl.BlockSpec((1, tq, LANES), lambda b, i: (b, i, C_SMALL // LANES))],
        out_specs=pl.BlockSpec((1, tq, NSA_Q_W), lambda b, i: (b, i, 0)),
        out_shape=jax.ShapeDtypeStruct((bsz, s, NSA_Q_W), BF16),
        compiler_params=_cparams("arbitrary", "arbitrary"),
        name="nsa_sw",
    )(p16, p16, p16, p16, p16, mask, expand, toep, o_c, p32)


def _deltanet_kernel(q_ref, k_ref, v_ref, z_ref, sm_ref, cw_ref, alog_ref, dt_ref, ng_ref, o_ref, xp_ref, st_ref):
    c = q_ref.shape[1]
    pad = 8
    ch = pl.program_id(1)

    @pl.when(ch == 0)
    def _():
        xp_ref[:, 0:pad, :] = jnp.zeros((3, pad, DN_W), F32)
        st_ref[...] = jnp.zeros(st_ref.shape, F32)

    conv = []
    for idx, r in enumerate((q_ref, k_ref, v_ref)):
        xp_ref[idx, pad:pad + c, :] = r[0]
        acc = None
        for j in range(DN_CONV):
            off = pad - (DN_CONV - 1) + j
            term = cw_ref[j:j + 1, idx * DN_W:(idx + 1) * DN_W] * xp_ref[idx, off:off + c, :]
            acc = term if acc is None else acc + term
        xp_ref[idx, 0:pad, :] = xp_ref[idx, c:c + pad, :]
        conv.append(_silu(acc))
    qa, ka, va = conv
    za = _silu(z_ref[0])

    sm = sm_ref[0]
    beta_all = _sigmoid(sm)
    g_all = -jnp.exp(alog_ref[...]) * _softplus(sm + dt_ref[...])
    row = lax.broadcasted_iota(jnp.int32, (c, c), 0)
    col = lax.broadcasted_iota(jnp.int32, (c, c), 1)
    tril = row >= col
    strict = row > col
    eye = row == col
    tril01 = jnp.where(tril, 1.0, 0.0).astype(BF16)
    ones01 = jnp.ones((c, c), BF16)
    gc_all = _dot_01(tril01, g_all)
    eg_all = jnp.exp(gc_all)
    eye_f = jnp.where(eye, 1.0, 0.0)
    ng = ng_ref[...]

    hh = range(DN_HEADS)
    hsl = [slice(h * DN_KDIM, (h + 1) * DN_KDIM) for h in hh]
    states = [st_ref[h] for h in hh]
    q = [qa[:, hsl[h]] for h in hh]
    k = [ka[:, hsl[h]] for h in hh]
    q = [q[h] * lax.rsqrt(jnp.sum(q[h] * q[h], axis=-1, keepdims=True) + EPS) * (DN_KDIM ** -0.5) for h in hh]
    k = [k[h] * lax.rsqrt(jnp.sum(k[h] * k[h], axis=-1, keepdims=True) + EPS) for h in hh]
    beta = [beta_all[:, SM_B + h:SM_B + h + 1] for h in hh]
    gcol = [gc_all[:, SM_A + h:SM_A + h + 1] for h in hh]
    egcol = [eg_all[:, SM_A + h:SM_A + h + 1] for h in hh]
    grow = _dot_01_many(ones01, [jnp.where(eye, gcol[h], 0.0) for h in hh])
    decay = [jnp.where(tril, jnp.exp(jnp.where(tril, gcol[h] - grow[h], 0.0)), 0.0) for h in hh]
    glast = [grow[h][:, c - 1:c] for h in hh]
    kb = [k[h] * beta[h] for h in hh]
    vb = [va[:, hsl[h]] * beta[h] for h in hh]
    kbf = [k[h].astype(BF16) for h in hh]
    kk = [_dot_nt(kb[h].astype(BF16), kbf[h]) for h in hh]
    a = [jnp.where(strict, kk[h] * decay[h], 0.0) for h in hh]
    t_inv = [eye_f - a[h] for h in hh]
    pw = a
    for _ in range(int(math.log2(c)) - 1):
        pw = _dot_x3_many(pw, pw)
        tp = _dot_x3_many(t_inv, pw)
        t_inv = [t_inv[h] + tp[h] for h in hh]
    rhs = [jnp.concatenate([vb[h], kb[h] * egcol[h]], axis=1) for h in hh]
    sol = _dot_x3_many(t_inv, rhs)
    qk = [_dot_nt(q[h].astype(BF16), kbf[h]) for h in hh]
    attn = [(qk[h] * decay[h]).astype(BF16) for h in hh]
    q_dec = [(q[h] * egcol[h]).astype(BF16) for h in hh]
    k_dec = [(k[h] * jnp.exp(glast[h] - gcol[h])).astype(BF16) for h in hh]
    sb = [states[h].astype(BF16) for h in hh]
    ws = [_dot(sol[h][:, DN_VDIM:].astype(BF16), sb[h]) for h in hh]
    v_new = [(sol[h][:, :DN_VDIM] - ws[h]).astype(BF16) for h in hh]
    o1 = [_dot(q_dec[h], sb[h]) for h in hh]
    o2 = [_dot(attn[h], v_new[h]) for h in hh]
    kv = [_dot_tn(k_dec[h], v_new[h]) for h in hh]
    for h in hh:
        st_ref[h] = states[h] * jnp.exp(glast[h][0:1, :]) + kv[h]
    o = [o1[h] + o2[h] for h in hh]
    outs = [o[h] * lax.rsqrt(jnp.mean(o[h] * o[h], axis=-1, keepdims=True) + EPS) * ng * za[:, hsl[h]] for h in hh]
    o_ref[0] = jnp.concatenate(outs, axis=1).astype(o_ref.dtype)


def _deltanet(p32, conv_w, alog_row, dt_row, norm_g):
    bsz, s, _ = p32.shape
    c = DN_CHUNK
    blk = lambda col: pl.BlockSpec((1, c, DN_W), lambda b, i: (b, i, col // DN_W))
    full2 = lambda a: pl.BlockSpec(a.shape, lambda b, i: (0, 0))
    ng = norm_g.reshape(1, DN_VDIM)
    return pl.pallas_call(
        _deltanet_kernel,
        grid=(bsz, s // c),
        in_specs=[blk(C_DNQ), blk(C_DNK), blk(C_DNV), blk(C_DNZ),
                  pl.BlockSpec((1, c, LANES), lambda b, i: (b, i, C_SMALL // LANES)),
                  full2(conv_w), full2(alog_row), full2(dt_row), full2(ng)],
        out_specs=pl.BlockSpec((1, c, DN_W), lambda b, i: (b, i, 0)),
        out_shape=jax.ShapeDtypeStruct((bsz, s, DN_W), BF16),
        scratch_shapes=[pltpu.VMEM((3, c + 8, DN_W), F32),
                        pltpu.VMEM((DN_HEADS, DN_KDIM, DN_VDIM), F32)],
        compiler_params=_cparams("arbitrary", "arbitrary"),
        name="deltanet",
    )(p32, p32, p32, p32, p32, conv_w, alog_row, dt_row, ng)


def _mix_out_kernel(x_ref, mod_ref, onsa_ref, odn_ref, gn_ref, gd_ref, wn_ref, wd_ref, wo_ref, o_ref):
    merged = (_sigmoid(gn_ref[0]) * _dot(onsa_ref[0], wn_ref[...])
              + _sigmoid(gd_ref[0]) * _dot(odn_ref[0], wd_ref[...]))
    y = _dot(merged.astype(BF16), wo_ref[...])
    o_ref[0] = x_ref[0] + mod_ref[0][2:3] * y


def _mix_out(x, mod, o_nsa, o_dn, p32, wn, wd, wo, tm=256):
    bsz, s, d = x.shape
    full2 = lambda a: pl.BlockSpec(a.shape, lambda b, i: (0, 0))
    return pl.pallas_call(
        _mix_out_kernel,
        grid=(bsz, s // tm),
        in_specs=[pl.BlockSpec((1, tm, d), lambda b, i: (b, i, 0)),
                  pl.BlockSpec((1, 6, d), lambda b, i: (b, 0, 0)),
                  pl.BlockSpec((1, tm, NSA_Q_W), lambda b, i: (b, i, 0)),
                  pl.BlockSpec((1, tm, DN_W), lambda b, i: (b, i, 0)),
                  pl.BlockSpec((1, tm, d), lambda b, i: (b, i, C_GNSA // d)),
                  pl.BlockSpec((1, tm, d), lambda b, i: (b, i, C_GDN // d)),
                  full2(wn), full2(wd), full2(wo)],
        out_specs=pl.BlockSpec((1, tm, d), lambda b, i: (b, i, 0)),
        out_shape=jax.ShapeDtypeStruct((bsz, s, d), F32),
        compiler_params=_cparams("arbitrary", "arbitrary"),
        name="mix_out",
    )(x, mod, o_nsa, o_dn, p32, p32, wn, wd, wo)


def _top_k_rows(s, idx, k, payload=None):
    vals, picks = [], []
    for _ in range(k):
        mx = jnp.max(s, axis=0, keepdims=True)
        am = jnp.min(jnp.where(s == mx, idx, 1e9), axis=0, keepdims=True)
        hit = idx == am
        vals.append(mx)
        if payload is None:
            picks.append(am)
        else:
            picks.append(jnp.sum(jnp.where(hit, payload, 0.0), axis=0, keepdims=True))
        s = jnp.where(hit, -jnp.inf, s)
    return jnp.concatenate(vals, axis=0), jnp.concatenate(picks, axis=0)


def _peer_route_kernel(x_ref, mod_ref, g_ref, wq_ref, k1_ref, k2_ref, h_ref, idx_ref, gate_ref):
    tm = x_ref.shape[1]
    h2 = _modulated_norm(x_ref[0], g_ref[...], mod_ref[0], 3, 4)
    hb = h2.astype(BF16)
    h_ref[0] = hb
    qry = _dot(hb, wq_ref[...])
    kidx = lax.broadcasted_iota(jnp.int32, (PEER_NKEYS, tm), 0).astype(F32)
    kk = PEER_TOPK
    hk = kk // 2
    b_iota = kidx[0:kk]
    cidx = jnp.concatenate([b_iota] + [a * float(kk) + b_iota[0:hk] for a in range(1, hk)]
                           + [(b_iota[hk:kk]) * float(kk)], axis=0)
    for h in range(PEER_HEADS):
        qh = qry[:, h * PEER_DKEY:(h + 1) * PEER_DKEY].astype(BF16)
        s1 = _dot_nt(k1_ref[h], qh[:, :PEER_HALF])
        s2 = _dot_nt(k2_ref[h], qh[:, PEER_HALF:])
        v1, i1 = _top_k_rows(s1, kidx, PEER_TOPK)
        v2, i2 = _top_k_rows(s2, kidx, PEER_TOPK)
        cand_s = jnp.concatenate([v1[0:1] + v2] + [v1[a:a + 1] + v2[0:hk] for a in range(1, hk)]
                                 + [v1[hk:kk] + v2[0:1]], axis=0)
        e1 = i1 * float(PEER_NKEYS)
        cand_i = jnp.concatenate([e1[0:1] + i2] + [e1[a:a + 1] + i2[0:hk] for a in range(1, hk)]
                                 + [e1[hk:kk] + i2[0:1]], axis=0)
        top_s, top_e = _top_k_rows(cand_s, cidx, PEER_TOPK, payload=cand_i)
        e = jnp.exp(top_s - top_s[0:1])
        gate = e / jnp.sum(e, axis=0, keepdims=True)
        idx_ref[h * PEER_TOPK:(h + 1) * PEER_TOPK, :] = (top_e * float(PACKED_ROWS)).astype(jnp.int32)
        gate_ref[h * PEER_TOPK:(h + 1) * PEER_TOPK, :] = gate


def _peer_route(x1, mod, g, wq, k1, k2, tm=128):
    bsz, s, d = x1.shape
    nt = s // tm
    ne = PEER_HEADS * PEER_TOPK
    full = lambda a: pl.BlockSpec(a.shape, lambda b, i: (0,) * a.ndim)
    return pl.pallas_call(
        _peer_route_kernel,
        grid=(bsz, nt),
        in_specs=[pl.BlockSpec((1, tm, d), lambda b, i: (b, i, 0)),
                  pl.BlockSpec((1, 6, d), lambda b, i: (b, 0, 0)),
                  pl.BlockSpec((1, d), lambda b, i: (0, 0)),
                  full(wq), full(k1), full(k2)],
        out_specs=[pl.BlockSpec((1, tm, d), lambda b, i: (b, i, 0)),
                   pl.BlockSpec((ne, tm), lambda b, i: (0, b * nt + i)),
                   pl.BlockSpec((ne, tm), lambda b, i: (0, b * nt + i))],
        out_shape=[jax.ShapeDtypeStruct((bsz, s, d), BF16),
                   jax.ShapeDtypeStruct((ne, bsz * s), jnp.int32),
                   jax.ShapeDtypeStruct((ne, bsz * s), F32)],
        compiler_params=_cparams("arbitrary", "arbitrary"),
        name="peer_route",
    )(x1, mod, g.reshape(1, d), wq, k1, k2)


PEER_TOK = 16
EXPERT_TILE_ROWS = 8
PACKED_ROWS = EXPERT_TILE_ROWS // 2


def _load_table_once(tab_hbm, tab_vmem, sem):
    @pl.when(pl.program_id(0) == 0)
    def _():
        cp = pltpu.make_async_copy(tab_hbm, tab_vmem, sem)
        cp.start()
        cp.wait()


def _pack_table(tab):
    e = tab.shape[0]
    bits = lax.bitcast_convert_type(tab.astype(BF16).reshape(e * PACKED_ROWS, 2, LANES), jnp.uint16)
    bits = bits.astype(jnp.uint32)
    return lax.bitcast_convert_type(bits[:, 0, :] | (bits[:, 1, :] << 16), jnp.int32)


def _stage_experts(idx_ref, tab_vmem, stage, nsel):
    for r in range(nsel):
        row = pl.multiple_of(idx_ref[0, 0, r], PACKED_ROWS)
        stage[r * PACKED_ROWS:(r + 1) * PACKED_ROWS, :] = tab_vmem[pl.ds(row, PACKED_ROWS), :]


def _staged_tiles(stage, t, ncol):
    return pltpu.bitcast(stage[t * (ncol // 2):(t + 1) * (ncol // 2), :], BF16)


def _diag_mask(ncol):
    row = lax.broadcasted_iota(jnp.int32, (EXPERT_TILE_ROWS, ncol), 0)
    col = lax.broadcasted_iota(jnp.int32, (EXPERT_TILE_ROWS, ncol), 1)
    return (col & (EXPERT_TILE_ROWS - 1)) == row


def _peer_act_kernel(idx_ref, tab_hbm, h_ref, gate_ref, grp_ref, w_ref, tab_vmem, stage, sem, *, rows_per_tok):
    _load_table_once(tab_hbm, tab_vmem, sem)
    _stage_experts(idx_ref, tab_vmem, stage, PEER_TOK * rows_per_tok)
    ncol = rows_per_tok * EXPERT_TILE_ROWS
    diag = _diag_mask(ncol)
    sums = []
    for t in range(PEER_TOK):
        m = _staged_tiles(stage, t, ncol)
        rt = _dot_nt(h_ref[t], m)
        sums.append(jnp.sum(jnp.where(diag, rt, 0.0), axis=0, keepdims=True))
    part = jnp.concatenate(sums, axis=0)
    hi, mid, lo = _split3(part)
    grp = grp_ref[...]
    act = _dot(hi, grp) + (_dot(mid, grp) + _dot(lo, grp))
    w_ref[...] = gate_ref[...] * _gelu_tanh(act)


def _peer_out_kernel(idx_ref, tab_hbm, w_ref, exp_ref, x_ref, mod_ref, fg_ref, o_ref, tab_vmem, stage, sem,
                     *, rows_per_tok):
    _load_table_once(tab_hbm, tab_vmem, sem)
    _stage_experts(idx_ref, tab_vmem, stage, PEER_TOK * rows_per_tok)
    ncol = rows_per_tok * EXPERT_TILE_ROWS
    diag = _diag_mask(ncol)
    wh, wl = _split2(w_ref[...])
    ex = exp_ref[...]
    weh = _dot(wh, ex)
    wel = _dot(wl, ex)
    g2 = mod_ref[0, 5]
    fg = fg_ref[...]
    nel = float(EXPERT_TILE_ROWS * LANES)
    for t in range(PEER_TOK):
        lhs = jnp.concatenate([jnp.where(diag, weh[t:t + 1], 0.0), jnp.where(diag, wel[t:t + 1], 0.0)], axis=0)
        o16 = _dot(lhs.astype(BF16), _staged_tiles(stage, t, ncol))
        y = o16[0:EXPERT_TILE_ROWS] + o16[EXPERT_TILE_ROWS:]
        xo = x_ref[t] + g2 * y
        ms = jnp.sum(jnp.sum(xo * xo, axis=1, keepdims=True), axis=0, keepdims=True) / nel
        o_ref[t] = xo * lax.rsqrt(ms + EPS) * fg


def _peer_experts(idx3, tab_u, tab_v, h3, gate, x3, mod4, fg3, s):
    n = h3.shape[0]
    ne = gate.shape[1]
    nstep = n // PEER_TOK
    nsel = PEER_TOK * ne
    ncol = ne * EXPERT_TILE_ROWS
    grp = jnp.asarray(np.arange(ncol)[:, None] // EXPERT_TILE_ROWS == np.arange(ne)[None, :], BF16)
    idx_spec = pl.BlockSpec((1, 1, nsel), lambda t: (t, 0, 0), memory_space=pltpu.SMEM)
    any_spec = pl.BlockSpec(memory_space=pl.ANY)
    tile_spec = pl.BlockSpec((PEER_TOK, EXPERT_TILE_ROWS, LANES), lambda t: (t, 0, 0))
    row_spec = pl.BlockSpec((PEER_TOK, ne), lambda t: (t, 0))
    scratch = [pltpu.VMEM(tab_u.shape, jnp.int32), pltpu.VMEM((PEER_TOK * ncol // 2, LANES), jnp.int32),
               pltpu.SemaphoreType.DMA]
    w = pl.pallas_call(
        functools.partial(_peer_act_kernel, rows_per_tok=ne),
        grid=(nstep,),
        in_specs=[idx_spec, any_spec, tile_spec, row_spec, pl.BlockSpec(grp.shape, lambda t: (0, 0))],
        out_specs=row_spec,
        out_shape=jax.ShapeDtypeStruct((n, ne), F32),
        scratch_shapes=scratch,
        compiler_params=_cparams("arbitrary"),
        name="peer_act",
    )(idx3, tab_u, h3, gate, grp)
    return pl.pallas_call(
        functools.partial(_peer_out_kernel, rows_per_tok=ne),
        grid=(nstep,),
        in_specs=[idx_spec, any_spec, row_spec, pl.BlockSpec((ne, ncol), lambda t: (0, 0)), tile_spec,
                  pl.BlockSpec((1, 6, EXPERT_TILE_ROWS, LANES), lambda t: ((t * PEER_TOK) // s, 0, 0, 0)),
                  pl.BlockSpec((EXPERT_TILE_ROWS, LANES), lambda t: (0, 0))],
        out_specs=tile_spec,
        out_shape=jax.ShapeDtypeStruct((n, EXPERT_TILE_ROWS, LANES), F32),
        scratch_shapes=scratch,
        compiler_params=_cparams("arbitrary"),
        name="peer_out",
    )(idx3, tab_v, w, grp.T, x3, mod4, fg3)


def _rel_bucket(dist):
    dist = jnp.maximum(dist, 0)
    max_exact = REL_BUCKETS // 2
    scaled = jnp.log(jnp.maximum(dist, 1).astype(F32) / max_exact) / math.log(REL_MAX_DIST / max_exact)
    large = jnp.minimum(max_exact + (scaled * (REL_BUCKETS - max_exact)).astype(jnp.int32), REL_BUCKETS - 1)
    return jnp.where(dist < max_exact, dist, large)


def _bias_tables(rel_bias, s, ncmp):
    dist_bias = rel_bias[_rel_bucket(jnp.arange(s))].T
    nq = s // ATT_TILE
    per_tile = ATT_TILE // CMP_STRIDE
    r = np.arange(ATT_TILE)[:, None]
    k0 = per_tile * (nq - 1)
    k = np.arange(k0 + ncmp)[None, :]
    dist_g = r - CMP_STRIDE * (k - k0) - (CMP_LEN - 1)
    g = dist_bias[:, np.clip(dist_g, 0, s - 1)]
    wins = jnp.stack([g[:, :, k0 - per_tile * i:k0 - per_tile * i + ncmp] for i in range(nq)], axis=1)
    bias_c = wins.reshape(rel_bias.shape[1], s, ncmp)
    ii = jnp.arange(ATT_TILE)[:, None]
    jj = jnp.arange(ATT_TILE)[None, :]
    dist = jnp.stack([m * ATT_TILE + ii - jj for m in range(3)])
    toep = dist_bias[:, jnp.clip(dist, 0, s - 1)]
    return bias_c, toep


def _check_bucket_saturation():
    d = np.arange(ATT_TILE + 1, 1 << 16)
    max_exact = REL_BUCKETS // 2
    scaled = np.log(d.astype(np.float32) / max_exact) / math.log(REL_MAX_DIST / max_exact)
    large = np.minimum(max_exact + (scaled * (REL_BUCKETS - max_exact)).astype(np.int32), REL_BUCKETS - 1)
    assert (large == REL_BUCKETS - 1).all()


_check_bucket_saturation()


def _token_mixer(x, mod, rel_bias, norm_g, w_in, cmp_pos_emb, w_ck1, w_ck2, w_cv1, w_cv2, conv_w, a_log, dt_bias,
                 dn_norm_g, w_branch_nsa, w_branch_dn, w_out):
    bsz, s, d = x.shape
    assert s % ATT_TILE == 0 and s % CMP_STRIDE == 0 and WINDOW % ATT_TILE == 0 and d == C_GDN - C_GNSA
    pts = np.cumsum(IN_SPLIT_SIZES + (d, d))[:-1]
    assert w_in.shape[1] == sum(IN_SPLIT_SIZES) + 2 * d
    (w_q, w_kc, w_vc, w_ks, w_vs, w_kw, w_vw, w_gates, w_dq, w_dk, w_dv, w_dz, w_db, w_da, w_gn, w_gd) = (
        jnp.split(w_in, [int(p) for p in pts], axis=1))
    w16 = jnp.concatenate([w_q * (HEAD_DIM ** -0.5), w_ks, w_vs, w_kw, w_vw], axis=1).astype(BF16)
    small_pad = jnp.zeros((d, LANES - NSA_HEADS * 3 - 2 * DN_HEADS), w_in.dtype)
    w32 = jnp.concatenate([w_dq, w_dk, w_dv, w_dz, w_gn, w_gd, w_kc, w_vc, w_gates, w_db, w_da, small_pad],
                          axis=1).astype(BF16)
    assert w16.shape[1] == C16 and w32.shape[1] == C32
    p16, p32 = _proj_in(x, mod, norm_g, w16, w32)

    nrow = s // CMP_STRIDE
    k2 = p32[:, :, C_KC:C_KC + NSA_KV_W].reshape(bsz, nrow, CMP_STRIDE * NSA_KV_W)
    v2 = p32[:, :, C_VC:C_VC + NSA_KV_W].reshape(bsz, nrow, CMP_STRIDE * NSA_KV_W)
    kc, vc = _compress(k2, v2, cmp_pos_emb, w_ck1, w_ck2, w_cv1, w_cv2)
    nslc = s // SLC_BLOCK
    top_n = min(SLC_TOPN, nslc)
    bias_c, toep = _bias_tables(rel_bias, s, nrow)
    cmp_start = np.arange(nrow) * CMP_STRIDE
    slc_start = np.arange(nslc) * SLC_BLOCK
    ovt = ((cmp_start[None, :] <= slc_start[:, None] + SLC_BLOCK - 1)
           & (cmp_start[None, :] + CMP_LEN - 1 >= slc_start[:, None]))
    ovt = jnp.asarray(ovt, BF16)
    o_c, mask = _nsa_cmp(p16, kc, vc, bias_c, ovt, top_n)
    nkt = s // ATT_TILE
    key_blk = (np.arange(nkt)[:, None, None] * ATT_TILE + np.arange(ATT_TILE)[None, None, :]) // SLC_BLOCK
    expand = jnp.asarray(key_blk == np.arange(nslc)[None, :, None], BF16)
    o_nsa = _nsa_sw(p16, p32, mask, expand, toep, o_c)

    lane_row = lambda vec: jnp.zeros((1, LANES), F32).at[0, SM_A:SM_A + DN_HEADS].set(vec.astype(F32))
    o_dn = _deltanet(p32, conv_w, lane_row(a_log), lane_row(dt_bias), dn_norm_g)

    return _mix_out(x, mod, o_nsa, o_dn, p32, w_branch_nsa.astype(BF16), w_branch_dn.astype(BF16),
                    w_out.astype(BF16))


def _peer_block(x1, mod, norm_g, final_g, w_query, keys1, keys2, expert_u, expert_v):
    bsz, s, d = x1.shape
    n = bsz * s
    ne = PEER_HEADS * PEER_TOPK
    h2, idx_t, gate_t = _peer_route(x1, mod, norm_g, w_query.astype(BF16), keys1.astype(BF16), keys2.astype(BF16))
    assert d == EXPERT_TILE_ROWS * LANES
    nstep = n // PEER_TOK
    idx3 = idx_t.T.reshape(nstep, 1, PEER_TOK * ne)
    tile = (EXPERT_TILE_ROWS, LANES)
    tab_u = _pack_table(expert_u)
    tab_v = _pack_table(expert_v)
    out = _peer_experts(idx3, tab_u, tab_v, h2.reshape((n,) + tile), gate_t.T, x1.reshape((n,) + tile),
                        mod.reshape((bsz, 6) + tile), final_g.reshape(tile), s)
    return out.reshape(bsz, s, d)


def kernel(x, c, rel_bias, final_g, w_ada, b_ada, norm1_g, w_in, cmp_pos_emb, w_cmp_k1, w_cmp_k2, w_cmp_v1,
           w_cmp_v2, dn_conv_w, dn_A_log, dn_dt_bias, dn_norm_g, w_branch_nsa, w_branch_dn, w_out, norm2_g,
           peer_w_query, peer_keys1, peer_keys2, peer_u, peer_v):
    depth = w_ada.shape[0]
    assert depth == 1, "the final norm is fused into the last PEER kernel; one layer is supported"
    bsz, s, d = x.shape
    mod = _ada(c, w_ada[0], b_ada[0]).reshape(bsz, 6, d)
    x1 = _token_mixer(x, mod, rel_bias, norm1_g[0], w_in[0], cmp_pos_emb[0], w_cmp_k1[0], w_cmp_k2[0], w_cmp_v1[0],
                      w_cmp_v2[0], dn_conv_w[0], dn_A_log[0], dn_dt_bias[0], dn_norm_g[0], w_branch_nsa[0],
                      w_branch_dn[0], w_out[0])
    return _peer_block(x1, mod, norm2_g[0], final_g, peer_w_query[0], peer_keys1[0], peer_keys2[0], peer_u[0],
                       peer_v[0])
```

```python
import functools
import math

import numpy as np
import jax
import jax.numpy as jnp
from jax import lax
from jax.experimental import pallas as pl
from jax.experimental.pallas import tpu as pltpu

F32 = jnp.float32
BF16 = jnp.bfloat16

NSA_HEADS = 8
NSA_KV_HEADS = 2
NSA_GROUP = NSA_HEADS // NSA_KV_HEADS
HEAD_DIM = 64
CMP_LEN = 32
CMP_STRIDE = 16
CMP_HIDDEN = 64
SLC_BLOCK = 64
SLC_TOPN = 16
WINDOW = 512
FORCED_BONUS = 1.0e4
DN_HEADS = 8
DN_KDIM = 64
DN_VDIM = 64
DN_CONV = 4
DN_CHUNK = 64
PEER_HEADS = 8
PEER_NKEYS = 128
PEER_DKEY = 128
PEER_HALF = PEER_DKEY // 2
PEER_TOPK = 16
REL_BUCKETS = 32
REL_MAX_DIST = 128
EPS = 1e-6
NEG_INF = -1e30

NSA_Q_W = NSA_HEADS * HEAD_DIM
NSA_KV_W = NSA_KV_HEADS * HEAD_DIM
DN_W = DN_HEADS * DN_KDIM
IN_SPLIT_SIZES = (NSA_Q_W, NSA_KV_W, NSA_KV_W, NSA_KV_W, NSA_KV_W, NSA_KV_W, NSA_KV_W, NSA_HEADS * 3,
                  DN_W, DN_W, DN_W, DN_W, DN_HEADS, DN_HEADS)

LANES = 128
ATT_TILE = 128
TILES_PER_TRIP = 4
VMEM_LIMIT = 56 * 1024 * 1024

C_DNQ, C_DNK, C_DNV, C_DNZ = 0, 512, 1024, 1536
C_GNSA, C_GDN = 2048, 3072
C_KC, C_VC, C_SMALL = 4096, 4224, 4352
C32 = 4480
SM_GATES, SM_B, SM_A = 0, 24, 32
C16 = 1024


def _gelu_tanh(x):
    return 0.5 * x * (1.0 + jnp.tanh(math.sqrt(2.0 / math.pi) * (x + 0.044715 * (x * x * x))))


def _sigmoid(x):
    return 1.0 / (1.0 + jnp.exp(-x))


def _silu(x):
    return x * _sigmoid(x)


def _softplus(x):
    return jnp.maximum(x, 0.0) + jnp.log1p(jnp.exp(-jnp.abs(x)))


def _dot(a, b):
    return jnp.dot(a, b, preferred_element_type=F32)


def _dot_nt(a, b):
    return lax.dot_general(a, b, (((1,), (1,)), ((), ())), preferred_element_type=F32)


def _dot_tn(a, b):
    return lax.dot_general(a, b, (((0,), (0,)), ((), ())), preferred_element_type=F32)


def _split2(x):
    hi = x.astype(BF16)
    lo = (x - hi.astype(F32)).astype(BF16)
    return hi, lo


def _split3(x):
    hi = x.astype(BF16)
    r = x - hi.astype(F32)
    mid = r.astype(BF16)
    lo = (r - mid.astype(F32)).astype(BF16)
    return hi, mid, lo


def _dot_x3(a, b):
    ah, al = _split2(a)
    bh, bl = _split2(b)
    return _dot(ah, bh) + (_dot(ah, bl) + _dot(al, bh))


def _dot_01(m01, x):
    hi, mid, lo = _split3(x)
    return _dot(m01, hi) + (_dot(m01, mid) + _dot(m01, lo))


def _dot_x3_many(a_list, b_list):
    n = range(len(a_list))
    asp = [_split2(a) for a in a_list]
    bsp = [_split2(b) for b in b_list]
    hh = [_dot(asp[i][0], bsp[i][0]) for i in n]
    hl = [_dot(asp[i][0], bsp[i][1]) for i in n]
    lh = [_dot(asp[i][1], bsp[i][0]) for i in n]
    return [hh[i] + (hl[i] + lh[i]) for i in n]


def _dot_01_many(m01, x_list):
    n = range(len(x_list))
    sp = [_split3(x) for x in x_list]
    d0 = [_dot(m01, sp[i][0]) for i in n]
    d1 = [_dot(m01, sp[i][1]) for i in n]
    d2 = [_dot(m01, sp[i][2]) for i in n]
    return [d0[i] + (d1[i] + d2[i]) for i in n]


def _cparams(*sem):
    return pltpu.CompilerParams(dimension_semantics=sem, vmem_limit_bytes=VMEM_LIMIT)


def _ada_kernel(c_ref, w_ref, b_ref, o_ref):
    cond = _silu(c_ref[...])
    o_ref[...] = _dot(cond.astype(BF16), w_ref[...].astype(BF16)) + b_ref[...]


def _ada(c, w_ada, b_ada):
    bsz, d = c.shape
    n_out = w_ada.shape[1]
    blk = d
    return pl.pallas_call(
        _ada_kernel,
        grid=(n_out // blk,),
        in_specs=[pl.BlockSpec((bsz, d), lambda j: (0, 0)),
                  pl.BlockSpec((d, blk), lambda j: (0, j)),
                  pl.BlockSpec((1, blk), lambda j: (0, j))],
        out_specs=pl.BlockSpec((bsz, blk), lambda j: (0, j)),
        out_shape=jax.ShapeDtypeStruct((bsz, n_out), F32),
        compiler_params=_cparams("arbitrary"),
        name="ada",
    )(c, w_ada, b_ada.reshape(1, n_out))


def _modulated_norm(x, g, mod, shift_row, scale_row):
    var = jnp.mean(x * x, axis=-1, keepdims=True)
    y = x * lax.rsqrt(var + EPS) * g
    return y * (1.0 + mod[scale_row:scale_row + 1]) + mod[shift_row:shift_row + 1]


def _proj_in_kernel(x_ref, mod_ref, g_ref, w16_ref, w32_ref, o16_ref, o32_ref):
    h = _modulated_norm(x_ref[0], g_ref[...], mod_ref[0], 0, 1).astype(BF16)
    step = 512
    for c0 in range(0, o16_ref.shape[2], step):
        c1 = min(c0 + step, o16_ref.shape[2])
        o16_ref[0, :, c0:c1] = _dot(h, w16_ref[:, c0:c1]).astype(BF16)
    for c0 in range(0, o32_ref.shape[2], step):
        c1 = min(c0 + step, o32_ref.shape[2])
        o32_ref[0, :, c0:c1] = _dot(h, w32_ref[:, c0:c1])


def _proj_in(x, mod, g, w16, w32, tm=256):
    bsz, s, d = x.shape
    return pl.pallas_call(
        _proj_in_kernel,
        grid=(bsz, s // tm),
        in_specs=[pl.BlockSpec((1, tm, d), lambda b, i: (b, i, 0)),
                  pl.BlockSpec((1, 6, d), lambda b, i: (b, 0, 0)),
                  pl.BlockSpec((1, d), lambda b, i: (0, 0)),
                  pl.BlockSpec(w16.shape, lambda b, i: (0, 0)),
                  pl.BlockSpec(w32.shape, lambda b, i: (0, 0))],
        out_specs=[pl.BlockSpec((1, tm, w16.shape[1]), lambda b, i: (b, i, 0)),
                   pl.BlockSpec((1, tm, w32.shape[1]), lambda b, i: (b, i, 0))],
        out_shape=[jax.ShapeDtypeStruct((bsz, s, w16.shape[1]), BF16),
                   jax.ShapeDtypeStruct((bsz, s, w32.shape[1]), F32)],
        compiler_params=_cparams("arbitrary", "arbitrary"),
        name="proj_in",
    )(x, mod, g.reshape(1, d), w16, w32)


def _compress_kernel(k2_ref, v2_ref, posa_ref, posb_ref, ak_ref, bk_ref, w2k_ref, av_ref, bv_ref, w2v_ref,
                     kc_ref, vc_ref):
    nrow = k2_ref.shape[1]

    def one(x_ref, a_ref, b_ref, w2_ref, o_ref):
        x = x_ref[0]
        ya = _dot((x + posa_ref[...]).astype(BF16), a_ref[...])
        yb = _dot((x + posb_ref[...]).astype(BF16), b_ref[...])
        hid = _gelu_tanh(ya + pltpu.roll(yb, nrow - 1, 0))
        out = _dot(hid.astype(BF16), w2_ref[...])
        for h in range(NSA_KV_HEADS):
            o_ref[0, h] = out[:, h * HEAD_DIM:(h + 1) * HEAD_DIM]

    one(k2_ref, ak_ref, bk_ref, w2k_ref, kc_ref)
    one(v2_ref, av_ref, bv_ref, w2v_ref, vc_ref)


def _expand_cmp_weights(w1, w2):
    half = CMP_LEN // 2
    w1r = w1.reshape(2, half, HEAD_DIM, CMP_HIDDEN)
    eye = jnp.eye(NSA_KV_HEADS, dtype=w1.dtype)
    exp = jnp.einsum('pldj,hg->plhdgj', w1r, eye).reshape(2, half * NSA_KV_HEADS * HEAD_DIM,
                                                          NSA_KV_HEADS * CMP_HIDDEN)
    w2e = jnp.einsum('jd,hg->hjgd', w2, eye).reshape(NSA_KV_HEADS * CMP_HIDDEN, NSA_KV_HEADS * HEAD_DIM)
    return exp[0].astype(BF16), exp[1].astype(BF16), w2e.astype(BF16)


def _compress(k2, v2, pos_emb, w_ck1, w_ck2, w_cv1, w_cv2):
    bsz, nrow, width = k2.shape
    half = CMP_LEN // 2
    pos = jnp.broadcast_to(pos_emb.reshape(2, half, 1, HEAD_DIM), (2, half, NSA_KV_HEADS, HEAD_DIM))
    pos = pos.reshape(2, 1, width)
    ak, bk, w2k = _expand_cmp_weights(w_ck1, w_ck2)
    av, bv, w2v = _expand_cmp_weights(w_cv1, w_cv2)
    full2 = lambda a: pl.BlockSpec(a.shape, lambda b: (0, 0))
    out_sds = jax.ShapeDtypeStruct((bsz, NSA_KV_HEADS, nrow, HEAD_DIM), F32)
    out_spec = pl.BlockSpec((1, NSA_KV_HEADS, nrow, HEAD_DIM), lambda b: (b, 0, 0, 0))
    return pl.pallas_call(
        _compress_kernel,
        grid=(bsz,),
        in_specs=[pl.BlockSpec((1, nrow, width), lambda b: (b, 0, 0)),
                  pl.BlockSpec((1, nrow, width), lambda b: (b, 0, 0)),
                  full2(pos[0]), full2(pos[1]), full2(ak), full2(bk), full2(w2k), full2(av), full2(bv), full2(w2v)],
        out_specs=[out_spec, out_spec],
        out_shape=[out_sds, out_sds],
        compiler_params=_cparams("arbitrary"),
        name="compress",
    )(k2, v2, pos[0], pos[1], ak, bk, w2k, av, bv, w2v)


def _nsa_cmp_kernel(q_ref, kc_ref, vc_ref, bias_ref, ovt_ref, oc_ref, mask_ref, *, top_n):
    tq = q_ref.shape[1]
    ncmp = kc_ref.shape[2]
    nslc = ovt_ref.shape[0]
    s0 = pl.program_id(2) * tq
    kcb = kc_ref[0, 0].astype(BF16)
    vcb = vc_ref[0, 0].astype(BF16)
    t_q = s0 + lax.broadcasted_iota(jnp.int32, (tq, ncmp), 0)
    n_i = lax.broadcasted_iota(jnp.int32, (tq, ncmp), 1)
    valid = (n_i * CMP_STRIDE + (CMP_LEN - 1)) <= t_q
    ovt = ovt_ref[...]
    gs = range(NSA_GROUP)
    sc = [_dot_nt(q_ref[0, :, g * HEAD_DIM:(g + 1) * HEAD_DIM], kcb) for g in gs]
    l = [jnp.where(valid, sc[g] + bias_ref[g], NEG_INF) for g in gs]
    m = [jnp.max(l[g], axis=-1, keepdims=True) for g in gs]
    e = [jnp.where(valid, jnp.exp(l[g] - m[g]), 0.0) for g in gs]
    den = [jnp.maximum(jnp.sum(e[g], axis=-1, keepdims=True), 1e-30) for g in gs]
    pb = [(e[g] / den[g]).astype(BF16) for g in gs]
    outs = [_dot(pb[g], vcb) for g in gs]
    imps = [_dot_nt(ovt, pb[g]) for g in gs]
    imp_t = functools.reduce(lambda acc, x: acc + x, imps)
    oc_ref[0] = jnp.concatenate(outs, axis=1)

    t_l = s0 + lax.broadcasted_iota(jnp.int32, (nslc, tq), 1)
    j_s = lax.broadcasted_iota(jnp.int32, (nslc, tq), 0)
    cur = lax.shift_right_logical(t_l, int(math.log2(SLC_BLOCK)))
    forced = (j_s == 0) | (j_s == cur) | (j_s == cur - 1)
    future = (j_s * SLC_BLOCK) > t_l
    score = jnp.where(future, NEG_INF, imp_t + jnp.where(forced, FORCED_BONUS, 0.0))
    rank = jnp.zeros((nslc, tq), F32)
    for i in range(nslc):
        r = score[i:i + 1, :]
        beats = (r > score) | ((r == score) & (j_s > i))
        rank = rank + jnp.where(beats, 1.0, 0.0)
    sel_t = jnp.where(rank < float(top_n), 1.0, 0.0).astype(BF16)
    eye = (lax.broadcasted_iota(jnp.int32, (tq, tq), 0) == lax.broadcasted_iota(jnp.int32, (tq, tq), 1))
    mask_ref[0, 0] = _dot_nt(jnp.where(eye, 1.0, 0.0).astype(BF16), sel_t).astype(BF16)


def _nsa_cmp(p16, kc, vc, bias_c, ovt, top_n):
    bsz, s, _ = p16.shape
    tq = ATT_TILE
    ncmp = kc.shape[2]
    nslc = ovt.shape[0]
    qw = NSA_GROUP * HEAD_DIM
    return pl.pallas_call(
        functools.partial(_nsa_cmp_kernel, top_n=top_n),
        grid=(bsz, NSA_KV_HEADS, s // tq),
        in_specs=[pl.BlockSpec((1, tq, qw), lambda b, h, i: (b, i, h)),
                  pl.BlockSpec((1, 1, ncmp, HEAD_DIM), lambda b, h, i: (b, h, 0, 0)),
                  pl.BlockSpec((1, 1, ncmp, HEAD_DIM), lambda b, h, i: (b, h, 0, 0)),
                  pl.BlockSpec((NSA_GROUP, tq, ncmp), lambda b, h, i: (h, i, 0)),
                  pl.BlockSpec(ovt.shape, lambda b, h, i: (0, 0))],
        out_specs=[pl.BlockSpec((1, tq, qw), lambda b, h, i: (b, i, h)),
                   pl.BlockSpec((1, 1, tq, nslc), lambda b, h, i: (b, h, i, 0))],
        out_shape=[jax.ShapeDtypeStruct((bsz, s, NSA_Q_W), F32),
                   jax.ShapeDtypeStruct((bsz, NSA_KV_HEADS, s, nslc), BF16)],
        compiler_params=_cparams("arbitrary", "arbitrary", "arbitrary"),
        name="nsa_cmp",
    )(p16, kc, vc, bias_c, ovt)


def _nsa_sw_kernel(q_ref, ks_ref, vs_ref, kw_ref, vw_ref, mask_ref, exp_ref, toep_ref, oc_ref, sm_ref, o_ref):
    tq = q_ref.shape[1]
    tk = ATT_TILE
    i = pl.program_id(1)
    s0 = i * tq
    gate = _sigmoid(sm_ref[0])
    t_q = s0 + lax.broadcasted_iota(jnp.int32, (tq, tk), 0)
    k_l = lax.broadcasted_iota(jnp.int32, (tq, tk), 1)
    n_win = WINDOW // tk
    pieces = []
    lane = lax.broadcasted_iota(jnp.int32, (tq, LANES), 1)
    assert NSA_KV_HEADS * HEAD_DIM == LANES and tq == tk
    for h in range(NSA_KV_HEADS):
        in_h = (lane >= h * HEAD_DIM) & (lane < (h + 1) * HEAD_DIM)
        qs = []
        for g in range(NSA_GROUP):
            hd = h * NSA_GROUP + g
            pair = q_ref[0, :, (hd // 2) * LANES:(hd // 2 + 1) * LANES].astype(F32)
            if hd % 2 != h:
                pair = pltpu.roll(pair, HEAD_DIM, 1)
            qs.append(jnp.where(in_h, pair, 0.0).astype(BF16))
        mq = mask_ref[0, h]

        def tile_update(j, carry, k_ref, v_ref, valid_fn, lo):
            us = range(TILES_PER_TRIP)
            gs = range(NSA_GROUP)
            kt_raw = [lo + TILES_PER_TRIP * j + u for u in us]
            kt = [jnp.minimum(kt_raw[u], i) for u in us]
            k = [k_ref[0, pl.ds(pl.multiple_of(kt[u] * tk, tk), tk), :] for u in us]
            v = [jnp.where(in_h, v_ref[0, pl.ds(pl.multiple_of(kt[u] * tk, tk), tk), :], 1.0) for u in us]
            mm = [jnp.minimum(i - kt[u], 2) for u in us]
            valid = [valid_fn(kt[u], kt_raw[u]) for u in us]
            sc = [[_dot_nt(qs[g], k[u]) for u in us] for g in gs]
            lg = [[jnp.where(valid[u], sc[g][u] + toep_ref[h * NSA_GROUP + g, mm[u]], NEG_INF) for u in us]
                  for g in gs]
            mx = [functools.reduce(jnp.maximum, [jnp.max(lg[g][u], axis=-1, keepdims=True) for u in us])
                  for g in gs]
            m_new = [jnp.maximum(carry[g][0], mx[g]) for g in gs]
            alpha = [jnp.exp(carry[g][0] - m_new[g]) for g in gs]
            e = [[jnp.where(valid[u], jnp.exp(lg[g][u] - m_new[g]), 0.0).astype(BF16) for u in us] for g in gs]
            pv = [functools.reduce(lambda a, b: a + b, [_dot(e[g][u], v[u]) for u in us]) for g in gs]
            return tuple((m_new[g], alpha[g] * carry[g][1] + pv[g]) for g in gs)

        def sel_valid(kt, kt_raw):
            mk = _dot(mq, exp_ref[kt])
            kpos = kt_raw * tk + k_l
            return (mk > 0.5) & (kpos <= t_q)

        def win_valid(kt, kt_raw):
            rel = t_q - (kt_raw * tk + k_l)
            return (rel >= 0) & (rel < WINDOW)

        init = tuple((jnp.full((tq, 1), NEG_INF, F32), jnp.zeros((tq, LANES), F32)) for _ in range(NSA_GROUP))
        trips = lambda lo: (i - lo + TILES_PER_TRIP) // TILES_PER_TRIP
        w_lo = jnp.maximum(i - n_win, 0)
        sel = lax.fori_loop(0, trips(0), lambda j, c: tile_update(j, c, ks_ref, vs_ref, sel_valid, 0), init)
        win = lax.fori_loop(0, trips(w_lo), lambda j, c: tile_update(j, c, kw_ref, vw_ref, win_valid, w_lo), init)
        hs = slice(h * HEAD_DIM, (h + 1) * HEAD_DIM)
        den = (1 - h) * HEAD_DIM
        for g in range(NSA_GROUP):
            hd = h * NSA_GROUP + g
            c = SM_GATES + hd * 3
            o_c = oc_ref[0, :, hd * HEAD_DIM:(hd + 1) * HEAD_DIM]
            o_s = (sel[g][1] / jnp.maximum(sel[g][1][:, den:den + 1], 1e-30))[:, hs]
            o_w = (win[g][1] / jnp.maximum(win[g][1][:, den:den + 1], 1e-30))[:, hs]
            pieces.append(gate[:, c:c + 1] * o_c + gate[:, c + 1:c + 2] * o_s + gate[:, c + 2:c + 3] * o_w)
    o_ref[0] = jnp.concatenate(pieces, axis=1).astype(o_ref.dtype)


def _nsa_sw(p16, p32, mask, expand, toep, o_c):
    bsz, s, _ = p16.shape
    tq = ATT_TILE
    nslc = mask.shape[3]
    kvb = NSA_Q_W // NSA_KV_W
    return pl.pallas_call(
        _nsa_sw_kernel,
        grid=(bsz, s // tq),
        in_specs=[pl.BlockSpec((1, tq, NSA_Q_W), lambda b, i: (b, i, 0)),
                  pl.BlockSpec((1, s, NSA_KV_W), lambda b, i: (b, 0, kvb)),
                  pl.BlockSpec((1, s, NSA_KV_W), lambda b, i: (b, 0, kvb + 1)),
                  pl.BlockSpec((1, s, NSA_KV_W), lambda b, i: (b, 0, kvb + 2)),
                  pl.BlockSpec((1, s, NSA_KV_W), lambda b, i: (b, 0, kvb + 3)),
                  pl.BlockSpec((1, NSA_KV_HEADS, tq, nslc), lambda b, i: (b, 0, i, 0)),
                  pl.BlockSpec(expand.shape, lambda b, i: (0, 0, 0)),
                  pl.BlockSpec(toep.shape, lambda b, i: (0, 0, 0, 0)),
                  pl.BlockSpec((1, tq, NSA_Q_W), lambda b, i: (b, i, 0)),
                  pl.BlockSpec((1, tq, LANES), lambda b, i: (b, i, C_SMALL // LANES))],
        out_specs=pl.BlockSpec((1, tq, NSA_Q_W), lambda b, i: (b, i, 0)),
        out_shape=jax.ShapeDtypeStruct((bsz, s, NSA_Q_W), BF16),
        compiler_params=_cparams("arbitrary", "arbitrary"),
        name="nsa_sw",
    )(p16, p16, p16, p16, p16, mask, expand, toep, o_c, p32)


def _deltanet_kernel(q_ref, k_ref, v_ref, z_ref, sm_ref, cw_ref, alog_ref, dt_ref, ng_ref, o_ref, xp_ref, st_ref):
    c = q_ref.shape[1]
    pad = 8
    ch = pl.program_id(1)

    @pl.when(ch == 0)
    def _():
        xp_ref[:, 0:pad, :] = jnp.zeros((3, pad, DN_W), F32)
        st_ref[...] = jnp.zeros(st_ref.shape, F32)

    conv = []
    for idx, r in enumerate((q_ref, k_ref, v_ref)):
        xp_ref[idx, pad:pad + c, :] = r[0]
        acc = None
        for j in range(DN_CONV):
            off = pad - (DN_CONV - 1) + j
            term = cw_ref[j:j + 1, idx * DN_W:(idx + 1) * DN_W] * xp_ref[idx, off:off + c, :]
            acc = term if acc is None else acc + term
        xp_ref[idx, 0:pad, :] = xp_ref[idx, c:c + pad, :]
        conv.append(_silu(acc))
    qa, ka, va = conv
    za = _silu(z_ref[0])

    sm = sm_ref[0]
    beta_all = _sigmoid(sm)
    g_all = -jnp.exp(alog_ref[...]) * _softplus(sm + dt_ref[...])
    row = lax.broadcasted_iota(jnp.int32, (c, c), 0)
    col = lax.broadcasted_iota(jnp.int32, (c, c), 1)
    tril = row >= col
    strict = row > col
    eye = row == col
    tril01 = jnp.where(tril, 1.0, 0.0).astype(BF16)
    ones01 = jnp.ones((c, c), BF16)
    gc_all = _dot_01(tril01, g_all)
    eg_all = jnp.exp(gc_all)
    eye_f = jnp.where(eye, 1.0, 0.0)
    ng = ng_ref[...]

    hh = range(DN_HEADS)
    hsl = [slice(h * DN_KDIM, (h + 1) * DN_KDIM) for h in hh]
    states = [st_ref[h] for h in hh]
    q = [qa[:, hsl[h]] for h in hh]
    k = [ka[:, hsl[h]] for h in hh]
    q = [q[h] * lax.rsqrt(jnp.sum(q[h] * q[h], axis=-1, keepdims=True) + EPS) * (DN_KDIM ** -0.5) for h in hh]
    k = [k[h] * lax.rsqrt(jnp.sum(k[h] * k[h], axis=-1, keepdims=True) + EPS) for h in hh]
    beta = [beta_all[:, SM_B + h:SM_B + h + 1] for h in hh]
    gcol = [gc_all[:, SM_A + h:SM_A + h + 1] for h in hh]
    egcol = [eg_all[:, SM_A + h:SM_A + h + 1] for h in hh]
    grow = _dot_01_many(ones01, [jnp.where(eye, gcol[h], 0.0) for h in hh])
    decay = [jnp.where(tril, jnp.exp(jnp.where(tril, gcol[h] - grow[h], 0.0)), 0.0) for h in hh]
    glast = [grow[h][:, c - 1:c] for h in hh]
    kb = [k[h] * beta[h] for h in hh]
    vb = [va[:, hsl[h]] * beta[h] for h in hh]
    kbf = [k[h].astype(BF16) for h in hh]
    kk = [_dot_nt(kb[h].astype(BF16), kbf[h]) for h in hh]
    a = [jnp.where(strict, kk[h] * decay[h], 0.0) for h in hh]
    t_inv = [eye_f - a[h] for h in hh]
    pw = a
    for _ in range(int(math.log2(c)) - 1):
        pw = _dot_x3_many(pw, pw)
        tp = _dot_x3_many(t_inv, pw)
        t_inv = [t_inv[h] + tp[h] for h in hh]
    rhs = [jnp.concatenate([vb[h], kb[h] * egcol[h]], axis=1) for h in hh]
    sol = _dot_x3_many(t_inv, rhs)
    qk = [_dot_nt(q[h].astype(BF16), kbf[h]) for h in hh]
    attn = [(qk[h] * decay[h]).astype(BF16) for h in hh]
    q_dec = [(q[h] * egcol[h]).astype(BF16) for h in hh]
    k_dec = [(k[h] * jnp.exp(glast[h] - gcol[h])).astype(BF16) for h in hh]
    sb = [states[h].astype(BF16) for h in hh]
    ws = [_dot(sol[h][:, DN_VDIM:].astype(BF16), sb[h]) for h in hh]
    v_new = [(sol[h][:, :DN_VDIM] - ws[h]).astype(BF16) for h in hh]
    o1 = [_dot(q_dec[h], sb[h]) for h in hh]
    o2 = [_dot(attn[h], v_new[h]) for h in hh]
    kv = [_dot_tn(k_dec[h], v_new[h]) for h in hh]
    for h in hh:
        st_ref[h] = states[h] * jnp.exp(glast[h][0:1, :]) + kv[h]
    o = [o1[h] + o2[h] for h in hh]
    outs = [o[h] * lax.rsqrt(jnp.mean(o[h] * o[h], axis=-1, keepdims=True) + EPS) * ng * za[:, hsl[h]] for h in hh]
    o_ref[0] = jnp.concatenate(outs, axis=1).astype(o_ref.dtype)


def _deltanet(p32, conv_w, alog_row, dt_row, norm_g):
    bsz, s, _ = p32.shape
    c = DN_CHUNK
    blk = lambda col: pl.BlockSpec((1, c, DN_W), lambda b, i: (b, i, col // DN_W))
    full2 = lambda a: pl.BlockSpec(a.shape, lambda b, i: (0, 0))
    ng = norm_g.reshape(1, DN_VDIM)
    return pl.pallas_call(
        _deltanet_kernel,
        grid=(bsz, s // c),
        in_specs=[blk(C_DNQ), blk(C_DNK), blk(C_DNV), blk(C_DNZ),
                  pl.BlockSpec((1, c, LANES), lambda b, i: (b, i, C_SMALL // LANES)),
                  full2(conv_w), full2(alog_row), full2(dt_row), full2(ng)],
        out_specs=pl.BlockSpec((1, c, DN_W), lambda b, i: (b, i, 0)),
        out_shape=jax.ShapeDtypeStruct((bsz, s, DN_W), BF16),
        scratch_shapes=[pltpu.VMEM((3, c + 8, DN_W), F32),
                        pltpu.VMEM((DN_HEADS, DN_KDIM, DN_VDIM), F32)],
        compiler_params=_cparams("arbitrary", "arbitrary"),
        name="deltanet",
    )(p32, p32, p32, p32, p32, conv_w, alog_row, dt_row, ng)


def _mix_out_kernel(x_ref, mod_ref, onsa_ref, odn_ref, gn_ref, gd_ref, wn_ref, wd_ref, wo_ref, o_ref):
    merged = (_sigmoid(gn_ref[0]) * _dot(onsa_ref[0], wn_ref[...])
              + _sigmoid(gd_ref[0]) * _dot(odn_ref[0], wd_ref[...]))
    y = _dot(merged.astype(BF16), wo_ref[...])
    o_ref[0] = x_ref[0] + mod_ref[0][2:3] * y


def _mix_out(x, mod, o_nsa, o_dn, p32, wn, wd, wo, tm=256):
    bsz, s, d = x.shape
    full2 = lambda a: pl.BlockSpec(a.shape, lambda b, i: (0, 0))
    return pl.pallas_call(
        _mix_out_kernel,
        grid=(bsz, s // tm),
        in_specs=[pl.BlockSpec((1, tm, d), lambda b, i: (b, i, 0)),
                  pl.BlockSpec((1, 6, d), lambda b, i: (b, 0, 0)),
                  pl.BlockSpec((1, tm, NSA_Q_W), lambda b, i: (b, i, 0)),
                  pl.BlockSpec((1, tm, DN_W), lambda b, i: (b, i, 0)),
                  pl.BlockSpec((1, tm, d), lambda b, i: (b, i, C_GNSA // d)),
                  pl.BlockSpec((1, tm, d), lambda b, i: (b, i, C_GDN // d)),
                  full2(wn), full2(wd), full2(wo)],
        out_specs=pl.BlockSpec((1, tm, d), lambda b, i: (b, i, 0)),
        out_shape=jax.ShapeDtypeStruct((bsz, s, d), F32),
        compiler_params=_cparams("arbitrary", "arbitrary"),
        name="mix_out",
    )(x, mod, o_nsa, o_dn, p32, p32, wn, wd, wo)


def _top_k_rows(s, idx, k, payload=None):
    vals, picks = [], []
    for _ in range(k):
        mx = jnp.max(s, axis=0, keepdims=True)
        am = jnp.min(jnp.where(s == mx, idx, 1e9), axis=0, keepdims=True)
        hit = idx == am
        vals.append(mx)
        if payload is None:
            picks.append(am)
        else:
            picks.append(jnp.sum(jnp.where(hit, payload, 0.0), axis=0, keepdims=True))
        s = jnp.where(hit, -jnp.inf, s)
    return jnp.concatenate(vals, axis=0), jnp.concatenate(picks, axis=0)


def _peer_route_kernel(x_ref, mod_ref, g_ref, wq_ref, k1_ref, k2_ref, h_ref, idx_ref, gate_ref):
    tm = x_ref.shape[1]
    h2 = _modulated_norm(x_ref[0], g_ref[...], mod_ref[0], 3, 4)
    hb = h2.astype(BF16)
    h_ref[0] = hb
    qry = _dot(hb, wq_ref[...])
    kidx = lax.broadcasted_iota(jnp.int32, (PEER_NKEYS, tm), 0).astype(F32)
    kk = PEER_TOPK
    hk = kk // 2
    b_iota = kidx[0:kk]
    cidx = jnp.concatenate([b_iota] + [a * float(kk) + b_iota[0:hk] for a in range(1, hk)]
                           + [(b_iota[hk:kk]) * float(kk)], axis=0)
    for h in range(PEER_HEADS):
        qh = qry[:, h * PEER_DKEY:(h + 1) * PEER_DKEY].astype(BF16)
        s1 = _dot_nt(k1_ref[h], qh[:, :PEER_HALF])
        s2 = _dot_nt(k2_ref[h], qh[:, PEER_HALF:])
        v1, i1 = _top_k_rows(s1, kidx, PEER_TOPK)
        v2, i2 = _top_k_rows(s2, kidx, PEER_TOPK)
        cand_s = jnp.concatenate([v1[0:1] + v2] + [v1[a:a + 1] + v2[0:hk] for a in range(1, hk)]
                                 + [v1[hk:kk] + v2[0:1]], axis=0)
        e1 = i1 * float(PEER_NKEYS)
        cand_i = jnp.concatenate([e1[0:1] + i2] + [e1[a:a + 1] + i2[0:hk] for a in range(1, hk)]
                                 + [e1[hk:kk] + i2[0:1]], axis=0)
        top_s, top_e = _top_k_rows(cand_s, cidx, PEER_TOPK, payload=cand_i)
        e = jnp.exp(top_s - top_s[0:1])
        gate = e / jnp.sum(e, axis=0, keepdims=True)
        idx_ref[h * PEER_TOPK:(h + 1) * PEER_TOPK, :] = (top_e * float(PACKED_ROWS)).astype(jnp.int32)
        gate_ref[h * PEER_TOPK:(h + 1) * PEER_TOPK, :] = gate


def _peer_route(x1, mod, g, wq, k1, k2, tm=128):
    bsz, s, d = x1.shape
    nt = s // tm
    ne = PEER_HEADS * PEER_TOPK
    full = lambda a: pl.BlockSpec(a.shape, lambda b, i: (0,) * a.ndim)
    return pl.pallas_call(
        _peer_route_kernel,
        grid=(bsz, nt),
        in_specs=[pl.BlockSpec((1, tm, d), lambda b, i: (b, i, 0)),
                  pl.BlockSpec((1, 6, d), lambda b, i: (b, 0, 0)),
                  pl.BlockSpec((1, d), lambda b, i: (0, 0)),
                  full(wq), full(k1), full(k2)],
        out_specs=[pl.BlockSpec((1, tm, d), lambda b, i: (b, i, 0)),
                   pl.BlockSpec((ne, tm), lambda b, i: (0, b * nt + i)),
                   pl.BlockSpec((ne, tm), lambda b, i: (0, b * nt + i))],
        out_shape=[jax.ShapeDtypeStruct((bsz, s, d), BF16),
                   jax.ShapeDtypeStruct((ne, bsz * s), jnp.int32),
                   jax.ShapeDtypeStruct((ne, bsz * s), F32)],
        compiler_params=_cparams("arbitrary", "arbitrary"),
        name="peer_route",
    )(x1, mod, g.reshape(1, d), wq, k1, k2)


PEER_TOK = 16
EXPERT_TILE_ROWS = 8
PACKED_ROWS = EXPERT_TILE_ROWS // 2


def _load_table_once(tab_hbm, tab_vmem, sem):
    @pl.when(pl.program_id(0) == 0)
    def _():
        cp = pltpu.make_async_copy(tab_hbm, tab_vmem, sem)
        cp.start()
        cp.wait()


def _pack_table(tab):
    e = tab.shape[0]
    bits = lax.bitcast_convert_type(tab.astype(BF16).reshape(e * PACKED_ROWS, 2, LANES), jnp.uint16)
    bits = bits.astype(jnp.uint32)
    return lax.bitcast_convert_type(bits[:, 0, :] | (bits[:, 1, :] << 16), jnp.int32)


def _stage_experts(idx_ref, tab_vmem, stage, nsel):
    for r in range(nsel):
        row = pl.multiple_of(idx_ref[0, 0, r], PACKED_ROWS)
        stage[r * PACKED_ROWS:(r + 1) * PACKED_ROWS, :] = tab_vmem[pl.ds(row, PACKED_ROWS), :]


def _staged_tiles(stage, t, ncol):
    return pltpu.bitcast(stage[t * (ncol // 2):(t + 1) * (ncol // 2), :], BF16)


def _diag_mask(ncol):
    row = lax.broadcasted_iota(jnp.int32, (EXPERT_TILE_ROWS, ncol), 0)
    col = lax.broadcasted_iota(jnp.int32, (EXPERT_TILE_ROWS, ncol), 1)
    return (col & (EXPERT_TILE_ROWS - 1)) == row


def _peer_act_kernel(idx_ref, tab_hbm, h_ref, gate_ref, grp_ref, w_ref, tab_vmem, stage, sem, *, rows_per_tok):
    _load_table_once(tab_hbm, tab_vmem, sem)
    _stage_experts(idx_ref, tab_vmem, stage, PEER_TOK * rows_per_tok)
    ncol = rows_per_tok * EXPERT_TILE_ROWS
    diag = _diag_mask(ncol)
    sums = []
    for t in range(PEER_TOK):
        m = _staged_tiles(stage, t, ncol)
        rt = _dot_nt(h_ref[t], m)
        sums.append(jnp.sum(jnp.where(diag, rt, 0.0), axis=0, keepdims=True))
    part = jnp.concatenate(sums, axis=0)
    hi, mid, lo = _split3(part)
    grp = grp_ref[...]
    act = _dot(hi, grp) + (_dot(mid, grp) + _dot(lo, grp))
    w_ref[...] = gate_ref[...] * _gelu_tanh(act)


def _peer_out_kernel(idx_ref, tab_hbm, w_ref, exp_ref, x_ref, mod_ref, fg_ref, o_ref, tab_vmem, stage, sem,
                     *, rows_per_tok):
    _load_table_once(tab_hbm, tab_vmem, sem)
    _stage_experts(idx_ref, tab_vmem, stage, PEER_TOK * rows_per_tok)
    ncol = rows_per_tok * EXPERT_TILE_ROWS
    diag = _diag_mask(ncol)
    wh, wl = _split2(w_ref[...])
    ex = exp_ref[...]
    weh = _dot(wh, ex)
    wel = _dot(wl, ex)
    g2 = mod_ref[0, 5]
    fg = fg_ref[...]
    nel = float(EXPERT_TILE_ROWS * LANES)
    for t in range(PEER_TOK):
        lhs = jnp.concatenate([jnp.where(diag, weh[t:t + 1], 0.0), jnp.where(diag, wel[t:t + 1], 0.0)], axis=0)
        o16 = _dot(lhs.astype(BF16), _staged_tiles(stage, t, ncol))
        y = o16[0:EXPERT_TILE_ROWS] + o16[EXPERT_TILE_ROWS:]
        xo = x_ref[t] + g2 * y
        ms = jnp.sum(jnp.sum(xo * xo, axis=1, keepdims=True), axis=0, keepdims=True) / nel
        o_ref[t] = xo * lax.rsqrt(ms + EPS) * fg


def _peer_experts(idx3, tab_u, tab_v, h3, gate, x3, mod4, fg3, s):
    n = h3.shape[0]
    ne = gate.shape[1]
    nstep = n // PEER_TOK
    nsel = PEER_TOK * ne
    ncol = ne * EXPERT_TILE_ROWS
    grp = jnp.asarray(np.arange(ncol)[:, None] // EXPERT_TILE_ROWS == np.arange(ne)[None, :], BF16)
    idx_spec = pl.BlockSpec((1, 1, nsel), lambda t: (t, 0, 0), memory_space=pltpu.SMEM)
    any_spec = pl.BlockSpec(memory_space=pl.ANY)
    tile_spec = pl.BlockSpec((PEER_TOK, EXPERT_TILE_ROWS, LANES), lambda t: (t, 0, 0))
    row_spec = pl.BlockSpec((PEER_TOK, ne), lambda t: (t, 0))
    scratch = [pltpu.VMEM(tab_u.shape, jnp.int32), pltpu.VMEM((PEER_TOK * ncol // 2, LANES), jnp.int32),
               pltpu.SemaphoreType.DMA]
    w = pl.pallas_call(
        functools.partial(_peer_act_kernel, rows_per_tok=ne),
        grid=(nstep,),
        in_specs=[idx_spec, any_spec, tile_spec, row_spec, pl.BlockSpec(grp.shape, lambda t: (0, 0))],
        out_specs=row_spec,
        out_shape=jax.ShapeDtypeStruct((n, ne), F32),
        scratch_shapes=scratch,
        compiler_params=_cparams("arbitrary"),
        name="peer_act",
    )(idx3, tab_u, h3, gate, grp)
    return pl.pallas_call(
        functools.partial(_peer_out_kernel, rows_per_tok=ne),
        grid=(nstep,),
        in_specs=[idx_spec, any_spec, row_spec, pl.BlockSpec((ne, ncol), lambda t: (0, 0)), tile_spec,
                  pl.BlockSpec((1, 6, EXPERT_TILE_ROWS, LANES), lambda t: ((t * PEER_TOK) // s, 0, 0, 0)),
                  pl.BlockSpec((EXPERT_TILE_ROWS, LANES), lambda t: (0, 0))],
        out_specs=tile_spec,
        out_shape=jax.ShapeDtypeStruct((n, EXPERT_TILE_ROWS, LANES), F32),
        scratch_shapes=scratch,
        compiler_params=_cparams("arbitrary"),
        name="peer_out",
    )(idx3, tab_v, w, grp.T, x3, mod4, fg3)


def _rel_bucket(dist):
    dist = jnp.maximum(dist, 0)
    max_exact = REL_BUCKETS // 2
    scaled = jnp.log(jnp.maximum(dist, 1).astype(F32) / max_exact) / math.log(REL_MAX_DIST / max_exact)
    large = jnp.minimum(max_exact + (scaled * (REL_BUCKETS - max_exact)).astype(jnp.int32), REL_BUCKETS - 1)
    return jnp.where(dist < max_exact, dist, large)


def _bias_tables(rel_bias, s, ncmp):
    dist_bias = rel_bias[_rel_bucket(jnp.arange(s))].T
    nq = s // ATT_TILE
    per_tile = ATT_TILE // CMP_STRIDE
    r = np.arange(ATT_TILE)[:, None]
    k0 = per_tile * (nq - 1)
    k = np.arange(k0 + ncmp)[None, :]
    dist_g = r - CMP_STRIDE * (k - k0) - (CMP_LEN - 1)
    g = dist_bias[:, np.clip(dist_g, 0, s - 1)]
    wins = jnp.stack([g[:, :, k0 - per_tile * i:k0 - per_tile * i + ncmp] for i in range(nq)], axis=1)
    bias_c = wins.reshape(rel_bias.shape[1], s, ncmp)
    ii = jnp.arange(ATT_TILE)[:, None]
    jj = jnp.arange(ATT_TILE)[None, :]
    dist = jnp.stack([m * ATT_TILE + ii - jj for m in range(3)])
    toep = dist_bias[:, jnp.clip(dist, 0, s - 1)]
    return bias_c, toep


def _check_bucket_saturation():
    d = np.arange(ATT_TILE + 1, 1 << 16)
    max_exact = REL_BUCKETS // 2
    scaled = np.log(d.astype(np.float32) / max_exact) / math.log(REL_MAX_DIST / max_exact)
    large = np.minimum(max_exact + (scaled * (REL_BUCKETS - max_exact)).astype(np.int32), REL_BUCKETS - 1)
    assert (large == REL_BUCKETS - 1).all()


_check_bucket_saturation()


def _token_mixer(x, mod, rel_bias, norm_g, w_in, cmp_pos_emb, w_ck1, w_ck2, w_cv1, w_cv2, conv_w, a_log, dt_bias,
                 dn_norm_g, w_branch_nsa, w_branch_dn, w_out):
    bsz, s, d = x.shape
    assert s % ATT_TILE == 0 and s % CMP_STRIDE == 0 and WINDOW % ATT_TILE == 0 and d == C_GDN - C_GNSA
    pts = np.cumsum(IN_SPLIT_SIZES + (d, d))[:-1]
    assert w_in.shape[1] == sum(IN_SPLIT_SIZES) + 2 * d
    (w_q, w_kc, w_vc, w_ks, w_vs, w_kw, w_vw, w_gates, w_dq, w_dk, w_dv, w_dz, w_db, w_da, w_gn, w_gd) = (
        jnp.split(w_in, [int(p) for p in pts], axis=1))
    w16 = jnp.concatenate([w_q * (HEAD_DIM ** -0.5), w_ks, w_vs, w_kw, w_vw], axis=1).astype(BF16)
    small_pad = jnp.zeros((d, LANES - NSA_HEADS * 3 - 2 * DN_HEADS), w_in.dtype)
    w32 = jnp.concatenate([w_dq, w_dk, w_dv, w_dz, w_gn, w_gd, w_kc, w_vc, w_gates, w_db, w_da, small_pad],
                          axis=1).astype(BF16)
    assert w16.shape[1] == C16 and w32.shape[1] == C32
    p16, p32 = _proj_in(x, mod, norm_g, w16, w32)

    nrow = s // CMP_STRIDE
    k2 = p32[:, :, C_KC:C_KC + NSA_KV_W].reshape(bsz, nrow, CMP_STRIDE * NSA_KV_W)
    v2 = p32[:, :, C_VC:C_VC + NSA_KV_W].reshape(bsz, nrow, CMP_STRIDE * NSA_KV_W)
    kc, vc = _compress(k2, v2, cmp_pos_emb, w_ck1, w_ck2, w_cv1, w_cv2)
    nslc = s // SLC_BLOCK
    top_n = min(SLC_TOPN, nslc)
    bias_c, toep = _bias_tables(rel_bias, s, nrow)
    cmp_start = np.arange(nrow) * CMP_STRIDE
    slc_start = np.arange(nslc) * SLC_BLOCK
    ovt = ((cmp_start[None, :] <= slc_start[:, None] + SLC_BLOCK - 1)
           & (cmp_start[None, :] + CMP_LEN - 1 >= slc_start[:, None]))
    ovt = jnp.asarray(ovt, BF16)
    o_c, mask = _nsa_cmp(p16, kc, vc, bias_c, ovt, top_n)
    nkt = s // ATT_TILE
    key_blk = (np.arange(nkt)[:, None, None] * ATT_TILE + np.arange(ATT_TILE)[None, None, :]) // SLC_BLOCK
    expand = jnp.asarray(key_blk == np.arange(nslc)[None, :, None], BF16)
    o_nsa = _nsa_sw(p16, p32, mask, expand, toep, o_c)

    lane_row = lambda vec: jnp.zeros((1, LANES), F32).at[0, SM_A:SM_A + DN_HEADS].set(vec.astype(F32))
    o_dn = _deltanet(p32, conv_w, lane_row(a_log), lane_row(dt_bias), dn_norm_g)

    return _mix_out(x, mod, o_nsa, o_dn, p32, w_branch_nsa.astype(BF16), w_branch_dn.astype(BF16),
                    w_out.astype(BF16))


def _peer_block(x1, mod, norm_g, final_g, w_query, keys1, keys2, expert_u, expert_v):
    bsz, s, d = x1.shape
    n = bsz * s
    ne = PEER_HEADS * PEER_TOPK
    h2, idx_t, gate_t = _peer_route(x1, mod, norm_g, w_query.astype(BF16), keys1.astype(BF16), keys2.astype(BF16))
    assert d == EXPERT_TILE_ROWS * LANES
    nstep = n // PEER_TOK
    idx3 = idx_t.T.reshape(nstep, 1, PEER_TOK * ne)
    tile = (EXPERT_TILE_ROWS, LANES)
    tab_u = _pack_table(expert_u)
    tab_v = _pack_table(expert_v)
    out = _peer_experts(idx3, tab_u, tab_v, h2.reshape((n,) + tile), gate_t.T, x1.reshape((n,) + tile),
                        mod.reshape((bsz, 6) + tile), final_g.reshape(tile), s)
    return out.reshape(bsz, s, d)


def kernel(x, c, rel_bias, final_g, w_ada, b_ada, norm1_g, w_in, cmp_pos_emb, w_cmp_k1, w_cmp_k2, w_cmp_v1,
           w_cmp_v2, dn_conv_w, dn_A_log, dn_dt_bias, dn_norm_g, w_branch_nsa, w_branch_dn, w_out, norm2_g,
           peer_w_query, peer_keys1, peer_keys2, peer_u, peer_v):
    depth = w_ada.shape[0]
    assert depth == 1, "the final norm is fused into the last PEER kernel; one layer is supported"
    bsz, s, d = x.shape
    mod = _ada(c, w_ada[0], b_ada[0]).reshape(bsz, 6, d)
    x1 = _token_mixer(x, mod, rel_bias, norm1_g[0], w_in[0], cmp_pos_emb[0], w_cmp_k1[0], w_cmp_k2[0], w_cmp_v1[0],
                      w_cmp_v2[0], dn_conv_w[0], dn_A_log[0], dn_dt_bias[0], dn_norm_g[0], w_branch_nsa[0],
                      w_branch_dn[0], w_out[0])
    return _peer_block(x1, mod, norm2_g[0], final_g, peer_w_query[0], peer_keys1[0], peer_keys2[0], peer_u[0],
                       peer_v[0])
```

```python
import functools
import math

import numpy as np
import jax
import jax.numpy as jnp
from jax import lax
from jax.experimental import pallas as pl
from jax.experimental.pallas import tpu as pltpu

F32 = jnp.float32
BF16 = jnp.bfloat16

NSA_HEADS = 8
NSA_KV_HEADS = 2
NSA_GROUP = NSA_HEADS // NSA_KV_HEADS
HEAD_DIM = 64
CMP_LEN = 32
CMP_STRIDE = 16
CMP_HIDDEN = 64
SLC_BLOCK = 64
SLC_TOPN = 16
WINDOW = 512
FORCED_BONUS = 1.0e4
DN_HEADS = 8
DN_KDIM = 64
DN_VDIM = 64
DN_CONV = 4
DN_CHUNK = 64
PEER_HEADS = 8
PEER_NKEYS = 128
PEER_DKEY = 128
PEER_HALF = PEER_DKEY // 2
PEER_TOPK = 16
REL_BUCKETS = 32
REL_MAX_DIST = 128
EPS = 1e-6
NEG_INF = -1e30

NSA_Q_W = NSA_HEADS * HEAD_DIM
NSA_KV_W = NSA_KV_HEADS * HEAD_DIM
DN_W = DN_HEADS * DN_KDIM
IN_SPLIT_SIZES = (NSA_Q_W, NSA_KV_W, NSA_KV_W, NSA_KV_W, NSA_KV_W, NSA_KV_W, NSA_KV_W, NSA_HEADS * 3,
                  DN_W, DN_W, DN_W, DN_W, DN_HEADS, DN_HEADS)

LANES = 128
ATT_TILE = 128
TILES_PER_TRIP = 4
VMEM_LIMIT = 56 * 1024 * 1024

C_DNQ, C_DNK, C_DNV, C_DNZ = 0, 512, 1024, 1536
C_GNSA, C_GDN = 2048, 3072
C_KC, C_VC, C_SMALL = 4096, 4224, 4352
C32 = 4480
SM_GATES, SM_B, SM_A = 0, 24, 32
C16 = 1024


def _gelu_tanh(x):
    return 0.5 * x * (1.0 + jnp.tanh(math.sqrt(2.0 / math.pi) * (x + 0.044715 * (x * x * x))))


def _sigmoid(x):
    return 1.0 / (1.0 + jnp.exp(-x))


def _silu(x):
    return x * _sigmoid(x)


def _softplus(x):
    return jnp.maximum(x, 0.0) + jnp.log1p(jnp.exp(-jnp.abs(x)))


def _dot(a, b):
    return jnp.dot(a, b, preferred_element_type=F32)


def _dot_nt(a, b):
    return lax.dot_general(a, b, (((1,), (1,)), ((), ())), preferred_element_type=F32)


def _dot_tn(a, b):
    return lax.dot_general(a, b, (((0,), (0,)), ((), ())), preferred_element_type=F32)


def _split2(x):
    hi = x.astype(BF16)
    lo = (x - hi.astype(F32)).astype(BF16)
    return hi, lo


def _split3(x):
    hi = x.astype(BF16)
    r = x - hi.astype(F32)
    mid = r.astype(BF16)
    lo = (r - mid.astype(F32)).astype(BF16)
    return hi, mid, lo


def _dot_x3(a, b):
    ah, al = _split2(a)
    bh, bl = _split2(b)
    return _dot(ah, bh) + (_dot(ah, bl) + _dot(al, bh))


def _dot_01(m01, x):
    hi, mid, lo = _split3(x)
    return _dot(m01, hi) + (_dot(m01, mid) + _dot(m01, lo))


def _dot_x3_many(a_list, b_list):
    n = range(len(a_list))
    asp = [_split2(a) for a in a_list]
    bsp = [_split2(b) for b in b_list]
    hh = [_dot(asp[i][0], bsp[i][0]) for i in n]
    hl = [_dot(asp[i][0], bsp[i][1]) for i in n]
    lh = [_dot(asp[i][1], bsp[i][0]) for i in n]
    return [hh[i] + (hl[i] + lh[i]) for i in n]


def _dot_01_many(m01, x_list):
    n = range(len(x_list))
    sp = [_split3(x) for x in x_list]
    d0 = [_dot(m01, sp[i][0]) for i in n]
    d1 = [_dot(m01, sp[i][1]) for i in n]
    d2 = [_dot(m01, sp[i][2]) for i in n]
    return [d0[i] + (d1[i] + d2[i]) for i in n]


def _cparams(*sem):
    return pltpu.CompilerParams(dimension_semantics=sem, vmem_limit_bytes=VMEM_LIMIT)


def _ada_kernel(c_ref, w_ref, b_ref, o_ref):
    cond = _silu(c_ref[...])
    o_ref[...] = _dot(cond.astype(BF16), w_ref[...].astype(BF16)) + b_ref[...]


def _ada(c, w_ada, b_ada):
    bsz, d = c.shape
    n_out = w_ada.shape[1]
    blk = d
    return pl.pallas_call(
        _ada_kernel,
        grid=(n_out // blk,),
        in_specs=[pl.BlockSpec((bsz, d), lambda j: (0, 0)),
                  pl.BlockSpec((d, blk), lambda j: (0, j)),
                  pl.BlockSpec((1, blk), lambda j: (0, j))],
        out_specs=pl.BlockSpec((bsz, blk), lambda j: (0, j)),
        out_shape=jax.ShapeDtypeStruct((bsz, n_out), F32),
        compiler_params=_cparams("arbitrary"),
        name="ada",
    )(c, w_ada, b_ada.reshape(1, n_out))


def _modulated_norm(x, g, mod, shift_row, scale_row):
    var = jnp.mean(x * x, axis=-1, keepdims=True)
    y = x * lax.rsqrt(var + EPS) * g
    return y * (1.0 + mod[scale_row:scale_row + 1]) + mod[shift_row:shift_row + 1]


def _proj_in_kernel(x_ref, mod_ref, g_ref, w16_ref, w32_ref, o16_ref, o32_ref):
    h = _modulated_norm(x_ref[0], g_ref[...], mod_ref[0], 0, 1).astype(BF16)
    step = 512
    for c0 in range(0, o16_ref.shape[2], step):
        c1 = min(c0 + step, o16_ref.shape[2])
        o16_ref[0, :, c0:c1] = _dot(h, w16_ref[:, c0:c1]).astype(BF16)
    for c0 in range(0, o32_ref.shape[2], step):
        c1 = min(c0 + step, o32_ref.shape[2])
        o32_ref[0, :, c0:c1] = _dot(h, w32_ref[:, c0:c1])


def _proj_in(x, mod, g, w16, w32, tm=256):
    bsz, s, d = x.shape
    return pl.pallas_call(
        _proj_in_kernel,
        grid=(bsz, s // tm),
        in_specs=[pl.BlockSpec((1, tm, d), lambda b, i: (b, i, 0)),
                  pl.BlockSpec((1, 6, d), lambda b, i: (b, 0, 0)),
                  pl.BlockSpec((1, d), lambda b, i: (0, 0)),
                  pl.BlockSpec(w16.shape, lambda b, i: (0, 0)),
                  pl.BlockSpec(w32.shape, lambda b, i: (0, 0))],
        out_specs=[pl.BlockSpec((1, tm, w16.shape[1]), lambda b, i: (b, i, 0)),
                   pl.BlockSpec((1, tm, w32.shape[1]), lambda b, i: (b, i, 0))],
        out_shape=[jax.ShapeDtypeStruct((bsz, s, w16.shape[1]), BF16),
                   jax.ShapeDtypeStruct((bsz, s, w32.shape[1]), F32)],
        compiler_params=_cparams("arbitrary", "arbitrary"),
        name="proj_in",
    )(x, mod, g.reshape(1, d), w16, w32)


def _compress_kernel(k2_ref, v2_ref, posa_ref, posb_ref, ak_ref, bk_ref, w2k_ref, av_ref, bv_ref, w2v_ref,
                     kc_ref, vc_ref):
    nrow = k2_ref.shape[1]

    def one(x_ref, a_ref, b_ref, w2_ref, o_ref):
        x = x_ref[0]
        ya = _dot((x + posa_ref[...]).astype(BF16), a_ref[...])
        yb = _dot((x + posb_ref[...]).astype(BF16), b_ref[...])
        hid = _gelu_tanh(ya + pltpu.roll(yb, nrow - 1, 0))
        out = _dot(hid.astype(BF16), w2_ref[...])
        for h in range(NSA_KV_HEADS):
            o_ref[0, h] = out[:, h * HEAD_DIM:(h + 1) * HEAD_DIM]

    one(k2_ref, ak_ref, bk_ref, w2k_ref, kc_ref)
    one(v2_ref, av_ref, bv_ref, w2v_ref, vc_ref)


def _expand_cmp_weights(w1, w2):
    half = CMP_LEN // 2
    w1r = w1.reshape(2, half, HEAD_DIM, CMP_HIDDEN)
    eye = jnp.eye(NSA_KV_HEADS, dtype=w1.dtype)
    exp = jnp.einsum('pldj,hg->plhdgj', w1r, eye).reshape(2, half * NSA_KV_HEADS * HEAD_DIM,
                                                          NSA_KV_HEADS * CMP_HIDDEN)
    w2e = jnp.einsum('jd,hg->hjgd', w2, eye).reshape(NSA_KV_HEADS * CMP_HIDDEN, NSA_KV_HEADS * HEAD_DIM)
    return exp[0].astype(BF16), exp[1].astype(BF16), w2e.astype(BF16)


def _compress(k2, v2, pos_emb, w_ck1, w_ck2, w_cv1, w_cv2):
    bsz, nrow, width = k2.shape
    half = CMP_LEN // 2
    pos = jnp.broadcast_to(pos_emb.reshape(2, half, 1, HEAD_DIM), (2, half, NSA_KV_HEADS, HEAD_DIM))
    pos = pos.reshape(2, 1, width)
    ak, bk, w2k = _expand_cmp_weights(w_ck1, w_ck2)
    av, bv, w2v = _expand_cmp_weights(w_cv1, w_cv2)
    full2 = lambda a: pl.BlockSpec(a.shape, lambda b: (0, 0))
    out_sds = jax.ShapeDtypeStruct((bsz, NSA_KV_HEADS, nrow, HEAD_DIM), F32)
    out_spec = pl.BlockSpec((1, NSA_KV_HEADS, nrow, HEAD_DIM), lambda b: (b, 0, 0, 0))
    return pl.pallas_call(
        _compress_kernel,
        grid=(bsz,),
        in_specs=[pl.BlockSpec((1, nrow, width), lambda b: (b, 0, 0)),
                  pl.BlockSpec((1, nrow, width), lambda b: (b, 0, 0)),
                  full2(pos[0]), full2(pos[1]), full2(ak), full2(bk), full2(w2k), full2(av), full2(bv), full2(w2v)],
        out_specs=[out_spec, out_spec],
        out_shape=[out_sds, out_sds],
        compiler_params=_cparams("arbitrary"),
        name="compress",
    )(k2, v2, pos[0], pos[1], ak, bk, w2k, av, bv, w2v)


def _nsa_cmp_kernel(q_ref, kc_ref, vc_ref, bias_ref, ovt_ref, oc_ref, mask_ref, *, top_n):
    tq = q_ref.shape[1]
    ncmp = kc_ref.shape[2]
    nslc = ovt_ref.shape[0]
    s0 = pl.program_id(2) * tq
    kcb = kc_ref[0, 0].astype(BF16)
    vcb = vc_ref[0, 0].astype(BF16)
    t_q = s0 + lax.broadcasted_iota(jnp.int32, (tq, ncmp), 0)
    n_i = lax.broadcasted_iota(jnp.int32, (tq, ncmp), 1)
    valid = (n_i * CMP_STRIDE + (CMP_LEN - 1)) <= t_q
    ovt = ovt_ref[...]
    gs = range(NSA_GROUP)
    sc = [_dot_nt(q_ref[0, :, g * HEAD_DIM:(g + 1) * HEAD_DIM], kcb) for g in gs]
    l = [jnp.where(valid, sc[g] + bias_ref[g], NEG_INF) for g in gs]
    m = [jnp.max(l[g], axis=-1, keepdims=True) for g in gs]
    e = [jnp.where(valid, jnp.exp(l[g] - m[g]), 0.0) for g in gs]
    den = [jnp.maximum(jnp.sum(e[g], axis=-1, keepdims=True), 1e-30) for g in gs]
    pb = [(e[g] / den[g]).astype(BF16) for g in gs]
    outs = [_dot(pb[g], vcb) for g in gs]
    imps = [_dot_nt(ovt, pb[g]) for g in gs]
    imp_t = functools.reduce(lambda acc, x: acc + x, imps)
    oc_ref[0] = jnp.concatenate(outs, axis=1)

    t_l = s0 + lax.broadcasted_iota(jnp.int32, (nslc, tq), 1)
    j_s = lax.broadcasted_iota(jnp.int32, (nslc, tq), 0)
    cur = lax.shift_right_logical(t_l, int(math.log2(SLC_BLOCK)))
    forced = (j_s == 0) | (j_s == cur) | (j_s == cur - 1)
    future = (j_s * SLC_BLOCK) > t_l
    score = jnp.where(future, NEG_INF, imp_t + jnp.where(forced, FORCED_BONUS, 0.0))
    rank = jnp.zeros((nslc, tq), F32)
    for i in range(nslc):
        r = score[i:i + 1, :]
        beats = (r > score) | ((r == score) & (j_s > i))
        rank = rank + jnp.where(beats, 1.0, 0.0)
    sel_t = jnp.where(rank < float(top_n), 1.0, 0.0).astype(BF16)
    eye = (lax.broadcasted_iota(jnp.int32, (tq, tq), 0) == lax.broadcasted_iota(jnp.int32, (tq, tq), 1))
    mask_ref[0, 0] = _dot_nt(jnp.where(eye, 1.0, 0.0).astype(BF16), sel_t).astype(BF16)


def _nsa_cmp(p16, kc, vc, bias_c, ovt, top_n):
    bsz, s, _ = p16.shape
    tq = ATT_TILE
    ncmp = kc.shape[2]
    nslc = ovt.shape[0]
    qw = NSA_GROUP * HEAD_DIM
    return pl.pallas_call(
        functools.partial(_nsa_cmp_kernel, top_n=top_n),
        grid=(bsz, NSA_KV_HEADS, s // tq),
        in_specs=[pl.BlockSpec((1, tq, qw), lambda b, h, i: (b, i, h)),
                  pl.BlockSpec((1, 1, ncmp, HEAD_DIM), lambda b, h, i: (b, h, 0, 0)),
                  pl.BlockSpec((1, 1, ncmp, HEAD_DIM), lambda b, h, i: (b, h, 0, 0)),
                  pl.BlockSpec((NSA_GROUP, tq, ncmp), lambda b, h, i: (h, i, 0)),
                  pl.BlockSpec(ovt.shape, lambda b, h, i: (0, 0))],
        out_specs=[pl.BlockSpec((1, tq, qw), lambda b, h, i: (b, i, h)),
                   pl.BlockSpec((1, 1, tq, nslc), lambda b, h, i: (b, h, i, 0))],
        out_shape=[jax.ShapeDtypeStruct((bsz, s, NSA_Q_W), F32),
                   jax.ShapeDtypeStruct((bsz, NSA_KV_HEADS, s, nslc), BF16)],
        compiler_params=_cparams("arbitrary", "arbitrary", "arbitrary"),
        name="nsa_cmp",
    )(p16, kc, vc, bias_c, ovt)


def _nsa_sw_kernel(q_ref, ks_ref, vs_ref, kw_ref, vw_ref, mask_ref, exp_ref, toep_ref, oc_ref, sm_ref, o_ref):
    tq = q_ref.shape[1]
    tk = ATT_TILE
    i = pl.program_id(1)
    s0 = i * tq
    gate = _sigmoid(sm_ref[0])
    t_q = s0 + lax.broadcasted_iota(jnp.int32, (tq, tk), 0)
    k_l = lax.broadcasted_iota(jnp.int32, (tq, tk), 1)
    n_win = WINDOW // tk
    pieces = []
    lane = lax.broadcasted_iota(jnp.int32, (tq, LANES), 1)
    assert NSA_KV_HEADS * HEAD_DIM == LANES and tq == tk
    for h in range(NSA_KV_HEADS):
        in_h = (lane >= h * HEAD_DIM) & (lane < (h + 1) * HEAD_DIM)
        qs = []
        for g in range(NSA_GROUP):
            hd = h * NSA_GROUP + g
            pair = q_ref[0, :, (hd // 2) * LANES:(hd // 2 + 1) * LANES].astype(F32)
            if hd % 2 != h:
                pair = pltpu.roll(pair, HEAD_DIM, 1)
            qs.append(jnp.where(in_h, pair, 0.0).astype(BF16))
        mq = mask_ref[0, h]

        def tile_update(j, carry, k_ref, v_ref, valid_fn, lo):
            us = range(TILES_PER_TRIP)
            gs = range(NSA_GROUP)
            kt_raw = [lo + TILES_PER_TRIP * j + u for u in us]
            kt = [jnp.minimum(kt_raw[u], i) for u in us]
            k = [k_ref[0, pl.ds(pl.multiple_of(kt[u] * tk, tk), tk), :] for u in us]
            v = [jnp.where(in_h, v_ref[0, pl.ds(pl.multiple_of(kt[u] * tk, tk), tk), :], 1.0) for u in us]
            mm = [jnp.minimum(i - kt[u], 2) for u in us]
            valid = [valid_fn(kt[u], kt_raw[u]) for u in us]
            sc = [[_dot_nt(qs[g], k[u]) for u in us] for g in gs]
            lg = [[jnp.where(valid[u], sc[g][u] + toep_ref[h * NSA_GROUP + g, mm[u]], NEG_INF) for u in us]
                  for g in gs]
            mx = [functools.reduce(jnp.maximum, [jnp.max(lg[g][u], axis=-1, keepdims=True) for u in us])
                  for g in gs]
            m_new = [jnp.maximum(carry[g][0], mx[g]) for g in gs]
            alpha = [jnp.exp(carry[g][0] - m_new[g]) for g in gs]
            e = [[jnp.where(valid[u], jnp.exp(lg[g][u] - m_new[g]), 0.0).astype(BF16) for u in us] for g in gs]
            pv = [functools.reduce(lambda a, b: a + b, [_dot(e[g][u], v[u]) for u in us]) for g in gs]
            return tuple((m_new[g], alpha[g] * carry[g][1] + pv[g]) for g in gs)

        def sel_valid(kt, kt_raw):
            mk = _dot(mq, exp_ref[kt])
            kpos = kt_raw * tk + k_l
            return (mk > 0.5) & (kpos <= t_q)

        def win_valid(kt, kt_raw):
            rel = t_q - (kt_raw * tk + k_l)
            return (rel >= 0) & (rel < WINDOW)

        init = tuple((jnp.full((tq, 1), NEG_INF, F32), jnp.zeros((tq, LANES), F32)) for _ in range(NSA_GROUP))
        trips = lambda lo: (i - lo + TILES_PER_TRIP) // TILES_PER_TRIP
        w_lo = jnp.maximum(i - n_win, 0)
        sel = lax.fori_loop(0, trips(0), lambda j, c: tile_update(j, c, ks_ref, vs_ref, sel_valid, 0), init)
        win = lax.fori_loop(0, trips(w_lo), lambda j, c: tile_update(j, c, kw_ref, vw_ref, win_valid, w_lo), init)
        hs = slice(h * HEAD_DIM, (h + 1) * HEAD_DIM)
        den = (1 - h) * HEAD_DIM
        for g in range(NSA_GROUP):
            hd = h * NSA_GROUP + g
            c = SM_GATES + hd * 3
            o_c = oc_ref[0, :, hd * HEAD_DIM:(hd + 1) * HEAD_DIM]
            o_s = (sel[g][1] / jnp.maximum(sel[g][1][:, den:den + 1], 1e-30))[:, hs]
            o_w = (win[g][1] / jnp.maximum(win[g][1][:, den:den + 1], 1e-30))[:, hs]
            pieces.append(gate[:, c:c + 1] * o_c + gate[:, c + 1:c + 2] * o_s + gate[:, c + 2:c + 3] * o_w)
    o_ref[0] = jnp.concatenate(pieces, axis=1).astype(o_ref.dtype)


def _nsa_sw(p16, p32, mask, expand, toep, o_c):
    bsz, s, _ = p16.shape
    tq = ATT_TILE
    nslc = mask.shape[3]
    kvb = NSA_Q_W // NSA_KV_W
    return pl.pallas_call(
        _nsa_sw_kernel,
        grid=(bsz, s // tq),
        in_specs=[pl.BlockSpec((1, tq, NSA_Q_W), lambda b, i: (b, i, 0)),
                  pl.BlockSpec((1, s, NSA_KV_W), lambda b, i: (b, 0, kvb)),
                  pl.BlockSpec((1, s, NSA_KV_W), lambda b, i: (b, 0, kvb + 1)),
                  pl.BlockSpec((1, s, NSA_KV_W), lambda b, i: (b, 0, kvb + 2)),
                  pl.BlockSpec((1, s, NSA_KV_W), lambda b, i: (b, 0, kvb + 3)),
                  pl.BlockSpec((1, NSA_KV_HEADS, tq, nslc), lambda b, i: (b, 0, i, 0)),
                  pl.BlockSpec(expand.shape, lambda b, i: (0, 0, 0)),
                  pl.BlockSpec(toep.shape, lambda b, i: (0, 0, 0, 0)),
                  pl.BlockSpec((1, tq, NSA_Q_W), lambda b, i: (b, i, 0)),
                  pl.BlockSpec((1, tq, LANES), lambda b, i: (b, i, C_SMALL // LANES))],
        out_specs=pl.BlockSpec((1, tq, NSA_Q_W), lambda b, i: (b, i, 0)),
        out_shape=jax.ShapeDtypeStruct((bsz, s, NSA_Q_W), BF16),
        compiler_params=_cparams("arbitrary", "arbitrary"),
        name="nsa_sw",
    )(p16, p16, p16, p16, p16, mask, expand, toep, o_c, p32)


def _deltanet_kernel(q_ref, k_ref, v_ref, z_ref, sm_ref, cw_ref, alog_ref, dt_ref, ng_ref, o_ref, xp_ref, st_ref):
    c = q_ref.shape[1]
    pad = 8
    ch = pl.program_id(1)

    @pl.when(ch == 0)
    def _():
        xp_ref[:, 0:pad, :] = jnp.zeros((3, pad, DN_W), F32)
        st_ref[...] = jnp.zeros(st_ref.shape, F32)

    conv = []
    for idx, r in enumerate((q_ref, k_ref, v_ref)):
        xp_ref[idx, pad:pad + c, :] = r[0]
        acc = None
        for j in range(DN_CONV):
            off = pad - (DN_CONV - 1) + j
            term = cw_ref[j:j + 1, idx * DN_W:(idx + 1) * DN_W] * xp_ref[idx, off:off + c, :]
            acc = term if acc is None else acc + term
        xp_ref[idx, 0:pad, :] = xp_ref[idx, c:c + pad, :]
        conv.append(_silu(acc))
    qa, ka, va = conv
    za = _silu(z_ref[0])

    sm = sm_ref[0]
    beta_all = _sigmoid(sm)
    g_all = -jnp.exp(alog_ref[...]) * _softplus(sm + dt_ref[...])
    row = lax.broadcasted_iota(jnp.int32, (c, c), 0)
    col = lax.broadcasted_iota(jnp.int32, (c, c), 1)
    tril = row >= col
    strict = row > col
    eye = row == col
    tril01 = jnp.where(tril, 1.0, 0.0).astype(BF16)
    ones01 = jnp.ones((c, c), BF16)
    gc_all = _dot_01(tril01, g_all)
    eg_all = jnp.exp(gc_all)
    eye_f = jnp.where(eye, 1.0, 0.0)
    ng = ng_ref[...]

    hh = range(DN_HEADS)
    hsl = [slice(h * DN_KDIM, (h + 1) * DN_KDIM) for h in hh]
    states = [st_ref[h] for h in hh]
    q = [qa[:, hsl[h]] for h in hh]
    k = [ka[:, hsl[h]] for h in hh]
    q = [q[h] * lax.rsqrt(jnp.sum(q[h] * q[h], axis=-1, keepdims=True) + EPS) * (DN_KDIM ** -0.5) for h in hh]
    k = [k[h] * lax.rsqrt(jnp.sum(k[h] * k[h], axis=-1, keepdims=True) + EPS) for h in hh]
    beta = [beta_all[:, SM_B + h:SM_B + h + 1] for h in hh]
    gcol = [gc_all[:, SM_A + h:SM_A + h + 1] for h in hh]
    egcol = [eg_all[:, SM_A + h:SM_A + h + 1] for h in hh]
    grow = _dot_01_many(ones01, [jnp.where(eye, gcol[h], 0.0) for h in hh])
    decay = [jnp.where(tril, jnp.exp(jnp.where(tril, gcol[h] - grow[h], 0.0)), 0.0) for h in hh]
    glast = [grow[h][:, c - 1:c] for h in hh]
    kb = [k[h] * beta[h] for h in hh]
    vb = [va[:, hsl[h]] * beta[h] for h in hh]
    kbf = [k[h].astype(BF16) for h in hh]
    kk = [_dot_nt(kb[h].astype(BF16), kbf[h]) for h in hh]
    a = [jnp.where(strict, kk[h] * decay[h], 0.0) for h in hh]
    t_inv = [eye_f - a[h] for h in hh]
    pw = a
    for _ in range(int(math.log2(c)) - 1):
        pw = _dot_x3_many(pw, pw)
        tp = _dot_x3_many(t_inv, pw)
        t_inv = [t_inv[h] + tp[h] for h in hh]
    rhs = [jnp.concatenate([vb[h], kb[h] * egcol[h]], axis=1) for h in hh]
    sol = _dot_x3_many(t_inv, rhs)
    qk = [_dot_nt(q[h].astype(BF16), kbf[h]) for h in hh]
    attn = [(qk[h] * decay[h]).astype(BF16) for h in hh]
    q_dec = [(q[h] * egcol[h]).astype(BF16) for h in hh]
    k_dec = [(k[h] * jnp.exp(glast[h] - gcol[h])).astype(BF16) for h in hh]
    sb = [states[h].astype(BF16) for h in hh]
    ws = [_dot(sol[h][:, DN_VDIM:].astype(BF16), sb[h]) for h in hh]
    v_new = [(sol[h][:, :DN_VDIM] - ws[h]).astype(BF16) for h in hh]
    o1 = [_dot(q_dec[h], sb[h]) for h in hh]
    o2 = [_dot(attn[h], v_new[h]) for h in hh]
    kv = [_dot_tn(k_dec[h], v_new[h]) for h in hh]
    for h in hh:
        st_ref[h] = states[h] * jnp.exp(glast[h][0:1, :]) + kv[h]
    o = [o1[h] + o2[h] for h in hh]
    outs = [o[h] * lax.rsqrt(jnp.mean(o[h] * o[h], axis=-1, keepdims=True) + EPS) * ng * za[:, hsl[h]] for h in hh]
    o_ref[0] = jnp.concatenate(outs, axis=1).astype(o_ref.dtype)


def _deltanet(p32, conv_w, alog_row, dt_row, norm_g):
    bsz, s, _ = p32.shape
    c = DN_CHUNK
    blk = lambda col: pl.BlockSpec((1, c, DN_W), lambda b, i: (b, i, col // DN_W))
    full2 = lambda a: pl.BlockSpec(a.shape, lambda b, i: (0, 0))
    ng = norm_g.reshape(1, DN_VDIM)
    return pl.pallas_call(
        _deltanet_kernel,
        grid=(bsz, s // c),
        in_specs=[blk(C_DNQ), blk(C_DNK), blk(C_DNV), blk(C_DNZ),
                  pl.BlockSpec((1, c, LANES), lambda b, i: (b, i, C_SMALL // LANES)),
                  full2(conv_w), full2(alog_row), full2(dt_row), full2(ng)],
        out_specs=pl.BlockSpec((1, c, DN_W), lambda b, i: (b, i, 0)),
        out_shape=jax.ShapeDtypeStruct((bsz, s, DN_W), BF16),
        scratch_shapes=[pltpu.VMEM((3, c + 8, DN_W), F32),
                        pltpu.VMEM((DN_HEADS, DN_KDIM, DN_VDIM), F32)],
        compiler_params=_cparams("arbitrary", "arbitrary"),
        name="deltanet",
    )(p32, p32, p32, p32, p32, conv_w, alog_row, dt_row, ng)


def _mix_out_kernel(x_ref, mod_ref, onsa_ref, odn_ref, gn_ref, gd_ref, wn_ref, wd_ref, wo_ref, o_ref):
    merged = (_sigmoid(gn_ref[0]) * _dot(onsa_ref[0], wn_ref[...])
              + _sigmoid(gd_ref[0]) * _dot(odn_ref[0], wd_ref[...]))
    y = _dot(merged.astype(BF16), wo_ref[...])
    o_ref[0] = x_ref[0] + mod_ref[0][2:3] * y


def _mix_out(x, mod, o_nsa, o_dn, p32, wn, wd, wo, tm=256):
    bsz, s, d = x.shape
    full2 = lambda a: pl.BlockSpec(a.shape, lambda b, i: (0, 0))
    return pl.pallas_call(
        _mix_out_kernel,
        grid=(bsz, s // tm),
        in_specs=[pl.BlockSpec((1, tm, d), lambda b, i: (b, i, 0)),
                  pl.BlockSpec((1, 6, d), lambda b, i: (b, 0, 0)),
                  pl.BlockSpec((1, tm, NSA_Q_W), lambda b, i: (b, i, 0)),
                  pl.BlockSpec((1, tm, DN_W), lambda b, i: (b, i, 0)),
                  pl.BlockSpec((1, tm, d), lambda b, i: (b, i, C_GNSA // d)),
                  pl.BlockSpec((1, tm, d), lambda b, i: (b, i, C_GDN // d)),
                  full2(wn), full2(wd), full2(wo)],
        out_specs=pl.BlockSpec((1, tm, d), lambda b, i: (b, i, 0)),
        out_shape=jax.ShapeDtypeStruct((bsz, s, d), F32),
        compiler_params=_cparams("arbitrary", "arbitrary"),
        name="mix_out",
    )(x, mod, o_nsa, o_dn, p32, p32, wn, wd, wo)


def _top_k_rows(s, idx, k, payload=None):
    vals, picks = [], []
    for _ in range(k):
        mx = jnp.max(s, axis=0, keepdims=True)
        am = jnp.min(jnp.where(s == mx, idx, 1e9), axis=0, keepdims=True)
        hit = idx == am
        vals.append(mx)
        if payload is None:
            picks.append(am)
        else:
            picks.append(jnp.sum(jnp.where(hit, payload, 0.0), axis=0, keepdims=True))
        s = jnp.where(hit, -jnp.inf, s)
    return jnp.concatenate(vals, axis=0), jnp.concatenate(picks, axis=0)


def _peer_route_kernel(x_ref, mod_ref, g_ref, wq_ref, k1_ref, k2_ref, h_ref, idx_ref, gate_ref):
    tm = x_ref.shape[1]
    h2 = _modulated_norm(x_ref[0], g_ref[...], mod_ref[0], 3, 4)
    hb = h2.astype(BF16)
    h_ref[0] = hb
    qry = _dot(hb, wq_ref[...])
    kidx = lax.broadcasted_iota(jnp.int32, (PEER_NKEYS, tm), 0).astype(F32)
    kk = PEER_TOPK
    hk = kk // 2
    b_iota = kidx[0:kk]
    cidx = jnp.concatenate([b_iota] + [a * float(kk) + b_iota[0:hk] for a in range(1, hk)]
                           + [(b_iota[hk:kk]) * float(kk)], axis=0)
    for h in range(PEER_HEADS):
        qh = qry[:, h * PEER_DKEY:(h + 1) * PEER_DKEY].astype(BF16)
        s1 = _dot_nt(k1_ref[h], qh[:, :PEER_HALF])
        s2 = _dot_nt(k2_ref[h], qh[:, PEER_HALF:])
        v1, i1 = _top_k_rows(s1, kidx, PEER_TOPK)
        v2, i2 = _top_k_rows(s2, kidx, PEER_TOPK)
        cand_s = jnp.concatenate([v1[0:1] + v2] + [v1[a:a + 1] + v2[0:hk] for a in range(1, hk)]
                                 + [v1[hk:kk] + v2[0:1]], axis=0)
        e1 = i1 * float(PEER_NKEYS)
        cand_i = jnp.concatenate([e1[0:1] + i2] + [e1[a:a + 1] + i2[0:hk] for a in range(1, hk)]
                                 + [e1[hk:kk] + i2[0:1]], axis=0)
        top_s, top_e = _top_k_rows(cand_s, cidx, PEER_TOPK, payload=cand_i)
        e = jnp.exp(top_s - top_s[0:1])
        gate = e / jnp.sum(e, axis=0, keepdims=True)
        idx_ref[h * PEER_TOPK:(h + 1) * PEER_TOPK, :] = (top_e * float(PACKED_ROWS)).astype(jnp.int32)
        gate_ref[h * PEER_TOPK:(h + 1) * PEER_TOPK, :] = gate


def _peer_route(x1, mod, g, wq, k1, k2, tm=128):
    bsz, s, d = x1.shape
    nt = s // tm
    ne = PEER_HEADS * PEER_TOPK
    full = lambda a: pl.BlockSpec(a.shape, lambda b, i: (0,) * a.ndim)
    return pl.pallas_call(
        _peer_route_kernel,
        grid=(bsz, nt),
        in_specs=[pl.BlockSpec((1, tm, d), lambda b, i: (b, i, 0)),
                  pl.BlockSpec((1, 6, d), lambda b, i: (b, 0, 0)),
                  pl.BlockSpec((1, d), lambda b, i: (0, 0)),
                  full(wq), full(k1), full(k2)],
        out_specs=[pl.BlockSpec((1, tm, d), lambda b, i: (b, i, 0)),
                   pl.BlockSpec((ne, tm), lambda b, i: (0, b * nt + i)),
                   pl.BlockSpec((ne, tm), lambda b, i: (0, b * nt + i))],
        out_shape=[jax.ShapeDtypeStruct((bsz, s, d), BF16),
                   jax.ShapeDtypeStruct((ne, bsz * s), jnp.int32),
                   jax.ShapeDtypeStruct((ne, bsz * s), F32)],
        compiler_params=_cparams("arbitrary", "arbitrary"),
        name="peer_route",
    )(x1, mod, g.reshape(1, d), wq, k1, k2)


PEER_TOK = 32
EXPERT_TILE_ROWS = 8
PACKED_ROWS = EXPERT_TILE_ROWS // 2


def _load_table_once(tab_hbm, tab_vmem, sem):
    @pl.when(pl.program_id(0) == 0)
    def _():
        cp = pltpu.make_async_copy(tab_hbm, tab_vmem, sem)
        cp.start()
        cp.wait()


def _pack_table(tab):
    e = tab.shape[0]
    bits = lax.bitcast_convert_type(tab.astype(BF16).reshape(e * PACKED_ROWS, 2, LANES), jnp.uint16)
    bits = bits.astype(jnp.uint32)
    return lax.bitcast_convert_type(bits[:, 0, :] | (bits[:, 1, :] << 16), jnp.int32)


def _stage_experts(idx_ref, tab_vmem, stage, nsel):
    for r in range(nsel):
        row = pl.multiple_of(idx_ref[0, 0, r], PACKED_ROWS)
        stage[r * PACKED_ROWS:(r + 1) * PACKED_ROWS, :] = tab_vmem[pl.ds(row, PACKED_ROWS), :]


def _staged_tiles(stage, t, ncol):
    return pltpu.bitcast(stage[t * (ncol // 2):(t + 1) * (ncol // 2), :], BF16)


def _diag_mask(ncol):
    row = lax.broadcasted_iota(jnp.int32, (EXPERT_TILE_ROWS, ncol), 0)
    col = lax.broadcasted_iota(jnp.int32, (EXPERT_TILE_ROWS, ncol), 1)
    return (col & (EXPERT_TILE_ROWS - 1)) == row


def _peer_act_kernel(idx_ref, tab_hbm, h_ref, gate_ref, grp_ref, w_ref, tab_vmem, stage, sem, *, rows_per_tok):
    _load_table_once(tab_hbm, tab_vmem, sem)
    _stage_experts(idx_ref, tab_vmem, stage, PEER_TOK * rows_per_tok)
    ncol = rows_per_tok * EXPERT_TILE_ROWS
    diag = _diag_mask(ncol)
    sums = []
    for t in range(PEER_TOK):
        m = _staged_tiles(stage, t, ncol)
        rt = _dot_nt(h_ref[t], m)
        sums.append(jnp.sum(jnp.where(diag, rt, 0.0), axis=0, keepdims=True))
    part = jnp.concatenate(sums, axis=0)
    hi, mid, lo = _split3(part)
    grp = grp_ref[...]
    act = _dot(hi, grp) + (_dot(mid, grp) + _dot(lo, grp))
    w_ref[...] = gate_ref[...] * _gelu_tanh(act)


def _peer_out_kernel(idx_ref, tab_hbm, w_ref, exp_ref, x_ref, mod_ref, fg_ref, o_ref, tab_vmem, stage, sem,
                     *, rows_per_tok):
    _load_table_once(tab_hbm, tab_vmem, sem)
    _stage_experts(idx_ref, tab_vmem, stage, PEER_TOK * rows_per_tok)
    ncol = rows_per_tok * EXPERT_TILE_ROWS
    diag = _diag_mask(ncol)
    wh, wl = _split2(w_ref[...])
    ex = exp_ref[...]
    weh = _dot(wh, ex)
    wel = _dot(wl, ex)
    g2 = mod_ref[0, 5]
    fg = fg_ref[...]
    nel = float(EXPERT_TILE_ROWS * LANES)
    for t in range(PEER_TOK):
        lhs = jnp.concatenate([jnp.where(diag, weh[t:t + 1], 0.0), jnp.where(diag, wel[t:t + 1], 0.0)], axis=0)
        o16 = _dot(lhs.astype(BF16), _staged_tiles(stage, t, ncol))
        y = o16[0:EXPERT_TILE_ROWS] + o16[EXPERT_TILE_ROWS:]
        xo = x_ref[t] + g2 * y
        ms = jnp.sum(jnp.sum(xo * xo, axis=1, keepdims=True), axis=0, keepdims=True) / nel
        o_ref[t] = xo * lax.rsqrt(ms + EPS) * fg


def _peer_experts(idx3, tab_u, tab_v, h3, gate, x3, mod4, fg3, s):
    n = h3.shape[0]
    ne = gate.shape[1]
    nstep = n // PEER_TOK
    nsel = PEER_TOK * ne
    ncol = ne * EXPERT_TILE_ROWS
    grp = jnp.asarray(np.arange(ncol)[:, None] // EXPERT_TILE_ROWS == np.arange(ne)[None, :], BF16)
    idx_spec = pl.BlockSpec((1, 1, nsel), lambda t: (t, 0, 0), memory_space=pltpu.SMEM)
    any_spec = pl.BlockSpec(memory_space=pl.ANY)
    tile_spec = pl.BlockSpec((PEER_TOK, EXPERT_TILE_ROWS, LANES), lambda t: (t, 0, 0))
    row_spec = pl.BlockSpec((PEER_TOK, ne), lambda t: (t, 0))
    scratch = [pltpu.VMEM(tab_u.shape, jnp.int32), pltpu.VMEM((PEER_TOK * ncol // 2, LANES), jnp.int32),
               pltpu.SemaphoreType.DMA]
    w = pl.pallas_call(
        functools.partial(_peer_act_kernel, rows_per_tok=ne),
        grid=(nstep,),
        in_specs=[idx_spec, any_spec, tile_spec, row_spec, pl.BlockSpec(grp.shape, lambda t: (0, 0))],
        out_specs=row_spec,
        out_shape=jax.ShapeDtypeStruct((n, ne), F32),
        scratch_shapes=scratch,
        compiler_params=_cparams("arbitrary"),
        name="peer_act",
    )(idx3, tab_u, h3, gate, grp)
    return pl.pallas_call(
        functools.partial(_peer_out_kernel, rows_per_tok=ne),
        grid=(nstep,),
        in_specs=[idx_spec, any_spec, row_spec, pl.BlockSpec((ne, ncol), lambda t: (0, 0)), tile_spec,
                  pl.BlockSpec((1, 6, EXPERT_TILE_ROWS, LANES), lambda t: ((t * PEER_TOK) // s, 0, 0, 0)),
                  pl.BlockSpec((EXPERT_TILE_ROWS, LANES), lambda t: (0, 0))],
        out_specs=tile_spec,
        out_shape=jax.ShapeDtypeStruct((n, EXPERT_TILE_ROWS, LANES), F32),
        scratch_shapes=scratch,
        compiler_params=_cparams("arbitrary"),
        name="peer_out",
    )(idx3, tab_v, w, grp.T, x3, mod4, fg3)


def _rel_bucket(dist):
    dist = jnp.maximum(dist, 0)
    max_exact = REL_BUCKETS // 2
    scaled = jnp.log(jnp.maximum(dist, 1).astype(F32) / max_exact) / math.log(REL_MAX_DIST / max_exact)
    large = jnp.minimum(max_exact + (scaled * (REL_BUCKETS - max_exact)).astype(jnp.int32), REL_BUCKETS - 1)
    return jnp.where(dist < max_exact, dist, large)


def _bias_tables(rel_bias, s, ncmp):
    dist_bias = rel_bias[_rel_bucket(jnp.arange(s))].T
    nq = s // ATT_TILE
    per_tile = ATT_TILE // CMP_STRIDE
    r = np.arange(ATT_TILE)[:, None]
    k0 = per_tile * (nq - 1)
    k = np.arange(k0 + ncmp)[None, :]
    dist_g = r - CMP_STRIDE * (k - k0) - (CMP_LEN - 1)
    g = dist_bias[:, np.clip(dist_g, 0, s - 1)]
    wins = jnp.stack([g[:, :, k0 - per_tile * i:k0 - per_tile * i + ncmp] for i in range(nq)], axis=1)
    bias_c = wins.reshape(rel_bias.shape[1], s, ncmp)
    ii = jnp.arange(ATT_TILE)[:, None]
    jj = jnp.arange(ATT_TILE)[None, :]
    dist = jnp.stack([m * ATT_TILE + ii - jj for m in range(3)])
    toep = dist_bias[:, jnp.clip(dist, 0, s - 1)]
    return bias_c, toep


def _check_bucket_saturation():
    d = np.arange(ATT_TILE + 1, 1 << 16)
    max_exact = REL_BUCKETS // 2
    scaled = np.log(d.astype(np.float32) / max_exact) / math.log(REL_MAX_DIST / max_exact)
    large = np.minimum(max_exact + (scaled * (REL_BUCKETS - max_exact)).astype(np.int32), REL_BUCKETS - 1)
    assert (large == REL_BUCKETS - 1).all()


_check_bucket_saturation()


def _token_mixer(x, mod, rel_bias, norm_g, w_in, cmp_pos_emb, w_ck1, w_ck2, w_cv1, w_cv2, conv_w, a_log, dt_bias,
                 dn_norm_g, w_branch_nsa, w_branch_dn, w_out):
    bsz, s, d = x.shape
    assert s % ATT_TILE == 0 and s % CMP_STRIDE == 0 and WINDOW % ATT_TILE == 0 and d == C_GDN - C_GNSA
    pts = np.cumsum(IN_SPLIT_SIZES + (d, d))[:-1]
    assert w_in.shape[1] == sum(IN_SPLIT_SIZES) + 2 * d
    (w_q, w_kc, w_vc, w_ks, w_vs, w_kw, w_vw, w_gates, w_dq, w_dk, w_dv, w_dz, w_db, w_da, w_gn, w_gd) = (
        jnp.split(w_in, [int(p) for p in pts], axis=1))
    w16 = jnp.concatenate([w_q * (HEAD_DIM ** -0.5), w_ks, w_vs, w_kw, w_vw], axis=1).astype(BF16)
    small_pad = jnp.zeros((d, LANES - NSA_HEADS * 3 - 2 * DN_HEADS), w_in.dtype)
    w32 = jnp.concatenate([w_dq, w_dk, w_dv, w_dz, w_gn, w_gd, w_kc, w_vc, w_gates, w_db, w_da, small_pad],
                          axis=1).astype(BF16)
    assert w16.shape[1] == C16 and w32.shape[1] == C32
    p16, p32 = _proj_in(x, mod, norm_g, w16, w32)

    nrow = s // CMP_STRIDE
    k2 = p32[:, :, C_KC:C_KC + NSA_KV_W].reshape(bsz, nrow, CMP_STRIDE * NSA_KV_W)
    v2 = p32[:, :, C_VC:C_VC + NSA_KV_W].reshape(bsz, nrow, CMP_STRIDE * NSA_KV_W)
    kc, vc = _compress(k2, v2, cmp_pos_emb, w_ck1, w_ck2, w_cv1, w_cv2)
    nslc = s // SLC_BLOCK
    top_n = min(SLC_TOPN, nslc)
    bias_c, toep = _bias_tables(rel_bias, s, nrow)
    cmp_start = np.arange(nrow) * CMP_STRIDE
    slc_start = np.arange(nslc) * SLC_BLOCK
    ovt = ((cmp_start[None, :] <= slc_start[:, None] + SLC_BLOCK - 1)
           & (cmp_start[None, :] + CMP_LEN - 1 >= slc_start[:, None]))
    ovt = jnp.asarray(ovt, BF16)
    o_c, mask = _nsa_cmp(p16, kc, vc, bias_c, ovt, top_n)
    nkt = s // ATT_TILE
    key_blk = (np.arange(nkt)[:, None, None] * ATT_TILE + np.arange(ATT_TILE)[None, None, :]) // SLC_BLOCK
    expand = jnp.asarray(key_blk == np.arange(nslc)[None, :, None], BF16)
    o_nsa = _nsa_sw(p16, p32, mask, expand, toep, o_c)

    lane_row = lambda vec: jnp.zeros((1, LANES), F32).at[0, SM_A:SM_A + DN_HEADS].set(vec.astype(F32))
    o_dn = _deltanet(p32, conv_w, lane_row(a_log), lane_row(dt_bias), dn_norm_g)

    return _mix_out(x, mod, o_nsa, o_dn, p32, w_branch_nsa.astype(BF16), w_branch_dn.astype(BF16),
                    w_out.astype(BF16))


def _peer_block(x1, mod, norm_g, final_g, w_query, keys1, keys2, expert_u, expert_v):
    bsz, s, d = x1.shape
    n = bsz * s
    ne = PEER_HEADS * PEER_TOPK
    h2, idx_t, gate_t = _peer_route(x1, mod, norm_g, w_query.astype(BF16), keys1.astype(BF16), keys2.astype(BF16))
    assert d == EXPERT_TILE_ROWS * LANES
    nstep = n // PEER_TOK
    idx3 = idx_t.T.reshape(nstep, 1, PEER_TOK * ne)
    tile = (EXPERT_TILE_ROWS, LANES)
    tab_u = _pack_table(expert_u)
    tab_v = _pack_table(expert_v)
    out = _peer_experts(idx3, tab_u, tab_v, h2.reshape((n,) + tile), gate_t.T, x1.reshape((n,) + tile),
                        mod.reshape((bsz, 6) + tile), final_g.reshape(tile), s)
    return out.reshape(bsz, s, d)


def kernel(x, c, rel_bias, final_g, w_ada, b_ada, norm1_g, w_in, cmp_pos_emb, w_cmp_k1, w_cmp_k2, w_cmp_v1,
           w_cmp_v2, dn_conv_w, dn_A_log, dn_dt_bias, dn_norm_g, w_branch_nsa, w_branch_dn, w_out, norm2_g,
           peer_w_query, peer_keys1, peer_keys2, peer_u, peer_v):
    depth = w_ada.shape[0]
    assert depth == 1, "the final norm is fused into the last PEER kernel; one layer is supported"
    bsz, s, d = x.shape
    mod = _ada(c, w_ada[0], b_ada[0]).reshape(bsz, 6, d)
    x1 = _token_mixer(x, mod, rel_bias, norm1_g[0], w_in[0], cmp_pos_emb[0], w_cmp_k1[0], w_cmp_k2[0], w_cmp_v1[0],
                      w_cmp_v2[0], dn_conv_w[0], dn_A_log[0], dn_dt_bias[0], dn_norm_g[0], w_branch_nsa[0],
                      w_branch_dn[0], w_out[0])
    return _peer_block(x1, mod, norm2_g[0], final_g, peer_w_query[0], peer_keys1[0], peer_keys2[0], peer_u[0],
                       peer_v[0])
```

```python
import functools
import math

import numpy as np
import jax
import jax.numpy as jnp
from jax import lax
from jax.experimental import pallas as pl
from jax.experimental.pallas import tpu as pltpu

F32 = jnp.float32
BF16 = jnp.bfloat16

NSA_HEADS = 8
NSA_KV_HEADS = 2
NSA_GROUP = NSA_HEADS // NSA_KV_HEADS
HEAD_DIM = 64
CMP_LEN = 32
CMP_STRIDE = 16
CMP_HIDDEN = 64
SLC_BLOCK = 64
SLC_TOPN = 16
WINDOW = 512
FORCED_BONUS = 1.0e4
DN_HEADS = 8
DN_KDIM = 64
DN_VDIM = 64
DN_CONV = 4
DN_CHUNK = 64
PEER_HEADS = 8
PEER_NKEYS = 128
PEER_DKEY = 128
PEER_HALF = PEER_DKEY // 2
PEER_TOPK = 16
REL_BUCKETS = 32
REL_MAX_DIST = 128
EPS = 1e-6
NEG_INF = -1e30

NSA_Q_W = NSA_HEADS * HEAD_DIM
NSA_KV_W = NSA_KV_HEADS * HEAD_DIM
DN_W = DN_HEADS * DN_KDIM
IN_SPLIT_SIZES = (NSA_Q_W, NSA_KV_W, NSA_KV_W, NSA_KV_W, NSA_KV_W, NSA_KV_W, NSA_KV_W, NSA_HEADS * 3,
                  DN_W, DN_W, DN_W, DN_W, DN_HEADS, DN_HEADS)

LANES = 128
ATT_TILE = 128
TILES_PER_TRIP = 4
VMEM_LIMIT = 56 * 1024 * 1024

C_DNQ, C_DNK, C_DNV, C_DNZ = 0, 512, 1024, 1536
C_GNSA, C_GDN = 2048, 3072
C_KC, C_VC, C_SMALL = 4096, 4224, 4352
C32 = 4480
SM_GATES, SM_B, SM_A = 0, 24, 32
C16 = 1024


def _gelu_tanh(x):
    return 0.5 * x * (1.0 + jnp.tanh(math.sqrt(2.0 / math.pi) * (x + 0.044715 * (x * x * x))))


def _sigmoid(x):
    return 1.0 / (1.0 + jnp.exp(-x))


def _silu(x):
    return x * _sigmoid(x)


def _softplus(x):
    return jnp.maximum(x, 0.0) + jnp.log1p(jnp.exp(-jnp.abs(x)))


def _dot(a, b):
    return jnp.dot(a, b, preferred_element_type=F32)


def _dot_nt(a, b):
    return lax.dot_general(a, b, (((1,), (1,)), ((), ())), preferred_element_type=F32)


def _dot_tn(a, b):
    return lax.dot_general(a, b, (((0,), (0,)), ((), ())), preferred_element_type=F32)


def _split2(x):
    hi = x.astype(BF16)
    lo = (x - hi.astype(F32)).astype(BF16)
    return hi, lo


def _split3(x):
    hi = x.astype(BF16)
    r = x - hi.astype(F32)
    mid = r.astype(BF16)
    lo = (r - mid.astype(F32)).astype(BF16)
    return hi, mid, lo


def _dot_x3(a, b):
    ah, al = _split2(a)
    bh, bl = _split2(b)
    return _dot(ah, bh) + (_dot(ah, bl) + _dot(al, bh))


def _dot_01(m01, x):
    hi, mid, lo = _split3(x)
    return _dot(m01, hi) + (_dot(m01, mid) + _dot(m01, lo))


def _dot_x3_many(a_list, b_list):
    n = range(len(a_list))
    asp = [_split2(a) for a in a_list]
    bsp = [_split2(b) for b in b_list]
    hh = [_dot(asp[i][0], bsp[i][0]) for i in n]
    hl = [_dot(asp[i][0], bsp[i][1]) for i in n]
    lh = [_dot(asp[i][1], bsp[i][0]) for i in n]
    return [hh[i] + (hl[i] + lh[i]) for i in n]


def _dot_01_many(m01, x_list):
    n = range(len(x_list))
    sp = [_split3(x) for x in x_list]
    d0 = [_dot(m01, sp[i][0]) for i in n]
    d1 = [_dot(m01, sp[i][1]) for i in n]
    d2 = [_dot(m01, sp[i][2]) for i in n]
    return [d0[i] + (d1[i] + d2[i]) for i in n]


def _cparams(*sem):
    return pltpu.CompilerParams(dimension_semantics=sem, vmem_limit_bytes=VMEM_LIMIT)


def _ada_kernel(c_ref, w_ref, b_ref, o_ref):
    cond = _silu(c_ref[...])
    o_ref[...] = _dot(cond.astype(BF16), w_ref[...].astype(BF16)) + b_ref[...]


def _ada(c, w_ada, b_ada):
    bsz, d = c.shape
    n_out = w_ada.shape[1]
    blk = d
    return pl.pallas_call(
        _ada_kernel,
        grid=(n_out // blk,),
        in_specs=[pl.BlockSpec((bsz, d), lambda j: (0, 0)),
                  pl.BlockSpec((d, blk), lambda j: (0, j)),
                  pl.BlockSpec((1, blk), lambda j: (0, j))],
        out_specs=pl.BlockSpec((bsz, blk), lambda j: (0, j)),
        out_shape=jax.ShapeDtypeStruct((bsz, n_out), F32),
        compiler_params=_cparams("arbitrary"),
        name="ada",
    )(c, w_ada, b_ada.reshape(1, n_out))


def _modulated_norm(x, g, mod, shift_row, scale_row):
    var = jnp.mean(x * x, axis=-1, keepdims=True)
    y = x * lax.rsqrt(var + EPS) * g
    return y * (1.0 + mod[scale_row:scale_row + 1]) + mod[shift_row:shift_row + 1]


def _proj_in_kernel(x_ref, mod_ref, g_ref, w16_ref, w32_ref, o16_ref, o32_ref):
    h = _modulated_norm(x_ref[0], g_ref[...], mod_ref[0], 0, 1).astype(BF16)
    step = 512
    for c0 in range(0, o16_ref.shape[2], step):
        c1 = min(c0 + step, o16_ref.shape[2])
        o16_ref[0, :, c0:c1] = _dot(h, w16_ref[:, c0:c1]).astype(BF16)
    for c0 in range(0, o32_ref.shape[2], step):
        c1 = min(c0 + step, o32_ref.shape[2])
        o32_ref[0, :, c0:c1] = _dot(h, w32_ref[:, c0:c1])


def _proj_in(x, mod, g, w16, w32, tm=256):
    bsz, s, d = x.shape
    return pl.pallas_call(
        _proj_in_kernel,
        grid=(bsz, s // tm),
        in_specs=[pl.BlockSpec((1, tm, d), lambda b, i: (b, i, 0)),
                  pl.BlockSpec((1, 6, d), lambda b, i: (b, 0, 0)),
                  pl.BlockSpec((1, d), lambda b, i: (0, 0)),
                  pl.BlockSpec(w16.shape, lambda b, i: (0, 0)),
                  pl.BlockSpec(w32.shape, lambda b, i: (0, 0))],
        out_specs=[pl.BlockSpec((1, tm, w16.shape[1]), lambda b, i: (b, i, 0)),
                   pl.BlockSpec((1, tm, w32.shape[1]), lambda b, i: (b, i, 0))],
        out_shape=[jax.ShapeDtypeStruct((bsz, s, w16.shape[1]), BF16),
                   jax.ShapeDtypeStruct((bsz, s, w32.shape[1]), F32)],
        compiler_params=_cparams("arbitrary", "arbitrary"),
        name="proj_in",
    )(x, mod, g.reshape(1, d), w16, w32)


def _compress_kernel(k2_ref, v2_ref, posa_ref, posb_ref, ak_ref, bk_ref, w2k_ref, av_ref, bv_ref, w2v_ref,
                     kc_ref, vc_ref):
    nrow = k2_ref.shape[1]

    def one(x_ref, a_ref, b_ref, w2_ref, o_ref):
        x = x_ref[0]
        ya = _dot((x + posa_ref[...]).astype(BF16), a_ref[...])
        yb = _dot((x + posb_ref[...]).astype(BF16), b_ref[...])
        hid = _gelu_tanh(ya + pltpu.roll(yb, nrow - 1, 0))
        out = _dot(hid.astype(BF16), w2_ref[...])
        for h in range(NSA_KV_HEADS):
            o_ref[0, h] = out[:, h * HEAD_DIM:(h + 1) * HEAD_DIM]

    one(k2_ref, ak_ref, bk_ref, w2k_ref, kc_ref)
    one(v2_ref, av_ref, bv_ref, w2v_ref, vc_ref)


def _expand_cmp_weights(w1, w2):
    half = CMP_LEN // 2
    w1r = w1.reshape(2, half, HEAD_DIM, CMP_HIDDEN)
    eye = jnp.eye(NSA_KV_HEADS, dtype=w1.dtype)
    exp = jnp.einsum('pldj,hg->plhdgj', w1r, eye).reshape(2, half * NSA_KV_HEADS * HEAD_DIM,
                                                          NSA_KV_HEADS * CMP_HIDDEN)
    w2e = jnp.einsum('jd,hg->hjgd', w2, eye).reshape(NSA_KV_HEADS * CMP_HIDDEN, NSA_KV_HEADS * HEAD_DIM)
    return exp[0].astype(BF16), exp[1].astype(BF16), w2e.astype(BF16)


def _compress(k2, v2, pos_emb, w_ck1, w_ck2, w_cv1, w_cv2):
    bsz, nrow, width = k2.shape
    half = CMP_LEN // 2
    pos = jnp.broadcast_to(pos_emb.reshape(2, half, 1, HEAD_DIM), (2, half, NSA_KV_HEADS, HEAD_DIM))
    pos = pos.reshape(2, 1, width)
    ak, bk, w2k = _expand_cmp_weights(w_ck1, w_ck2)
    av, bv, w2v = _expand_cmp_weights(w_cv1, w_cv2)
    full2 = lambda a: pl.BlockSpec(a.shape, lambda b: (0, 0))
    out_sds = jax.ShapeDtypeStruct((bsz, NSA_KV_HEADS, nrow, HEAD_DIM), F32)
    out_spec = pl.BlockSpec((1, NSA_KV_HEADS, nrow, HEAD_DIM), lambda b: (b, 0, 0, 0))
    return pl.pallas_call(
        _compress_kernel,
        grid=(bsz,),
        in_specs=[pl.BlockSpec((1, nrow, width), lambda b: (b, 0, 0)),
                  pl.BlockSpec((1, nrow, width), lambda b: (b, 0, 0)),
                  full2(pos[0]), full2(pos[1]), full2(ak), full2(bk), full2(w2k), full2(av), full2(bv), full2(w2v)],
        out_specs=[out_spec, out_spec],
        out_shape=[out_sds, out_sds],
        compiler_params=_cparams("arbitrary"),
        name="compress",
    )(k2, v2, pos[0], pos[1], ak, bk, w2k, av, bv, w2v)


def _nsa_cmp_kernel(q_ref, kc_ref, vc_ref, bias_ref, ovt_ref, oc_ref, mask_ref, *, top_n):
    tq = q_ref.shape[1]
    ncmp = kc_ref.shape[2]
    nslc = ovt_ref.shape[0]
    s0 = pl.program_id(2) * tq
    kcb = kc_ref[0, 0].astype(BF16)
    vcb = vc_ref[0, 0].astype(BF16)
    t_q = s0 + lax.broadcasted_iota(jnp.int32, (tq, ncmp), 0)
    n_i = lax.broadcasted_iota(jnp.int32, (tq, ncmp), 1)
    valid = (n_i * CMP_STRIDE + (CMP_LEN - 1)) <= t_q
    ovt = ovt_ref[...]
    gs = range(NSA_GROUP)
    sc = [_dot_nt(q_ref[0, :, g * HEAD_DIM:(g + 1) * HEAD_DIM], kcb) for g in gs]
    l = [jnp.where(valid, sc[g] + bias_ref[g], NEG_INF) for g in gs]
    m = [jnp.max(l[g], axis=-1, keepdims=True) for g in gs]
    e = [jnp.where(valid, jnp.exp(l[g] - m[g]), 0.0) for g in gs]
    den = [jnp.maximum(jnp.sum(e[g], axis=-1, keepdims=True), 1e-30) for g in gs]
    pb = [(e[g] / den[g]).astype(BF16) for g in gs]
    outs = [_dot(pb[g], vcb) for g in gs]
    imps = [_dot_nt(ovt, pb[g]) for g in gs]
    imp_t = functools.reduce(lambda acc, x: acc + x, imps)
    oc_ref[0] = jnp.concatenate(outs, axis=1)

    t_l = s0 + lax.broadcasted_iota(jnp.int32, (nslc, tq), 1)
    j_s = lax.broadcasted_iota(jnp.int32, (nslc, tq), 0)
    cur = lax.shift_right_logical(t_l, int(math.log2(SLC_BLOCK)))
    forced = (j_s == 0) | (j_s == cur) | (j_s == cur - 1)
    future = (j_s * SLC_BLOCK) > t_l
    score = jnp.where(future, NEG_INF, imp_t + jnp.where(forced, FORCED_BONUS, 0.0))
    rank = jnp.zeros((nslc, tq), F32)
    for i in range(nslc):
        r = score[i:i + 1, :]
        beats = (r > score) | ((r == score) & (j_s > i))
        rank = rank + jnp.where(beats, 1.0, 0.0)
    sel_t = jnp.where(rank < float(top_n), 1.0, 0.0).astype(BF16)
    eye = (lax.broadcasted_iota(jnp.int32, (tq, tq), 0) == lax.broadcasted_iota(jnp.int32, (tq, tq), 1))
    mask_ref[0, 0] = _dot_nt(jnp.where(eye, 1.0, 0.0).astype(BF16), sel_t).astype(BF16)


def _nsa_cmp(p16, kc, vc, bias_c, ovt, top_n):
    bsz, s, _ = p16.shape
    tq = ATT_TILE
    ncmp = kc.shape[2]
    nslc = ovt.shape[0]
    qw = NSA_GROUP * HEAD_DIM
    return pl.pallas_call(
        functools.partial(_nsa_cmp_kernel, top_n=top_n),
        grid=(bsz, NSA_KV_HEADS, s // tq),
        in_specs=[pl.BlockSpec((1, tq, qw), lambda b, h, i: (b, i, h)),
                  pl.BlockSpec((1, 1, ncmp, HEAD_DIM), lambda b, h, i: (b, h, 0, 0)),
                  pl.BlockSpec((1, 1, ncmp, HEAD_DIM), lambda b, h, i: (b, h, 0, 0)),
                  pl.BlockSpec((NSA_GROUP, tq, ncmp), lambda b, h, i: (h, i, 0)),
                  pl.BlockSpec(ovt.shape, lambda b, h, i: (0, 0))],
        out_specs=[pl.BlockSpec((1, tq, qw), lambda b, h, i: (b, i, h)),
                   pl.BlockSpec((1, 1, tq, nslc), lambda b, h, i: (b, h, i, 0))],
        out_shape=[jax.ShapeDtypeStruct((bsz, s, NSA_Q_W), F32),
                   jax.ShapeDtypeStruct((bsz, NSA_KV_HEADS, s, nslc), BF16)],
        compiler_params=_cparams("arbitrary", "arbitrary", "arbitrary"),
        name="nsa_cmp",
    )(p16, kc, vc, bias_c, ovt)


def _nsa_sw_kernel(q_ref, ks_ref, vs_ref, kw_ref, vw_ref, mask_ref, exp_ref, toep_ref, oc_ref, sm_ref, o_ref):
    tq = q_ref.shape[1]
    tk = ATT_TILE
    i = pl.program_id(1)
    s0 = i * tq
    gate = _sigmoid(sm_ref[0])
    t_q = s0 + lax.broadcasted_iota(jnp.int32, (tq, tk), 0)
    k_l = lax.broadcasted_iota(jnp.int32, (tq, tk), 1)
    n_win = WINDOW // tk
    pieces = []
    lane = lax.broadcasted_iota(jnp.int32, (tq, LANES), 1)
    assert NSA_KV_HEADS * HEAD_DIM == LANES and tq == tk
    for h in range(NSA_KV_HEADS):
        in_h = (lane >= h * HEAD_DIM) & (lane < (h + 1) * HEAD_DIM)
        qs = []
        for g in range(NSA_GROUP):
            hd = h * NSA_GROUP + g
            pair = q_ref[0, :, (hd // 2) * LANES:(hd // 2 + 1) * LANES].astype(F32)
            if hd % 2 != h:
                pair = pltpu.roll(pair, HEAD_DIM, 1)
            qs.append(jnp.where(in_h, pair, 0.0).astype(BF16))
        mq = mask_ref[0, h]

        def tile_update(j, carry, k_ref, v_ref, valid_fn, lo, per):
            us = range(per)
            gs = range(NSA_GROUP)
            kt_raw = [lo + per * j + u for u in us]
            kt = [jnp.minimum(kt_raw[u], i) for u in us]
            k = [k_ref[0, pl.ds(pl.multiple_of(kt[u] * tk, tk), tk), :] for u in us]
            v = [jnp.where(in_h, v_ref[0, pl.ds(pl.multiple_of(kt[u] * tk, tk), tk), :], 1.0) for u in us]
            mm = [jnp.minimum(i - kt[u], 2) for u in us]
            valid = [valid_fn(kt[u], kt_raw[u]) for u in us]
            sc = [[_dot_nt(qs[g], k[u]) for u in us] for g in gs]
            lg = [[jnp.where(valid[u], sc[g][u] + toep_ref[h * NSA_GROUP + g, mm[u]], NEG_INF) for u in us]
                  for g in gs]
            mx = [functools.reduce(jnp.maximum, [jnp.max(lg[g][u], axis=-1, keepdims=True) for u in us])
                  for g in gs]
            m_new = [jnp.maximum(carry[g][0], mx[g]) for g in gs]
            alpha = [jnp.exp(carry[g][0] - m_new[g]) for g in gs]
            e = [[jnp.where(valid[u], jnp.exp(lg[g][u] - m_new[g]), 0.0).astype(BF16) for u in us] for g in gs]
            pv = [functools.reduce(lambda a, b: a + b, [_dot(e[g][u], v[u]) for u in us]) for g in gs]
            return tuple((m_new[g], alpha[g] * carry[g][1] + pv[g]) for g in gs)

        def sel_valid(kt, kt_raw):
            mk = _dot(mq, exp_ref[kt])
            kpos = kt_raw * tk + k_l
            return (mk > 0.5) & (kpos <= t_q)

        def win_valid(kt, kt_raw):
            rel = t_q - (kt_raw * tk + k_l)
            return (rel >= 0) & (rel < WINDOW)

        init = tuple((jnp.full((tq, 1), NEG_INF, F32), jnp.zeros((tq, LANES), F32)) for _ in range(NSA_GROUP))
        sel_trips = (i + TILES_PER_TRIP) // TILES_PER_TRIP
        sel = lax.fori_loop(0, sel_trips,
                            lambda j, c: tile_update(j, c, ks_ref, vs_ref, sel_valid, 0, TILES_PER_TRIP), init)
        win = tile_update(0, init, kw_ref, vw_ref, win_valid, jnp.maximum(i - n_win, 0), n_win + 1)
        hs = slice(h * HEAD_DIM, (h + 1) * HEAD_DIM)
        den = (1 - h) * HEAD_DIM
        for g in range(NSA_GROUP):
            hd = h * NSA_GROUP + g
            c = SM_GATES + hd * 3
            o_c = oc_ref[0, :, hd * HEAD_DIM:(hd + 1) * HEAD_DIM]
            o_s = (sel[g][1] / jnp.maximum(sel[g][1][:, den:den + 1], 1e-30))[:, hs]
            o_w = (win[g][1] / jnp.maximum(win[g][1][:, den:den + 1], 1e-30))[:, hs]
            pieces.append(gate[:, c:c + 1] * o_c + gate[:, c + 1:c + 2] * o_s + gate[:, c + 2:c + 3] * o_w)
    o_ref[0] = jnp.concatenate(pieces, axis=1).astype(o_ref.dtype)


def _nsa_sw(p16, p32, mask, expand, toep, o_c):
    bsz, s, _ = p16.shape
    tq = ATT_TILE
    nslc = mask.shape[3]
    kvb = NSA_Q_W // NSA_KV_W
    return pl.pallas_call(
        _nsa_sw_kernel,
        grid=(bsz, s // tq),
        in_specs=[pl.BlockSpec((1, tq, NSA_Q_W), lambda b, i: (b, i, 0)),
                  pl.BlockSpec((1, s, NSA_KV_W), lambda b, i: (b, 0, kvb)),
                  pl.BlockSpec((1, s, NSA_KV_W), lambda b, i: (b, 0, kvb + 1)),
                  pl.BlockSpec((1, s, NSA_KV_W), lambda b, i: (b, 0, kvb + 2)),
                  pl.BlockSpec((1, s, NSA_KV_W), lambda b, i: (b, 0, kvb + 3)),
                  pl.BlockSpec((1, NSA_KV_HEADS, tq, nslc), lambda b, i: (b, 0, i, 0)),
                  pl.BlockSpec(expand.shape, lambda b, i: (0, 0, 0)),
                  pl.BlockSpec(toep.shape, lambda b, i: (0, 0, 0, 0)),
                  pl.BlockSpec((1, tq, NSA_Q_W), lambda b, i: (b, i, 0)),
                  pl.BlockSpec((1, tq, LANES), lambda b, i: (b, i, C_SMALL // LANES))],
        out_specs=pl.BlockSpec((1, tq, NSA_Q_W), lambda b, i: (b, i, 0)),
        out_shape=jax.ShapeDtypeStruct((bsz, s, NSA_Q_W), BF16),
        compiler_params=_cparams("arbitrary", "arbitrary"),
        name="nsa_sw",
    )(p16, p16, p16, p16, p16, mask, expand, toep, o_c, p32)


def _deltanet_kernel(q_ref, k_ref, v_ref, z_ref, sm_ref, cw_ref, alog_ref, dt_ref, ng_ref, o_ref, xp_ref, st_ref):
    c = q_ref.shape[1]
    pad = 8
    ch = pl.program_id(1)

    @pl.when(ch == 0)
    def _():
        xp_ref[:, 0:pad, :] = jnp.zeros((3, pad, DN_W), F32)
        st_ref[...] = jnp.zeros(st_ref.shape, F32)

    conv = []
    for idx, r in enumerate((q_ref, k_ref, v_ref)):
        xp_ref[idx, pad:pad + c, :] = r[0]
        acc = None
        for j in range(DN_CONV):
            off = pad - (DN_CONV - 1) + j
            term = cw_ref[j:j + 1, idx * DN_W:(idx + 1) * DN_W] * xp_ref[idx, off:off + c, :]
            acc = term if acc is None else acc + term
        xp_ref[idx, 0:pad, :] = xp_ref[idx, c:c + pad, :]
        conv.append(_silu(acc))
    qa, ka, va = conv
    za = _silu(z_ref[0])

    sm = sm_ref[0]
    beta_all = _sigmoid(sm)
    g_all = -jnp.exp(alog_ref[...]) * _softplus(sm + dt_ref[...])
    row = lax.broadcasted_iota(jnp.int32, (c, c), 0)
    col = lax.broadcasted_iota(jnp.int32, (c, c), 1)
    tril = row >= col
    strict = row > col
    eye = row == col
    tril01 = jnp.where(tril, 1.0, 0.0).astype(BF16)
    ones01 = jnp.ones((c, c), BF16)
    gc_all = _dot_01(tril01, g_all)
    eg_all = jnp.exp(gc_all)
    eye_f = jnp.where(eye, 1.0, 0.0)
    ng = ng_ref[...]

    hh = range(DN_HEADS)
    hsl = [slice(h * DN_KDIM, (h + 1) * DN_KDIM) for h in hh]
    states = [st_ref[h] for h in hh]
    q = [qa[:, hsl[h]] for h in hh]
    k = [ka[:, hsl[h]] for h in hh]
    q = [q[h] * lax.rsqrt(jnp.sum(q[h] * q[h], axis=-1, keepdims=True) + EPS) * (DN_KDIM ** -0.5) for h in hh]
    k = [k[h] * lax.rsqrt(jnp.sum(k[h] * k[h], axis=-1, keepdims=True) + EPS) for h in hh]
    beta = [beta_all[:, SM_B + h:SM_B + h + 1] for h in hh]
    gcol = [gc_all[:, SM_A + h:SM_A + h + 1] for h in hh]
    egcol = [eg_all[:, SM_A + h:SM_A + h + 1] for h in hh]
    grow = _dot_01_many(ones01, [jnp.where(eye, gcol[h], 0.0) for h in hh])
    decay = [jnp.where(tril, jnp.exp(jnp.where(tril, gcol[h] - grow[h], 0.0)), 0.0) for h in hh]
    glast = [grow[h][:, c - 1:c] for h in hh]
    kb = [k[h] * beta[h] for h in hh]
    vb = [va[:, hsl[h]] * beta[h] for h in hh]
    kbf = [k[h].astype(BF16) for h in hh]
    kk = [_dot_nt(kb[h].astype(BF16), kbf[h]) for h in hh]
    a = [jnp.where(strict, kk[h] * decay[h], 0.0) for h in hh]
    t_inv = [eye_f - a[h] for h in hh]
    pw = a
    for _ in range(int(math.log2(c)) - 1):
        pw = _dot_x3_many(pw, pw)
        tp = _dot_x3_many(t_inv, pw)
        t_inv = [t_inv[h] + tp[h] for h in hh]
    rhs = [jnp.concatenate([vb[h], kb[h] * egcol[h]], axis=1) for h in hh]
    sol = _dot_x3_many(t_inv, rhs)
    qk = [_dot_nt(q[h].astype(BF16), kbf[h]) for h in hh]
    attn = [(qk[h] * decay[h]).astype(BF16) for h in hh]
    q_dec = [(q[h] * egcol[h]).astype(BF16) for h in hh]
    k_dec = [(k[h] * jnp.exp(glast[h] - gcol[h])).astype(BF16) for h in hh]
    sb = [states[h].astype(BF16) for h in hh]
    ws = [_dot(sol[h][:, DN_VDIM:].astype(BF16), sb[h]) for h in hh]
    v_new = [(sol[h][:, :DN_VDIM] - ws[h]).astype(BF16) for h in hh]
    o1 = [_dot(q_dec[h], sb[h]) for h in hh]
    o2 = [_dot(attn[h], v_new[h]) for h in hh]
    kv = [_dot_tn(k_dec[h], v_new[h]) for h in hh]
    for h in hh:
        st_ref[h] = states[h] * jnp.exp(glast[h][0:1, :]) + kv[h]
    o = [o1[h] + o2[h] for h in hh]
    outs = [o[h] * lax.rsqrt(jnp.mean(o[h] * o[h], axis=-1, keepdims=True) + EPS) * ng * za[:, hsl[h]] for h in hh]
    o_ref[0] = jnp.concatenate(outs, axis=1).astype(o_ref.dtype)


def _deltanet(p32, conv_w, alog_row, dt_row, norm_g):
    bsz, s, _ = p32.shape
    c = DN_CHUNK
    blk = lambda col: pl.BlockSpec((1, c, DN_W), lambda b, i: (b, i, col // DN_W))
    full2 = lambda a: pl.BlockSpec(a.shape, lambda b, i: (0, 0))
    ng = norm_g.reshape(1, DN_VDIM)
    return pl.pallas_call(
        _deltanet_kernel,
        grid=(bsz, s // c),
        in_specs=[blk(C_DNQ), blk(C_DNK), blk(C_DNV), blk(C_DNZ),
                  pl.BlockSpec((1, c, LANES), lambda b, i: (b, i, C_SMALL // LANES)),
                  full2(conv_w), full2(alog_row), full2(dt_row), full2(ng)],
        out_specs=pl.BlockSpec((1, c, DN_W), lambda b, i: (b, i, 0)),
        out_shape=jax.ShapeDtypeStruct((bsz, s, DN_W), BF16),
        scratch_shapes=[pltpu.VMEM((3, c + 8, DN_W), F32),
                        pltpu.VMEM((DN_HEADS, DN_KDIM, DN_VDIM), F32)],
        compiler_params=_cparams("arbitrary", "arbitrary"),
        name="deltanet",
    )(p32, p32, p32, p32, p32, conv_w, alog_row, dt_row, ng)


def _mix_out_kernel(x_ref, mod_ref, onsa_ref, odn_ref, gn_ref, gd_ref, wn_ref, wd_ref, wo_ref, o_ref):
    merged = (_sigmoid(gn_ref[0]) * _dot(onsa_ref[0], wn_ref[...])
              + _sigmoid(gd_ref[0]) * _dot(odn_ref[0], wd_ref[...]))
    y = _dot(merged.astype(BF16), wo_ref[...])
    o_ref[0] = x_ref[0] + mod_ref[0][2:3] * y


def _mix_out(x, mod, o_nsa, o_dn, p32, wn, wd, wo, tm=256):
    bsz, s, d = x.shape
    full2 = lambda a: pl.BlockSpec(a.shape, lambda b, i: (0, 0))
    return pl.pallas_call(
        _mix_out_kernel,
        grid=(bsz, s // tm),
        in_specs=[pl.BlockSpec((1, tm, d), lambda b, i: (b, i, 0)),
                  pl.BlockSpec((1, 6, d), lambda b, i: (b, 0, 0)),
                  pl.BlockSpec((1, tm, NSA_Q_W), lambda b, i: (b, i, 0)),
                  pl.BlockSpec((1, tm, DN_W), lambda b, i: (b, i, 0)),
                  pl.BlockSpec((1, tm, d), lambda b, i: (b, i, C_GNSA // d)),
                  pl.BlockSpec((1, tm, d), lambda b, i: (b, i, C_GDN // d)),
                  full2(wn), full2(wd), full2(wo)],
        out_specs=pl.BlockSpec((1, tm, d), lambda b, i: (b, i, 0)),
        out_shape=jax.ShapeDtypeStruct((bsz, s, d), F32),
        compiler_params=_cparams("arbitrary", "arbitrary"),
        name="mix_out",
    )(x, mod, o_nsa, o_dn, p32, p32, wn, wd, wo)


def _top_k_rows(s, idx, k, payload=None):
    vals, picks = [], []
    for _ in range(k):
        mx = jnp.max(s, axis=0, keepdims=True)
        am = jnp.min(jnp.where(s == mx, idx, 1e9), axis=0, keepdims=True)
        hit = idx == am
        vals.append(mx)
        if payload is None:
            picks.append(am)
        else:
            picks.append(jnp.sum(jnp.where(hit, payload, 0.0), axis=0, keepdims=True))
        s = jnp.where(hit, -jnp.inf, s)
    return jnp.concatenate(vals, axis=0), jnp.concatenate(picks, axis=0)


def _peer_route_kernel(x_ref, mod_ref, g_ref, wq_ref, k1_ref, k2_ref, h_ref, idx_ref, gate_ref):
    tm = x_ref.shape[1]
    h2 = _modulated_norm(x_ref[0], g_ref[...], mod_ref[0], 3, 4)
    hb = h2.astype(BF16)
    h_ref[0] = hb
    qry = _dot(hb, wq_ref[...])
    kidx = lax.broadcasted_iota(jnp.int32, (PEER_NKEYS, tm), 0).astype(F32)
    kk = PEER_TOPK
    hk = kk // 2
    b_iota = kidx[0:kk]
    cidx = jnp.concatenate([b_iota] + [a * float(kk) + b_iota[0:hk] for a in range(1, hk)]
                           + [(b_iota[hk:kk]) * float(kk)], axis=0)
    for h in range(PEER_HEADS):
        qh = qry[:, h * PEER_DKEY:(h + 1) * PEER_DKEY].astype(BF16)
        s1 = _dot_nt(k1_ref[h], qh[:, :PEER_HALF])
        s2 = _dot_nt(k2_ref[h], qh[:, PEER_HALF:])
        v1, i1 = _top_k_rows(s1, kidx, PEER_TOPK)
        v2, i2 = _top_k_rows(s2, kidx, PEER_TOPK)
        cand_s = jnp.concatenate([v1[0:1] + v2] + [v1[a:a + 1] + v2[0:hk] for a in range(1, hk)]
                                 + [v1[hk:kk] + v2[0:1]], axis=0)
        e1 = i1 * float(PEER_NKEYS)
        cand_i = jnp.concatenate([e1[0:1] + i2] + [e1[a:a + 1] + i2[0:hk] for a in range(1, hk)]
                                 + [e1[hk:kk] + i2[0:1]], axis=0)
        top_s, top_e = _top_k_rows(cand_s, cidx, PEER_TOPK, payload=cand_i)
        e = jnp.exp(top_s - top_s[0:1])
        gate = e / jnp.sum(e, axis=0, keepdims=True)
        idx_ref[h * PEER_TOPK:(h + 1) * PEER_TOPK, :] = (top_e * float(PACKED_ROWS)).astype(jnp.int32)
        gate_ref[h * PEER_TOPK:(h + 1) * PEER_TOPK, :] = gate


def _peer_route(x1, mod, g, wq, k1, k2, tm=128):
    bsz, s, d = x1.shape
    nt = s // tm
    ne = PEER_HEADS * PEER_TOPK
    full = lambda a: pl.BlockSpec(a.shape, lambda b, i: (0,) * a.ndim)
    return pl.pallas_call(
        _peer_route_kernel,
        grid=(bsz, nt),
        in_specs=[pl.BlockSpec((1, tm, d), lambda b, i: (b, i, 0)),
                  pl.BlockSpec((1, 6, d), lambda b, i: (b, 0, 0)),
                  pl.BlockSpec((1, d), lambda b, i: (0, 0)),
                  full(wq), full(k1), full(k2)],
        out_specs=[pl.BlockSpec((1, tm, d), lambda b, i: (b, i, 0)),
                   pl.BlockSpec((ne, tm), lambda b, i: (0, b * nt + i)),
                   pl.BlockSpec((ne, tm), lambda b, i: (0, b * nt + i))],
        out_shape=[jax.ShapeDtypeStruct((bsz, s, d), BF16),
                   jax.ShapeDtypeStruct((ne, bsz * s), jnp.int32),
                   jax.ShapeDtypeStruct((ne, bsz * s), F32)],
        compiler_params=_cparams("arbitrary", "arbitrary"),
        name="peer_route",
    )(x1, mod, g.reshape(1, d), wq, k1, k2)


PEER_TOK = 32
EXPERT_TILE_ROWS = 8
PACKED_ROWS = EXPERT_TILE_ROWS // 2


def _load_table_once(tab_hbm, tab_vmem, sem):
    @pl.when(pl.program_id(0) == 0)
    def _():
        cp = pltpu.make_async_copy(tab_hbm, tab_vmem, sem)
        cp.start()
        cp.wait()


def _pack_table(tab):
    e = tab.shape[0]
    bits = lax.bitcast_convert_type(tab.astype(BF16).reshape(e * PACKED_ROWS, 2, LANES), jnp.uint16)
    bits = bits.astype(jnp.uint32)
    return lax.bitcast_convert_type(bits[:, 0, :] | (bits[:, 1, :] << 16), jnp.int32)


def _stage_experts(idx_ref, tab_vmem, stage, nsel):
    for r in range(nsel):
        row = pl.multiple_of(idx_ref[0, 0, r], PACKED_ROWS)
        stage[r * PACKED_ROWS:(r + 1) * PACKED_ROWS, :] = tab_vmem[pl.ds(row, PACKED_ROWS), :]


def _staged_tiles(stage, t, ncol):
    return pltpu.bitcast(stage[t * (ncol // 2):(t + 1) * (ncol // 2), :], BF16)


def _diag_mask(ncol):
    row = lax.broadcasted_iota(jnp.int32, (EXPERT_TILE_ROWS, ncol), 0)
    col = lax.broadcasted_iota(jnp.int32, (EXPERT_TILE_ROWS, ncol), 1)
    return (col & (EXPERT_TILE_ROWS - 1)) == row


def _peer_act_kernel(idx_ref, tab_hbm, h_ref, gate_ref, grp_ref, w_ref, tab_vmem, stage, sem, *, rows_per_tok):
    _load_table_once(tab_hbm, tab_vmem, sem)
    _stage_experts(idx_ref, tab_vmem, stage, PEER_TOK * rows_per_tok)
    ncol = rows_per_tok * EXPERT_TILE_ROWS
    diag = _diag_mask(ncol)
    sums = []
    for t in range(PEER_TOK):
        m = _staged_tiles(stage, t, ncol)
        rt = _dot_nt(h_ref[t], m)
        sums.append(jnp.sum(jnp.where(diag, rt, 0.0), axis=0, keepdims=True))
    part = jnp.concatenate(sums, axis=0)
    hi, mid, lo = _split3(part)
    grp = grp_ref[...]
    act = _dot(hi, grp) + (_dot(mid, grp) + _dot(lo, grp))
    w_ref[...] = gate_ref[...] * _gelu_tanh(act)


def _peer_out_kernel(idx_ref, tab_hbm, w_ref, exp_ref, x_ref, mod_ref, fg_ref, o_ref, tab_vmem, stage, sem,
                     *, rows_per_tok):
    _load_table_once(tab_hbm, tab_vmem, sem)
    _stage_experts(idx_ref, tab_vmem, stage, PEER_TOK * rows_per_tok)
    ncol = rows_per_tok * EXPERT_TILE_ROWS
    diag = _diag_mask(ncol)
    wh, wl = _split2(w_ref[...])
    ex = exp_ref[...]
    weh = _dot(wh, ex)
    wel = _dot(wl, ex)
    g2 = mod_ref[0, 5]
    fg = fg_ref[...]
    nel = float(EXPERT_TILE_ROWS * LANES)
    for t in range(PEER_TOK):
        lhs = jnp.concatenate([jnp.where(diag, weh[t:t + 1], 0.0), jnp.where(diag, wel[t:t + 1], 0.0)], axis=0)
        o16 = _dot(lhs.astype(BF16), _staged_tiles(stage, t, ncol))
        y = o16[0:EXPERT_TILE_ROWS] + o16[EXPERT_TILE_ROWS:]
        xo = x_ref[t] + g2 * y
        ms = jnp.sum(jnp.sum(xo * xo, axis=1, keepdims=True), axis=0, keepdims=True) / nel
        o_ref[t] = xo * lax.rsqrt(ms + EPS) * fg


def _peer_experts(idx3, tab_u, tab_v, h3, gate, x3, mod4, fg3, s):
    n = h3.shape[0]
    ne = gate.shape[1]
    nstep = n // PEER_TOK
    nsel = PEER_TOK * ne
    ncol = ne * EXPERT_TILE_ROWS
    grp = jnp.asarray(np.arange(ncol)[:, None] // EXPERT_TILE_ROWS == np.arange(ne)[None, :], BF16)
    idx_spec = pl.BlockSpec((1, 1, nsel), lambda t: (t, 0, 0), memory_space=pltpu.SMEM)
    any_spec = pl.BlockSpec(memory_space=pl.ANY)
    tile_spec = pl.BlockSpec((PEER_TOK, EXPERT_TILE_ROWS, LANES), lambda t: (t, 0, 0))
    row_spec = pl.BlockSpec((PEER_TOK, ne), lambda t: (t, 0))
    scratch = [pltpu.VMEM(tab_u.shape, jnp.int32), pltpu.VMEM((PEER_TOK * ncol // 2, LANES), jnp.int32),
               pltpu.SemaphoreType.DMA]
    w = pl.pallas_call(
        functools.partial(_peer_act_kernel, rows_per_tok=ne),
        grid=(nstep,),
        in_specs=[idx_spec, any_spec, tile_spec, row_spec, pl.BlockSpec(grp.shape, lambda t: (0, 0))],
        out_specs=row_spec,
        out_shape=jax.ShapeDtypeStruct((n, ne), F32),
        scratch_shapes=scratch,
        compiler_params=_cparams("arbitrary"),
        name="peer_act",
    )(idx3, tab_u, h3, gate, grp)
    return pl.pallas_call(
        functools.partial(_peer_out_kernel, rows_per_tok=ne),
        grid=(nstep,),
        in_specs=[idx_spec, any_spec, row_spec, pl.BlockSpec((ne, ncol), lambda t: (0, 0)), tile_spec,
                  pl.BlockSpec((1, 6, EXPERT_TILE_ROWS, LANES), lambda t: ((t * PEER_TOK) // s, 0, 0, 0)),
                  pl.BlockSpec((EXPERT_TILE_ROWS, LANES), lambda t: (0, 0))],
        out_specs=tile_spec,
        out_shape=jax.ShapeDtypeStruct((n, EXPERT_TILE_ROWS, LANES), F32),
        scratch_shapes=scratch,
        compiler_params=_cparams("arbitrary"),
        name="peer_out",
    )(idx3, tab_v, w, grp.T, x3, mod4, fg3)


def _rel_bucket(dist):
    dist = jnp.maximum(dist, 0)
    max_exact = REL_BUCKETS // 2
    scaled = jnp.log(jnp.maximum(dist, 1).astype(F32) / max_exact) / math.log(REL_MAX_DIST / max_exact)
    large = jnp.minimum(max_exact + (scaled * (REL_BUCKETS - max_exact)).astype(jnp.int32), REL_BUCKETS - 1)
    return jnp.where(dist < max_exact, dist, large)


def _bias_tables(rel_bias, s, ncmp):
    dist_bias = rel_bias[_rel_bucket(jnp.arange(s))].T
    nq = s // ATT_TILE
    per_tile = ATT_TILE // CMP_STRIDE
    r = np.arange(ATT_TILE)[:, None]
    k0 = per_tile * (nq - 1)
    k = np.arange(k0 + ncmp)[None, :]
    dist_g = r - CMP_STRIDE * (k - k0) - (CMP_LEN - 1)
    g = dist_bias[:, np.clip(dist_g, 0, s - 1)]
    wins = jnp.stack([g[:, :, k0 - per_tile * i:k0 - per_tile * i + ncmp] for i in range(nq)], axis=1)
    bias_c = wins.reshape(rel_bias.shape[1], s, ncmp)
    ii = jnp.arange(ATT_TILE)[:, None]
    jj = jnp.arange(ATT_TILE)[None, :]
    dist = jnp.stack([m * ATT_TILE + ii - jj for m in range(3)])
    toep = dist_bias[:, jnp.clip(dist, 0, s - 1)]
    return bias_c, toep


def _check_bucket_saturation():
    d = np.arange(ATT_TILE + 1, 1 << 16)
    max_exact = REL_BUCKETS // 2
    scaled = np.log(d.astype(np.float32) / max_exact) / math.log(REL_MAX_DIST / max_exact)
    large = np.minimum(max_exact + (scaled * (REL_BUCKETS - max_exact)).astype(np.int32), REL_BUCKETS - 1)
    assert (large == REL_BUCKETS - 1).all()


_check_bucket_saturation()


def _token_mixer(x, mod, rel_bias, norm_g, w_in, cmp_pos_emb, w_ck1, w_ck2, w_cv1, w_cv2, conv_w, a_log, dt_bias,
                 dn_norm_g, w_branch_nsa, w_branch_dn, w_out):
    bsz, s, d = x.shape
    assert s % ATT_TILE == 0 and s % CMP_STRIDE == 0 and WINDOW % ATT_TILE == 0 and d == C_GDN - C_GNSA
    pts = np.cumsum(IN_SPLIT_SIZES + (d, d))[:-1]
    assert w_in.shape[1] == sum(IN_SPLIT_SIZES) + 2 * d
    (w_q, w_kc, w_vc, w_ks, w_vs, w_kw, w_vw, w_gates, w_dq, w_dk, w_dv, w_dz, w_db, w_da, w_gn, w_gd) = (
        jnp.split(w_in, [int(p) for p in pts], axis=1))
    w16 = jnp.concatenate([w_q * (HEAD_DIM ** -0.5), w_ks, w_vs, w_kw, w_vw], axis=1).astype(BF16)
    small_pad = jnp.zeros((d, LANES - NSA_HEADS * 3 - 2 * DN_HEADS), w_in.dtype)
    w32 = jnp.concatenate([w_dq, w_dk, w_dv, w_dz, w_gn, w_gd, w_kc, w_vc, w_gates, w_db, w_da, small_pad],
                          axis=1).astype(BF16)
    assert w16.shape[1] == C16 and w32.shape[1] == C32
    p16, p32 = _proj_in(x, mod, norm_g, w16, w32)

    nrow = s // CMP_STRIDE
    k2 = p32[:, :, C_KC:C_KC + NSA_KV_W].reshape(bsz, nrow, CMP_STRIDE * NSA_KV_W)
    v2 = p32[:, :, C_VC:C_VC + NSA_KV_W].reshape(bsz, nrow, CMP_STRIDE * NSA_KV_W)
    kc, vc = _compress(k2, v2, cmp_pos_emb, w_ck1, w_ck2, w_cv1, w_cv2)
    nslc = s // SLC_BLOCK
    top_n = min(SLC_TOPN, nslc)
    bias_c, toep = _bias_tables(rel_bias, s, nrow)
    cmp_start = np.arange(nrow) * CMP_STRIDE
    slc_start = np.arange(nslc) * SLC_BLOCK
    ovt = ((cmp_start[None, :] <= slc_start[:, None] + SLC_BLOCK - 1)
           & (cmp_start[None, :] + CMP_LEN - 1 >= slc_start[:, None]))
    ovt = jnp.asarray(ovt, BF16)
    o_c, mask = _nsa_cmp(p16, kc, vc, bias_c, ovt, top_n)
    nkt = s // ATT_TILE
    key_blk = (np.arange(nkt)[:, None, None] * ATT_TILE + np.arange(ATT_TILE)[None, None, :]) // SLC_BLOCK
    expand = jnp.asarray(key_blk == np.arange(nslc)[None, :, None], BF16)
    o_nsa = _nsa_sw(p16, p32, mask, expand, toep, o_c)

    lane_row = lambda vec: jnp.zeros((1, LANES), F32).at[0, SM_A:SM_A + DN_HEADS].set(vec.astype(F32))
    o_dn = _deltanet(p32, conv_w, lane_row(a_log), lane_row(dt_bias), dn_norm_g)

    return _mix_out(x, mod, o_nsa, o_dn, p32, w_branch_nsa.astype(BF16), w_branch_dn.astype(BF16),
                    w_out.astype(BF16))


def _peer_block(x1, mod, norm_g, final_g, w_query, keys1, keys2, expert_u, expert_v):
    bsz, s, d = x1.shape
    n = bsz * s
    ne = PEER_HEADS * PEER_TOPK
    h2, idx_t, gate_t = _peer_route(x1, mod, norm_g, w_query.astype(BF16), keys1.astype(BF16), keys2.astype(BF16))
    assert d == EXPERT_TILE_ROWS * LANES
    nstep = n // PEER_TOK
    idx3 = idx_t.T.reshape(nstep, 1, PEER_TOK * ne)
    tile = (EXPERT_TILE_ROWS, LANES)
    tab_u = _pack_table(expert_u)
    tab_v = _pack_table(expert_v)
    out = _peer_experts(idx3, tab_u, tab_v, h2.reshape((n,) + tile), gate_t.T, x1.reshape((n,) + tile),
                        mod.reshape((bsz, 6) + tile), final_g.reshape(tile), s)
    return out.reshape(bsz, s, d)


def kernel(x, c, rel_bias, final_g, w_ada, b_ada, norm1_g, w_in, cmp_pos_emb, w_cmp_k1, w_cmp_k2, w_cmp_v1,
           w_cmp_v2, dn_conv_w, dn_A_log, dn_dt_bias, dn_norm_g, w_branch_nsa, w_branch_dn, w_out, norm2_g,
           peer_w_query, peer_keys1, peer_keys2, peer_u, peer_v):
    depth = w_ada.shape[0]
    assert depth == 1, "the final norm is fused into the last PEER kernel; one layer is supported"
    bsz, s, d = x.shape
    mod = _ada(c, w_ada[0], b_ada[0]).reshape(bsz, 6, d)
    x1 = _token_mixer(x, mod, rel_bias, norm1_g[0], w_in[0], cmp_pos_emb[0], w_cmp_k1[0], w_cmp_k2[0], w_cmp_v1[0],
                      w_cmp_v2[0], dn_conv_w[0], dn_A_log[0], dn_dt_bias[0], dn_norm_g[0], w_branch_nsa[0],
                      w_branch_dn[0], w_out[0])
    return _peer_block(x1, mod, norm2_g[0], final_g, peer_w_query[0], peer_keys1[0], peer_keys2[0], peer_u[0],
                       peer_v[0])
```

```python
import functools
import math

import numpy as np
import jax
import jax.numpy as jnp
from jax import lax
from jax.experimental import pallas as pl
from jax.experimental.pallas import tpu as pltpu

F32 = jnp.float32
BF16 = jnp.bfloat16

NSA_HEADS = 8
NSA_KV_HEADS = 2
NSA_GROUP = NSA_HEADS // NSA_KV_HEADS
HEAD_DIM = 64
CMP_LEN = 32
CMP_STRIDE = 16
CMP_HIDDEN = 64
SLC_BLOCK = 64
SLC_TOPN = 16
WINDOW = 512
FORCED_BONUS = 1.0e4
DN_HEADS = 8
DN_KDIM = 64
DN_VDIM = 64
DN_CONV = 4
DN_CHUNK = 64
PEER_HEADS = 8
PEER_NKEYS = 128
PEER_DKEY = 128
PEER_HALF = PEER_DKEY // 2
PEER_TOPK = 16
REL_BUCKETS = 32
REL_MAX_DIST = 128
EPS = 1e-6
NEG_INF = -1e30

NSA_Q_W = NSA_HEADS * HEAD_DIM
NSA_KV_W = NSA_KV_HEADS * HEAD_DIM
DN_W = DN_HEADS * DN_KDIM
IN_SPLIT_SIZES = (NSA_Q_W, NSA_KV_W, NSA_KV_W, NSA_KV_W, NSA_KV_W, NSA_KV_W, NSA_KV_W, NSA_HEADS * 3,
                  DN_W, DN_W, DN_W, DN_W, DN_HEADS, DN_HEADS)

LANES = 128
ATT_TILE = 128
TILES_PER_TRIP = 4
VMEM_LIMIT = 56 * 1024 * 1024

C_DNQ, C_DNK, C_DNV, C_DNZ = 0, 512, 1024, 1536
C_GNSA, C_GDN = 2048, 3072
C_KC, C_VC, C_SMALL = 4096, 4224, 4352
C32 = 4480
SM_GATES, SM_B, SM_A = 0, 24, 32
C16 = 1024


def _gelu_tanh(x):
    return 0.5 * x * (1.0 + jnp.tanh(math.sqrt(2.0 / math.pi) * (x + 0.044715 * (x * x * x))))


def _sigmoid(x):
    return 1.0 / (1.0 + jnp.exp(-x))


def _silu(x):
    return x * _sigmoid(x)


def _softplus(x):
    return jnp.maximum(x, 0.0) + jnp.log1p(jnp.exp(-jnp.abs(x)))


def _dot(a, b):
    return jnp.dot(a, b, preferred_element_type=F32)


def _dot_nt(a, b):
    return lax.dot_general(a, b, (((1,), (1,)), ((), ())), preferred_element_type=F32)


def _dot_tn(a, b):
    return lax.dot_general(a, b, (((0,), (0,)), ((), ())), preferred_element_type=F32)


def _split2(x):
    hi = x.astype(BF16)
    lo = (x - hi.astype(F32)).astype(BF16)
    return hi, lo


def _split3(x):
    hi = x.astype(BF16)
    r = x - hi.astype(F32)
    mid = r.astype(BF16)
    lo = (r - mid.astype(F32)).astype(BF16)
    return hi, mid, lo


def _dot_x3(a, b):
    ah, al = _split2(a)
    bh, bl = _split2(b)
    return _dot(ah, bh) + (_dot(ah, bl) + _dot(al, bh))


def _dot_01(m01, x):
    hi, mid, lo = _split3(x)
    return _dot(m01, hi) + (_dot(m01, mid) + _dot(m01, lo))


def _dot_x3_many(a_list, b_list):
    n = range(len(a_list))
    asp = [_split2(a) for a in a_list]
    bsp = [_split2(b) for b in b_list]
    hh = [_dot(asp[i][0], bsp[i][0]) for i in n]
    hl = [_dot(asp[i][0], bsp[i][1]) for i in n]
    lh = [_dot(asp[i][1], bsp[i][0]) for i in n]
    return [hh[i] + (hl[i] + lh[i]) for i in n]


def _dot_01_many(m01, x_list):
    n = range(len(x_list))
    sp = [_split3(x) for x in x_list]
    d0 = [_dot(m01, sp[i][0]) for i in n]
    d1 = [_dot(m01, sp[i][1]) for i in n]
    d2 = [_dot(m01, sp[i][2]) for i in n]
    return [d0[i] + (d1[i] + d2[i]) for i in n]


def _cparams(*sem):
    return pltpu.CompilerParams(dimension_semantics=sem, vmem_limit_bytes=VMEM_LIMIT)


def _ada_kernel(c_ref, w_ref, b_ref, o_ref):
    cond = _silu(c_ref[...])
    o_ref[...] = _dot(cond.astype(BF16), w_ref[...].astype(BF16)) + b_ref[...]


def _ada(c, w_ada, b_ada):
    bsz, d = c.shape
    n_out = w_ada.shape[1]
    blk = d
    return pl.pallas_call(
        _ada_kernel,
        grid=(n_out // blk,),
        in_specs=[pl.BlockSpec((bsz, d), lambda j: (0, 0)),
                  pl.BlockSpec((d, blk), lambda j: (0, j)),
                  pl.BlockSpec((1, blk), lambda j: (0, j))],
        out_specs=pl.BlockSpec((bsz, blk), lambda j: (0, j)),
        out_shape=jax.ShapeDtypeStruct((bsz, n_out), F32),
        compiler_params=_cparams("arbitrary"),
        name="ada",
    )(c, w_ada, b_ada.reshape(1, n_out))


def _modulated_norm(x, g, mod, shift_row, scale_row):
    var = jnp.mean(x * x, axis=-1, keepdims=True)
    y = x * lax.rsqrt(var + EPS) * g
    return y * (1.0 + mod[scale_row:scale_row + 1]) + mod[shift_row:shift_row + 1]


def _proj_in_kernel(x_ref, mod_ref, g_ref, w16_ref, w32_ref, o16_ref, o32_ref):
    h = _modulated_norm(x_ref[0], g_ref[...], mod_ref[0], 0, 1).astype(BF16)
    step = 512
    for c0 in range(0, o16_ref.shape[2], step):
        c1 = min(c0 + step, o16_ref.shape[2])
        o16_ref[0, :, c0:c1] = _dot(h, w16_ref[:, c0:c1]).astype(BF16)
    for c0 in range(0, o32_ref.shape[2], step):
        c1 = min(c0 + step, o32_ref.shape[2])
        o32_ref[0, :, c0:c1] = _dot(h, w32_ref[:, c0:c1])


def _proj_in(x, mod, g, w16, w32, tm=256):
    bsz, s, d = x.shape
    return pl.pallas_call(
        _proj_in_kernel,
        grid=(bsz, s // tm),
        in_specs=[pl.BlockSpec((1, tm, d), lambda b, i: (b, i, 0)),
                  pl.BlockSpec((1, 6, d), lambda b, i: (b, 0, 0)),
                  pl.BlockSpec((1, d), lambda b, i: (0, 0)),
                  pl.BlockSpec(w16.shape, lambda b, i: (0, 0)),
                  pl.BlockSpec(w32.shape, lambda b, i: (0, 0))],
        out_specs=[pl.BlockSpec((1, tm, w16.shape[1]), lambda b, i: (b, i, 0)),
                   pl.BlockSpec((1, tm, w32.shape[1]), lambda b, i: (b, i, 0))],
        out_shape=[jax.ShapeDtypeStruct((bsz, s, w16.shape[1]), BF16),
                   jax.ShapeDtypeStruct((bsz, s, w32.shape[1]), F32)],
        compiler_params=_cparams("arbitrary", "arbitrary"),
        name="proj_in",
    )(x, mod, g.reshape(1, d), w16, w32)


def _compress_kernel(k2_ref, v2_ref, posa_ref, posb_ref, ak_ref, bk_ref, w2k_ref, av_ref, bv_ref, w2v_ref,
                     kc_ref, vc_ref):
    nrow = k2_ref.shape[1]

    def one(x_ref, a_ref, b_ref, w2_ref, o_ref):
        x = x_ref[0]
        ya = _dot((x + posa_ref[...]).astype(BF16), a_ref[...])
        yb = _dot((x + posb_ref[...]).astype(BF16), b_ref[...])
        hid = _gelu_tanh(ya + pltpu.roll(yb, nrow - 1, 0))
        out = _dot(hid.astype(BF16), w2_ref[...])
        for h in range(NSA_KV_HEADS):
            o_ref[0, h] = out[:, h * HEAD_DIM:(h + 1) * HEAD_DIM]

    one(k2_ref, ak_ref, bk_ref, w2k_ref, kc_ref)
    one(v2_ref, av_ref, bv_ref, w2v_ref, vc_ref)


def _expand_cmp_weights(w1, w2):
    half = CMP_LEN // 2
    w1r = w1.reshape(2, half, HEAD_DIM, CMP_HIDDEN)
    eye = jnp.eye(NSA_KV_HEADS, dtype=w1.dtype)
    exp = jnp.einsum('pldj,hg->plhdgj', w1r, eye).reshape(2, half * NSA_KV_HEADS * HEAD_DIM,
                                                          NSA_KV_HEADS * CMP_HIDDEN)
    w2e = jnp.einsum('jd,hg->hjgd', w2, eye).reshape(NSA_KV_HEADS * CMP_HIDDEN, NSA_KV_HEADS * HEAD_DIM)
    return exp[0].astype(BF16), exp[1].astype(BF16), w2e.astype(BF16)


def _compress(k2, v2, pos_emb, w_ck1, w_ck2, w_cv1, w_cv2):
    bsz, nrow, width = k2.shape
    half = CMP_LEN // 2
    pos = jnp.broadcast_to(pos_emb.reshape(2, half, 1, HEAD_DIM), (2, half, NSA_KV_HEADS, HEAD_DIM))
    pos = pos.reshape(2, 1, width)
    ak, bk, w2k = _expand_cmp_weights(w_ck1, w_ck2)
    av, bv, w2v = _expand_cmp_weights(w_cv1, w_cv2)
    full2 = lambda a: pl.BlockSpec(a.shape, lambda b: (0, 0))
    out_sds = jax.ShapeDtypeStruct((bsz, NSA_KV_HEADS, nrow, HEAD_DIM), F32)
    out_spec = pl.BlockSpec((1, NSA_KV_HEADS, nrow, HEAD_DIM), lambda b: (b, 0, 0, 0))
    return pl.pallas_call(
        _compress_kernel,
        grid=(bsz,),
        in_specs=[pl.BlockSpec((1, nrow, width), lambda b: (b, 0, 0)),
                  pl.BlockSpec((1, nrow, width), lambda b: (b, 0, 0)),
                  full2(pos[0]), full2(pos[1]), full2(ak), full2(bk), full2(w2k), full2(av), full2(bv), full2(w2v)],
        out_specs=[out_spec, out_spec],
        out_shape=[out_sds, out_sds],
        compiler_params=_cparams("arbitrary"),
        name="compress",
    )(k2, v2, pos[0], pos[1], ak, bk, w2k, av, bv, w2v)


def _nsa_cmp_kernel(q_ref, kc_ref, vc_ref, bias_ref, ovt_ref, oc_ref, mask_ref, *, top_n):
    tq = q_ref.shape[1]
    ncmp = kc_ref.shape[2]
    nslc = ovt_ref.shape[0]
    s0 = pl.program_id(2) * tq
    kcb = kc_ref[0, 0].astype(BF16)
    vcb = vc_ref[0, 0].astype(BF16)
    t_q = s0 + lax.broadcasted_iota(jnp.int32, (tq, ncmp), 0)
    n_i = lax.broadcasted_iota(jnp.int32, (tq, ncmp), 1)
    valid = (n_i * CMP_STRIDE + (CMP_LEN - 1)) <= t_q
    ovt = ovt_ref[...]
    gs = range(NSA_GROUP)
    sc = [_dot_nt(q_ref[0, :, g * HEAD_DIM:(g + 1) * HEAD_DIM], kcb) for g in gs]
    l = [jnp.where(valid, sc[g] + bias_ref[g], NEG_INF) for g in gs]
    m = [jnp.max(l[g], axis=-1, keepdims=True) for g in gs]
    e = [jnp.where(valid, jnp.exp(l[g] - m[g]), 0.0) for g in gs]
    den = [jnp.maximum(jnp.sum(e[g], axis=-1, keepdims=True), 1e-30) for g in gs]
    pb = [(e[g] / den[g]).astype(BF16) for g in gs]
    outs = [_dot(pb[g], vcb) for g in gs]
    imps = [_dot_nt(ovt, pb[g]) for g in gs]
    imp_t = functools.reduce(lambda acc, x: acc + x, imps)
    oc_ref[0] = jnp.concatenate(outs, axis=1)

    t_l = s0 + lax.broadcasted_iota(jnp.int32, (nslc, tq), 1)
    j_s = lax.broadcasted_iota(jnp.int32, (nslc, tq), 0)
    cur = lax.shift_right_logical(t_l, int(math.log2(SLC_BLOCK)))
    forced = (j_s == 0) | (j_s == cur) | (j_s == cur - 1)
    future = (j_s * SLC_BLOCK) > t_l
    score = jnp.where(future, NEG_INF, imp_t + jnp.where(forced, FORCED_BONUS, 0.0))
    rank = jnp.zeros((nslc, tq), F32)
    for i in range(nslc):
        r = score[i:i + 1, :]
        beats = (r > score) | ((r == score) & (j_s > i))
        rank = rank + jnp.where(beats, 1.0, 0.0)
    sel_t = jnp.where(rank < float(top_n), 1.0, 0.0).astype(BF16)
    eye = (lax.broadcasted_iota(jnp.int32, (tq, tq), 0) == lax.broadcasted_iota(jnp.int32, (tq, tq), 1))
    mask_ref[0, 0] = _dot_nt(jnp.where(eye, 1.0, 0.0).astype(BF16), sel_t).astype(BF16)


def _nsa_cmp(p16, kc, vc, bias_c, ovt, top_n):
    bsz, s, _ = p16.shape
    tq = ATT_TILE
    ncmp = kc.shape[2]
    nslc = ovt.shape[0]
    qw = NSA_GROUP * HEAD_DIM
    return pl.pallas_call(
        functools.partial(_nsa_cmp_kernel, top_n=top_n),
        grid=(bsz, NSA_KV_HEADS, s // tq),
        in_specs=[pl.BlockSpec((1, tq, qw), lambda b, h, i: (b, i, h)),
                  pl.BlockSpec((1, 1, ncmp, HEAD_DIM), lambda b, h, i: (b, h, 0, 0)),
                  pl.BlockSpec((1, 1, ncmp, HEAD_DIM), lambda b, h, i: (b, h, 0, 0)),
                  pl.BlockSpec((NSA_GROUP, tq, ncmp), lambda b, h, i: (h, i, 0)),
                  pl.BlockSpec(ovt.shape, lambda b, h, i: (0, 0))],
        out_specs=[pl.BlockSpec((1, tq, qw), lambda b, h, i: (b, i, h)),
                   pl.BlockSpec((1, 1, tq, nslc), lambda b, h, i: (b, h, i, 0))],
        out_shape=[jax.ShapeDtypeStruct((bsz, s, NSA_Q_W), F32),
                   jax.ShapeDtypeStruct((bsz, NSA_KV_HEADS, s, nslc), BF16)],
        compiler_params=_cparams("arbitrary", "arbitrary", "arbitrary"),
        name="nsa_cmp",
    )(p16, kc, vc, bias_c, ovt)


def _nsa_sw_kernel(q_ref, ks_ref, vs_ref, kw_ref, vw_ref, mask_ref, exp_ref, toep_ref, oc_ref, sm_ref, o_ref):
    tq = q_ref.shape[1]
    tk = ATT_TILE
    i = pl.program_id(1)
    s0 = i * tq
    gate = _sigmoid(sm_ref[0])
    t_q = s0 + lax.broadcasted_iota(jnp.int32, (tq, tk), 0)
    k_l = lax.broadcasted_iota(jnp.int32, (tq, tk), 1)
    n_win = WINDOW // tk
    pieces = []
    lane = lax.broadcasted_iota(jnp.int32, (tq, LANES), 1)
    assert NSA_KV_HEADS * HEAD_DIM == LANES and tq == tk
    for h in range(NSA_KV_HEADS):
        in_h = (lane >= h * HEAD_DIM) & (lane < (h + 1) * HEAD_DIM)
        qs = []
        for g in range(NSA_GROUP):
            hd = h * NSA_GROUP + g
            pair = q_ref[0, :, (hd // 2) * LANES:(hd // 2 + 1) * LANES].astype(F32)
            if hd % 2 != h:
                pair = pltpu.roll(pair, HEAD_DIM, 1)
            qs.append(jnp.where(in_h, pair, 0.0).astype(BF16))
        mq = mask_ref[0, h]

        def tile_update(j, carry, k_ref, v_ref, valid_fn, lo, per):
            us = range(per)
            gs = range(NSA_GROUP)
            kt_raw = [lo + per * j + u for u in us]
            kt = [jnp.minimum(kt_raw[u], i) for u in us]
            k = [k_ref[0, pl.ds(pl.multiple_of(kt[u] * tk, tk), tk), :] for u in us]
            v = [jnp.where(in_h, v_ref[0, pl.ds(pl.multiple_of(kt[u] * tk, tk), tk), :], 1.0) for u in us]
            mm = [jnp.minimum(i - kt[u], 2) for u in us]
            valid = [valid_fn(kt[u], kt_raw[u]) for u in us]
            sc = [[_dot_nt(qs[g], k[u]) for u in us] for g in gs]
            lg = [[jnp.where(valid[u], sc[g][u] + toep_ref[h * NSA_GROUP + g, mm[u]], NEG_INF) for u in us]
                  for g in gs]
            mx = [functools.reduce(jnp.maximum, [jnp.max(lg[g][u], axis=-1, keepdims=True) for u in us])
                  for g in gs]
            m_new = [jnp.maximum(carry[g][0], mx[g]) for g in gs]
            alpha = [jnp.exp(carry[g][0] - m_new[g]) for g in gs]
            e = [[jnp.where(valid[u], jnp.exp(lg[g][u] - m_new[g]), 0.0).astype(BF16) for u in us] for g in gs]
            pv = [functools.reduce(lambda a, b: a + b, [_dot(e[g][u], v[u]) for u in us]) for g in gs]
            return tuple((m_new[g], alpha[g] * carry[g][1] + pv[g]) for g in gs)

        def sel_valid(kt, kt_raw):
            mk = _dot(mq, exp_ref[kt])
            kpos = kt_raw * tk + k_l
            return (mk > 0.5) & (kpos <= t_q)

        def win_valid(kt, kt_raw):
            rel = t_q - (kt_raw * tk + k_l)
            return (rel >= 0) & (rel < WINDOW)

        init = tuple((jnp.full((tq, 1), NEG_INF, F32), jnp.zeros((tq, LANES), F32)) for _ in range(NSA_GROUP))
        sel_trips = (i + TILES_PER_TRIP) // TILES_PER_TRIP
        sel = lax.fori_loop(0, sel_trips,
                            lambda j, c: tile_update(j, c, ks_ref, vs_ref, sel_valid, 0, TILES_PER_TRIP), init)
        win = tile_update(0, init, kw_ref, vw_ref, win_valid, jnp.maximum(i - n_win, 0), n_win + 1)
        hs = slice(h * HEAD_DIM, (h + 1) * HEAD_DIM)
        den = (1 - h) * HEAD_DIM
        for g in range(NSA_GROUP):
            hd = h * NSA_GROUP + g
            c = SM_GATES + hd * 3
            o_c = oc_ref[0, :, hd * HEAD_DIM:(hd + 1) * HEAD_DIM]
            o_s = (sel[g][1] / jnp.maximum(sel[g][1][:, den:den + 1], 1e-30))[:, hs]
            o_w = (win[g][1] / jnp.maximum(win[g][1][:, den:den + 1], 1e-30))[:, hs]
            pieces.append(gate[:, c:c + 1] * o_c + gate[:, c + 1:c + 2] * o_s + gate[:, c + 2:c + 3] * o_w)
    o_ref[0] = jnp.concatenate(pieces, axis=1).astype(o_ref.dtype)


def _nsa_sw(p16, p32, mask, expand, toep, o_c):
    bsz, s, _ = p16.shape
    tq = ATT_TILE
    nslc = mask.shape[3]
    kvb = NSA_Q_W // NSA_KV_W
    return pl.pallas_call(
        _nsa_sw_kernel,
        grid=(bsz, s // tq),
        in_specs=[pl.BlockSpec((1, tq, NSA_Q_W), lambda b, i: (b, i, 0)),
                  pl.BlockSpec((1, s, NSA_KV_W), lambda b, i: (b, 0, kvb)),
                  pl.BlockSpec((1, s, NSA_KV_W), lambda b, i: (b, 0, kvb + 1)),
                  pl.BlockSpec((1, s, NSA_KV_W), lambda b, i: (b, 0, kvb + 2)),
                  pl.BlockSpec((1, s, NSA_KV_W), lambda b, i: (b, 0, kvb + 3)),
                  pl.BlockSpec((1, NSA_KV_HEADS, tq, nslc), lambda b, i: (b, 0, i, 0)),
                  pl.BlockSpec(expand.shape, lambda b, i: (0, 0, 0)),
                  pl.BlockSpec(toep.shape, lambda b, i: (0, 0, 0, 0)),
                  pl.BlockSpec((1, tq, NSA_Q_W), lambda b, i: (b, i, 0)),
                  pl.BlockSpec((1, tq, LANES), lambda b, i: (b, i, C_SMALL // LANES))],
        out_specs=pl.BlockSpec((1, tq, NSA_Q_W), lambda b, i: (b, i, 0)),
        out_shape=jax.ShapeDtypeStruct((bsz, s, NSA_Q_W), BF16),
        compiler_params=_cparams("arbitrary", "arbitrary"),
        name="nsa_sw",
    )(p16, p16, p16, p16, p16, mask, expand, toep, o_c, p32)


def _deltanet_kernel(q_ref, k_ref, v_ref, z_ref, sm_ref, cw_ref, alog_ref, dt_ref, ng_ref, o_ref, xp_ref, st_ref):
    c = q_ref.shape[1]
    pad = 8
    ch = pl.program_id(1)

    @pl.when(ch == 0)
    def _():
        xp_ref[:, 0:pad, :] = jnp.zeros((3, pad, DN_W), F32)
        st_ref[...] = jnp.zeros(st_ref.shape, F32)

    conv = []
    for idx, r in enumerate((q_ref, k_ref, v_ref)):
        xp_ref[idx, pad:pad + c, :] = r[0]
        acc = None
        for j in range(DN_CONV):
            off = pad - (DN_CONV - 1) + j
            term = cw_ref[j:j + 1, idx * DN_W:(idx + 1) * DN_W] * xp_ref[idx, off:off + c, :]
            acc = term if acc is None else acc + term
        xp_ref[idx, 0:pad, :] = xp_ref[idx, c:c + pad, :]
        conv.append(_silu(acc))
    qa, ka, va = conv
    za = _silu(z_ref[0])

    sm = sm_ref[0]
    beta_all = _sigmoid(sm)
    g_all = -jnp.exp(alog_ref[...]) * _softplus(sm + dt_ref[...])
    row = lax.broadcasted_iota(jnp.int32, (c, c), 0)
    col = lax.broadcasted_iota(jnp.int32, (c, c), 1)
    tril = row >= col
    strict = row > col
    eye = row == col
    tril01 = jnp.where(tril, 1.0, 0.0).astype(BF16)
    ones01 = jnp.ones((c, c), BF16)
    gc_all = _dot_01(tril01, g_all)
    eg_all = jnp.exp(gc_all)
    eye_f = jnp.where(eye, 1.0, 0.0)
    ng = ng_ref[...]

    hh = range(DN_HEADS)
    hsl = [slice(h * DN_KDIM, (h + 1) * DN_KDIM) for h in hh]
    states = [st_ref[h] for h in hh]
    q = [qa[:, hsl[h]] for h in hh]
    k = [ka[:, hsl[h]] for h in hh]
    q = [q[h] * lax.rsqrt(jnp.sum(q[h] * q[h], axis=-1, keepdims=True) + EPS) * (DN_KDIM ** -0.5) for h in hh]
    k = [k[h] * lax.rsqrt(jnp.sum(k[h] * k[h], axis=-1, keepdims=True) + EPS) for h in hh]
    beta = [beta_all[:, SM_B + h:SM_B + h + 1] for h in hh]
    gcol = [gc_all[:, SM_A + h:SM_A + h + 1] for h in hh]
    egcol = [eg_all[:, SM_A + h:SM_A + h + 1] for h in hh]
    grow = _dot_01_many(ones01, [jnp.where(eye, gcol[h], 0.0) for h in hh])
    decay = [jnp.where(tril, jnp.exp(jnp.where(tril, gcol[h] - grow[h], 0.0)), 0.0) for h in hh]
    glast = [grow[h][:, c - 1:c] for h in hh]
    kb = [k[h] * beta[h] for h in hh]
    vb = [va[:, hsl[h]] * beta[h] for h in hh]
    kbf = [k[h].astype(BF16) for h in hh]
    kk = [_dot_nt(kb[h].astype(BF16), kbf[h]) for h in hh]
    a = [jnp.where(strict, kk[h] * decay[h], 0.0) for h in hh]
    t_inv = [eye_f - a[h] for h in hh]
    pw = a
    for _ in range(int(math.log2(c)) - 1):
        pw = _dot_x3_many(pw, pw)
        tp = _dot_x3_many(t_inv, pw)
        t_inv = [t_inv[h] + tp[h] for h in hh]
    rhs = [jnp.concatenate([vb[h], kb[h] * egcol[h]], axis=1) for h in hh]
    sol = _dot_x3_many(t_inv, rhs)
    qk = [_dot_nt(q[h].astype(BF16), kbf[h]) for h in hh]
    attn = [(qk[h] * decay[h]).astype(BF16) for h in hh]
    q_dec = [(q[h] * egcol[h]).astype(BF16) for h in hh]
    k_dec = [(k[h] * jnp.exp(glast[h] - gcol[h])).astype(BF16) for h in hh]
    sb = [states[h].astype(BF16) for h in hh]
    ws = [_dot(sol[h][:, DN_VDIM:].astype(BF16), sb[h]) for h in hh]
    v_new = [(sol[h][:, :DN_VDIM] - ws[h]).astype(BF16) for h in hh]
    o1 = [_dot(q_dec[h], sb[h]) for h in hh]
    o2 = [_dot(attn[h], v_new[h]) for h in hh]
    kv = [_dot_tn(k_dec[h], v_new[h]) for h in hh]
    for h in hh:
        st_ref[h] = states[h] * jnp.exp(glast[h][0:1, :]) + kv[h]
    o = [o1[h] + o2[h] for h in hh]
    outs = [o[h] * lax.rsqrt(jnp.mean(o[h] * o[h], axis=-1, keepdims=True) + EPS) * ng * za[:, hsl[h]] for h in hh]
    o_ref[0] = jnp.concatenate(outs, axis=1).astype(o_ref.dtype)


def _deltanet(p32, conv_w, alog_row, dt_row, norm_g):
    bsz, s, _ = p32.shape
    c = DN_CHUNK
    blk = lambda col: pl.BlockSpec((1, c, DN_W), lambda b, i: (b, i, col // DN_W))
    full2 = lambda a: pl.BlockSpec(a.shape, lambda b, i: (0, 0))
    ng = norm_g.reshape(1, DN_VDIM)
    return pl.pallas_call(
        _deltanet_kernel,
        grid=(bsz, s // c),
        in_specs=[blk(C_DNQ), blk(C_DNK), blk(C_DNV), blk(C_DNZ),
                  pl.BlockSpec((1, c, LANES), lambda b, i: (b, i, C_SMALL // LANES)),
                  full2(conv_w), full2(alog_row), full2(dt_row), full2(ng)],
        out_specs=pl.BlockSpec((1, c, DN_W), lambda b, i: (b, i, 0)),
        out_shape=jax.ShapeDtypeStruct((bsz, s, DN_W), BF16),
        scratch_shapes=[pltpu.VMEM((3, c + 8, DN_W), F32),
                        pltpu.VMEM((DN_HEADS, DN_KDIM, DN_VDIM), F32)],
        compiler_params=_cparams("arbitrary", "arbitrary"),
        name="deltanet",
    )(p32, p32, p32, p32, p32, conv_w, alog_row, dt_row, ng)


def _mix_out_kernel(x_ref, mod_ref, onsa_ref, odn_ref, gn_ref, gd_ref, wn_ref, wd_ref, wo_ref, o_ref):
    merged = (_sigmoid(gn_ref[0]) * _dot(onsa_ref[0], wn_ref[...])
              + _sigmoid(gd_ref[0]) * _dot(odn_ref[0], wd_ref[...]))
    y = _dot(merged.astype(BF16), wo_ref[...])
    o_ref[0] = x_ref[0] + mod_ref[0][2:3] * y


def _mix_out(x, mod, o_nsa, o_dn, p32, wn, wd, wo, tm=256):
    bsz, s, d = x.shape
    full2 = lambda a: pl.BlockSpec(a.shape, lambda b, i: (0, 0))
    return pl.pallas_call(
        _mix_out_kernel,
        grid=(bsz, s // tm),
        in_specs=[pl.BlockSpec((1, tm, d), lambda b, i: (b, i, 0)),
                  pl.BlockSpec((1, 6, d), lambda b, i: (b, 0, 0)),
                  pl.BlockSpec((1, tm, NSA_Q_W), lambda b, i: (b, i, 0)),
                  pl.BlockSpec((1, tm, DN_W), lambda b, i: (b, i, 0)),
                  pl.BlockSpec((1, tm, d), lambda b, i: (b, i, C_GNSA // d)),
                  pl.BlockSpec((1, tm, d), lambda b, i: (b, i, C_GDN // d)),
                  full2(wn), full2(wd), full2(wo)],
        out_specs=pl.BlockSpec((1, tm, d), lambda b, i: (b, i, 0)),
        out_shape=jax.ShapeDtypeStruct((bsz, s, d), F32),
        compiler_params=_cparams("arbitrary", "arbitrary"),
        name="mix_out",
    )(x, mod, o_nsa, o_dn, p32, p32, wn, wd, wo)


def _top_k_rows(s, idx, k, payload=None):
    vals, picks = [], []
    for _ in range(k):
        mx = jnp.max(s, axis=0, keepdims=True)
        am = jnp.min(jnp.where(s == mx, idx, 1e9), axis=0, keepdims=True)
        hit = idx == am
        vals.append(mx)
        if payload is None:
            picks.append(am)
        else:
            picks.append(jnp.sum(jnp.where(hit, payload, 0.0), axis=0, keepdims=True))
        s = jnp.where(hit, -jnp.inf, s)
    return jnp.concatenate(vals, axis=0), jnp.concatenate(picks, axis=0)


def _peer_route_kernel(x_ref, mod_ref, g_ref, wq_ref, k1_ref, k2_ref, h_ref, idx_ref, gate_ref):
    tm = x_ref.shape[1]
    h2 = _modulated_norm(x_ref[0], g_ref[...], mod_ref[0], 3, 4)
    hb = h2.astype(BF16)
    h_ref[0] = hb
    qry = _dot(hb, wq_ref[...])
    kidx = lax.broadcasted_iota(jnp.int32, (PEER_NKEYS, tm), 0).astype(F32)
    kk = PEER_TOPK
    hk = kk // 2
    b_iota = kidx[0:kk]
    cidx = jnp.concatenate([b_iota] + [a * float(kk) + b_iota[0:hk] for a in range(1, hk)]
                           + [(b_iota[hk:kk]) * float(kk)], axis=0)
    for h in range(PEER_HEADS):
        qh = qry[:, h * PEER_DKEY:(h + 1) * PEER_DKEY].astype(BF16)
        s1 = _dot_nt(k1_ref[h], qh[:, :PEER_HALF])
        s2 = _dot_nt(k2_ref[h], qh[:, PEER_HALF:])
        v1, i1 = _top_k_rows(s1, kidx, PEER_TOPK)
        v2, i2 = _top_k_rows(s2, kidx, PEER_TOPK)
        cand_s = jnp.concatenate([v1[0:1] + v2] + [v1[a:a + 1] + v2[0:hk] for a in range(1, hk)]
                                 + [v1[hk:kk] + v2[0:1]], axis=0)
        e1 = i1 * float(PEER_NKEYS)
        cand_i = jnp.concatenate([e1[0:1] + i2] + [e1[a:a + 1] + i2[0:hk] for a in range(1, hk)]
                                 + [e1[hk:kk] + i2[0:1]], axis=0)
        top_s, top_e = _top_k_rows(cand_s, cidx, PEER_TOPK, payload=cand_i)
        e = jnp.exp(top_s - top_s[0:1])
        gate = e / jnp.sum(e, axis=0, keepdims=True)
        idx_ref[h * PEER_TOPK:(h + 1) * PEER_TOPK, :] = (top_e * float(PACKED_ROWS)).astype(jnp.int32)
        gate_ref[h * PEER_TOPK:(h + 1) * PEER_TOPK, :] = gate


def _peer_route(x1, mod, g, wq, k1, k2, tm=128):
    bsz, s, d = x1.shape
    nt = s // tm
    ne = PEER_HEADS * PEER_TOPK
    full = lambda a: pl.BlockSpec(a.shape, lambda b, i: (0,) * a.ndim)
    return pl.pallas_call(
        _peer_route_kernel,
        grid=(bsz, nt),
        in_specs=[pl.BlockSpec((1, tm, d), lambda b, i: (b, i, 0)),
                  pl.BlockSpec((1, 6, d), lambda b, i: (b, 0, 0)),
                  pl.BlockSpec((1, d), lambda b, i: (0, 0)),
                  full(wq), full(k1), full(k2)],
        out_specs=[pl.BlockSpec((1, tm, d), lambda b, i: (b, i, 0)),
                   pl.BlockSpec((ne, tm), lambda b, i: (0, b * nt + i)),
                   pl.BlockSpec((ne, tm), lambda b, i: (0, b * nt + i))],
        out_shape=[jax.ShapeDtypeStruct((bsz, s, d), BF16),
                   jax.ShapeDtypeStruct((ne, bsz * s), jnp.int32),
                   jax.ShapeDtypeStruct((ne, bsz * s), F32)],
        compiler_params=_cparams("arbitrary", "arbitrary"),
        name="peer_route",
    )(x1, mod, g.reshape(1, d), wq, k1, k2)


PEER_TOK = 64
EXPERT_TILE_ROWS = 8
PACKED_ROWS = EXPERT_TILE_ROWS // 2


def _load_table_once(tab_hbm, tab_vmem, sem):
    @pl.when(pl.program_id(0) == 0)
    def _():
        cp = pltpu.make_async_copy(tab_hbm, tab_vmem, sem)
        cp.start()
        cp.wait()


def _pack_table(tab):
    e = tab.shape[0]
    bits = lax.bitcast_convert_type(tab.astype(BF16).reshape(e * PACKED_ROWS, 2, LANES), jnp.uint16)
    bits = bits.astype(jnp.uint32)
    return lax.bitcast_convert_type(bits[:, 0, :] | (bits[:, 1, :] << 16), jnp.int32)


def _stage_experts(idx_ref, tab_vmem, stage, nsel):
    for r in range(nsel):
        row = pl.multiple_of(idx_ref[0, 0, r], PACKED_ROWS)
        stage[r * PACKED_ROWS:(r + 1) * PACKED_ROWS, :] = tab_vmem[pl.ds(row, PACKED_ROWS), :]


def _staged_tiles(stage, t, ncol):
    return pltpu.bitcast(stage[t * (ncol // 2):(t + 1) * (ncol // 2), :], BF16)


def _diag_mask(ncol):
    row = lax.broadcasted_iota(jnp.int32, (EXPERT_TILE_ROWS, ncol), 0)
    col = lax.broadcasted_iota(jnp.int32, (EXPERT_TILE_ROWS, ncol), 1)
    return (col & (EXPERT_TILE_ROWS - 1)) == row


def _peer_act_kernel(idx_ref, tab_hbm, h_ref, gate_ref, grp_ref, w_ref, tab_vmem, stage, sem, *, rows_per_tok):
    _load_table_once(tab_hbm, tab_vmem, sem)
    _stage_experts(idx_ref, tab_vmem, stage, PEER_TOK * rows_per_tok)
    ncol = rows_per_tok * EXPERT_TILE_ROWS
    diag = _diag_mask(ncol)
    sums = []
    for t in range(PEER_TOK):
        m = _staged_tiles(stage, t, ncol)
        rt = _dot_nt(h_ref[t], m)
        sums.append(jnp.sum(jnp.where(diag, rt, 0.0), axis=0, keepdims=True))
    part = jnp.concatenate(sums, axis=0)
    hi, mid, lo = _split3(part)
    grp = grp_ref[...]
    act = _dot(hi, grp) + (_dot(mid, grp) + _dot(lo, grp))
    w_ref[...] = gate_ref[...] * _gelu_tanh(act)


def _peer_out_kernel(idx_ref, tab_hbm, w_ref, exp_ref, x_ref, mod_ref, fg_ref, o_ref, tab_vmem, stage, sem,
                     *, rows_per_tok):
    _load_table_once(tab_hbm, tab_vmem, sem)
    _stage_experts(idx_ref, tab_vmem, stage, PEER_TOK * rows_per_tok)
    ncol = rows_per_tok * EXPERT_TILE_ROWS
    diag = _diag_mask(ncol)
    wh, wl = _split2(w_ref[...])
    ex = exp_ref[...]
    weh = _dot(wh, ex)
    wel = _dot(wl, ex)
    g2 = mod_ref[0, 5]
    fg = fg_ref[...]
    nel = float(EXPERT_TILE_ROWS * LANES)
    for t in range(PEER_TOK):
        lhs = jnp.concatenate([jnp.where(diag, weh[t:t + 1], 0.0), jnp.where(diag, wel[t:t + 1], 0.0)], axis=0)
        o16 = _dot(lhs.astype(BF16), _staged_tiles(stage, t, ncol))
        y = o16[0:EXPERT_TILE_ROWS] + o16[EXPERT_TILE_ROWS:]
        xo = x_ref[t] + g2 * y
        ms = jnp.sum(jnp.sum(xo * xo, axis=1, keepdims=True), axis=0, keepdims=True) / nel
        o_ref[t] = xo * lax.rsqrt(ms + EPS) * fg


def _peer_experts(idx3, tab_u, tab_v, h3, gate, x3, mod4, fg3, s):
    n = h3.shape[0]
    ne = gate.shape[1]
    nstep = n // PEER_TOK
    nsel = PEER_TOK * ne
    ncol = ne * EXPERT_TILE_ROWS
    grp = jnp.asarray(np.arange(ncol)[:, None] // EXPERT_TILE_ROWS == np.arange(ne)[None, :], BF16)
    idx_spec = pl.BlockSpec((1, 1, nsel), lambda t: (t, 0, 0), memory_space=pltpu.SMEM)
    any_spec = pl.BlockSpec(memory_space=pl.ANY)
    tile_spec = pl.BlockSpec((PEER_TOK, EXPERT_TILE_ROWS, LANES), lambda t: (t, 0, 0))
    row_spec = pl.BlockSpec((PEER_TOK, ne), lambda t: (t, 0))
    scratch = [pltpu.VMEM(tab_u.shape, jnp.int32), pltpu.VMEM((PEER_TOK * ncol // 2, LANES), jnp.int32),
               pltpu.SemaphoreType.DMA]
    w = pl.pallas_call(
        functools.partial(_peer_act_kernel, rows_per_tok=ne),
        grid=(nstep,),
        in_specs=[idx_spec, any_spec, tile_spec, row_spec, pl.BlockSpec(grp.shape, lambda t: (0, 0))],
        out_specs=row_spec,
        out_shape=jax.ShapeDtypeStruct((n, ne), F32),
        scratch_shapes=scratch,
        compiler_params=_cparams("arbitrary"),
        name="peer_act",
    )(idx3, tab_u, h3, gate, grp)
    return pl.pallas_call(
        functools.partial(_peer_out_kernel, rows_per_tok=ne),
        grid=(nstep,),
        in_specs=[idx_spec, any_spec, row_spec, pl.BlockSpec((ne, ncol), lambda t: (0, 0)), tile_spec,
                  pl.BlockSpec((1, 6, EXPERT_TILE_ROWS, LANES), lambda t: ((t * PEER_TOK) // s, 0, 0, 0)),
                  pl.BlockSpec((EXPERT_TILE_ROWS, LANES), lambda t: (0, 0))],
        out_specs=tile_spec,
        out_shape=jax.ShapeDtypeStruct((n, EXPERT_TILE_ROWS, LANES), F32),
        scratch_shapes=scratch,
        compiler_params=_cparams("arbitrary"),
        name="peer_out",
    )(idx3, tab_v, w, grp.T, x3, mod4, fg3)


def _rel_bucket(dist):
    dist = jnp.maximum(dist, 0)
    max_exact = REL_BUCKETS // 2
    scaled = jnp.log(jnp.maximum(dist, 1).astype(F32) / max_exact) / math.log(REL_MAX_DIST / max_exact)
    large = jnp.minimum(max_exact + (scaled * (REL_BUCKETS - max_exact)).astype(jnp.int32), REL_BUCKETS - 1)
    return jnp.where(dist < max_exact, dist, large)


def _bias_tables(rel_bias, s, ncmp):
    dist_bias = rel_bias[_rel_bucket(jnp.arange(s))].T
    nq = s // ATT_TILE
    per_tile = ATT_TILE // CMP_STRIDE
    r = np.arange(ATT_TILE)[:, None]
    k0 = per_tile * (nq - 1)
    k = np.arange(k0 + ncmp)[None, :]
    dist_g = r - CMP_STRIDE * (k - k0) - (CMP_LEN - 1)
    g = dist_bias[:, np.clip(dist_g, 0, s - 1)]
    wins = jnp.stack([g[:, :, k0 - per_tile * i:k0 - per_tile * i + ncmp] for i in range(nq)], axis=1)
    bias_c = wins.reshape(rel_bias.shape[1], s, ncmp)
    ii = jnp.arange(ATT_TILE)[:, None]
    jj = jnp.arange(ATT_TILE)[None, :]
    dist = jnp.stack([m * ATT_TILE + ii - jj for m in range(3)])
    toep = dist_bias[:, jnp.clip(dist, 0, s - 1)]
    return bias_c, toep


def _check_bucket_saturation():
    d = np.arange(ATT_TILE + 1, 1 << 16)
    max_exact = REL_BUCKETS // 2
    scaled = np.log(d.astype(np.float32) / max_exact) / math.log(REL_MAX_DIST / max_exact)
    large = np.minimum(max_exact + (scaled * (REL_BUCKETS - max_exact)).astype(np.int32), REL_BUCKETS - 1)
    assert (large == REL_BUCKETS - 1).all()


_check_bucket_saturation()


def _token_mixer(x, mod, rel_bias, norm_g, w_in, cmp_pos_emb, w_ck1, w_ck2, w_cv1, w_cv2, conv_w, a_log, dt_bias,
                 dn_norm_g, w_branch_nsa, w_branch_dn, w_out):
    bsz, s, d = x.shape
    assert s % ATT_TILE == 0 and s % CMP_STRIDE == 0 and WINDOW % ATT_TILE == 0 and d == C_GDN - C_GNSA
    pts = np.cumsum(IN_SPLIT_SIZES + (d, d))[:-1]
    assert w_in.shape[1] == sum(IN_SPLIT_SIZES) + 2 * d
    (w_q, w_kc, w_vc, w_ks, w_vs, w_kw, w_vw, w_gates, w_dq, w_dk, w_dv, w_dz, w_db, w_da, w_gn, w_gd) = (
        jnp.split(w_in, [int(p) for p in pts], axis=1))
    w16 = jnp.concatenate([w_q * (HEAD_DIM ** -0.5), w_ks, w_vs, w_kw, w_vw], axis=1).astype(BF16)
    small_pad = jnp.zeros((d, LANES - NSA_HEADS * 3 - 2 * DN_HEADS), w_in.dtype)
    w32 = jnp.concatenate([w_dq, w_dk, w_dv, w_dz, w_gn, w_gd, w_kc, w_vc, w_gates, w_db, w_da, small_pad],
                          axis=1).astype(BF16)
    assert w16.shape[1] == C16 and w32.shape[1] == C32
    p16, p32 = _proj_in(x, mod, norm_g, w16, w32)

    nrow = s // CMP_STRIDE
    k2 = p32[:, :, C_KC:C_KC + NSA_KV_W].reshape(bsz, nrow, CMP_STRIDE * NSA_KV_W)
    v2 = p32[:, :, C_VC:C_VC + NSA_KV_W].reshape(bsz, nrow, CMP_STRIDE * NSA_KV_W)
    kc, vc = _compress(k2, v2, cmp_pos_emb, w_ck1, w_ck2, w_cv1, w_cv2)
    nslc = s // SLC_BLOCK
    top_n = min(SLC_TOPN, nslc)
    bias_c, toep = _bias_tables(rel_bias, s, nrow)
    cmp_start = np.arange(nrow) * CMP_STRIDE
    slc_start = np.arange(nslc) * SLC_BLOCK
    ovt = ((cmp_start[None, :] <= slc_start[:, None] + SLC_BLOCK - 1)
           & (cmp_start[None, :] + CMP_LEN - 1 >= slc_start[:, None]))
    ovt = jnp.asarray(ovt, BF16)
    o_c, mask = _nsa_cmp(p16, kc, vc, bias_c, ovt, top_n)
    nkt = s // ATT_TILE
    key_blk = (np.arange(nkt)[:, None, None] * ATT_TILE + np.arange(ATT_TILE)[None, None, :]) // SLC_BLOCK
    expand = jnp.asarray(key_blk == np.arange(nslc)[None, :, None], BF16)
    o_nsa = _nsa_sw(p16, p32, mask, expand, toep, o_c)

    lane_row = lambda vec: jnp.zeros((1, LANES), F32).at[0, SM_A:SM_A + DN_HEADS].set(vec.astype(F32))
    o_dn = _deltanet(p32, conv_w, lane_row(a_log), lane_row(dt_bias), dn_norm_g)

    return _mix_out(x, mod, o_nsa, o_dn, p32, w_branch_nsa.astype(BF16), w_branch_dn.astype(BF16),
                    w_out.astype(BF16))


def _peer_block(x1, mod, norm_g, final_g, w_query, keys1, keys2, expert_u, expert_v):
    bsz, s, d = x1.shape
    n = bsz * s
    ne = PEER_HEADS * PEER_TOPK
    h2, idx_t, gate_t = _peer_route(x1, mod, norm_g, w_query.astype(BF16), keys1.astype(BF16), keys2.astype(BF16))
    assert d == EXPERT_TILE_ROWS * LANES
    nstep = n // PEER_TOK
    idx3 = idx_t.T.reshape(nstep, 1, PEER_TOK * ne)
    tile = (EXPERT_TILE_ROWS, LANES)
    tab_u = _pack_table(expert_u)
    tab_v = _pack_table(expert_v)
    out = _peer_experts(idx3, tab_u, tab_v, h2.reshape((n,) + tile), gate_t.T, x1.reshape((n,) + tile),
                        mod.reshape((bsz, 6) + tile), final_g.reshape(tile), s)
    return out.reshape(bsz, s, d)


def kernel(x, c, rel_bias, final_g, w_ada, b_ada, norm1_g, w_in, cmp_pos_emb, w_cmp_k1, w_cmp_k2, w_cmp_v1,
           w_cmp_v2, dn_conv_w, dn_A_log, dn_dt_bias, dn_norm_g, w_branch_nsa, w_branch_dn, w_out, norm2_g,
           peer_w_query, peer_keys1, peer_keys2, peer_u, peer_v):
    depth = w_ada.shape[0]
    assert depth == 1, "the final norm is fused into the last PEER kernel; one layer is supported"
    bsz, s, d = x.shape
    mod = _ada(c, w_ada[0], b_ada[0]).reshape(bsz, 6, d)
    x1 = _token_mixer(x, mod, rel_bias, norm1_g[0], w_in[0], cmp_pos_emb[0], w_cmp_k1[0], w_cmp_k2[0], w_cmp_v1[0],
                      w_cmp_v2[0], dn_conv_w[0], dn_A_log[0], dn_dt_bias[0], dn_norm_g[0], w_branch_nsa[0],
                      w_branch_dn[0], w_out[0])
    return _peer_block(x1, mod, norm2_g[0], final_g, peer_w_query[0], peer_keys1[0], peer_keys2[0], peer_u[0],
                       peer_v[0])
```

```python
import functools
import math

import numpy as np
import jax
import jax.numpy as jnp
from jax import lax
from jax.experimental import pallas as pl
from jax.experimental.pallas import tpu as pltpu

F32 = jnp.float32
BF16 = jnp.bfloat16

NSA_HEADS = 8
NSA_KV_HEADS = 2
NSA_GROUP = NSA_HEADS // NSA_KV_HEADS
HEAD_DIM = 64
CMP_LEN = 32
CMP_STRIDE = 16
CMP_HIDDEN = 64
SLC_BLOCK = 64
SLC_TOPN = 16
WINDOW = 512
FORCED_BONUS = 1.0e4
DN_HEADS = 8
DN_KDIM = 64
DN_VDIM = 64
DN_CONV = 4
DN_CHUNK = 64
PEER_HEADS = 8
PEER_NKEYS = 128
PEER_DKEY = 128
PEER_HALF = PEER_DKEY // 2
PEER_TOPK = 16
REL_BUCKETS = 32
REL_MAX_DIST = 128
EPS = 1e-6
NEG_INF = -1e30

NSA_Q_W = NSA_HEADS * HEAD_DIM
NSA_KV_W = NSA_KV_HEADS * HEAD_DIM
DN_W = DN_HEADS * DN_KDIM
IN_SPLIT_SIZES = (NSA_Q_W, NSA_KV_W, NSA_KV_W, NSA_KV_W, NSA_KV_W, NSA_KV_W, NSA_KV_W, NSA_HEADS * 3,
                  DN_W, DN_W, DN_W, DN_W, DN_HEADS, DN_HEADS)

LANES = 128
ATT_TILE = 128
TILES_PER_TRIP = 8
VMEM_LIMIT = 56 * 1024 * 1024

C_DNQ, C_DNK, C_DNV, C_DNZ = 0, 512, 1024, 1536
C_GNSA, C_GDN = 2048, 3072
C_KC, C_VC, C_SMALL = 4096, 4224, 4352
C32 = 4480
SM_GATES, SM_B, SM_A = 0, 24, 32
C16 = 1024


def _gelu_tanh(x):
    return 0.5 * x * (1.0 + jnp.tanh(math.sqrt(2.0 / math.pi) * (x + 0.044715 * (x * x * x))))


def _sigmoid(x):
    return 1.0 / (1.0 + jnp.exp(-x))


def _silu(x):
    return x * _sigmoid(x)


def _softplus(x):
    return jnp.maximum(x, 0.0) + jnp.log1p(jnp.exp(-jnp.abs(x)))


def _dot(a, b):
    return jnp.dot(a, b, preferred_element_type=F32)


def _dot_nt(a, b):
    return lax.dot_general(a, b, (((1,), (1,)), ((), ())), preferred_element_type=F32)


def _dot_tn(a, b):
    return lax.dot_general(a, b, (((0,), (0,)), ((), ())), preferred_element_type=F32)


def _split2(x):
    hi = x.astype(BF16)
    lo = (x - hi.astype(F32)).astype(BF16)
    return hi, lo


def _split3(x):
    hi = x.astype(BF16)
    r = x - hi.astype(F32)
    mid = r.astype(BF16)
    lo = (r - mid.astype(F32)).astype(BF16)
    return hi, mid, lo


def _dot_x3(a, b):
    ah, al = _split2(a)
    bh, bl = _split2(b)
    return _dot(ah, bh) + (_dot(ah, bl) + _dot(al, bh))


def _dot_01(m01, x):
    hi, mid, lo = _split3(x)
    return _dot(m01, hi) + (_dot(m01, mid) + _dot(m01, lo))


def _dot_x3_many(a_list, b_list):
    n = range(len(a_list))
    asp = [_split2(a) for a in a_list]
    bsp = [_split2(b) for b in b_list]
    hh = [_dot(asp[i][0], bsp[i][0]) for i in n]
    hl = [_dot(asp[i][0], bsp[i][1]) for i in n]
    lh = [_dot(asp[i][1], bsp[i][0]) for i in n]
    return [hh[i] + (hl[i] + lh[i]) for i in n]


def _dot_01_many(m01, x_list):
    n = range(len(x_list))
    sp = [_split3(x) for x in x_list]
    d0 = [_dot(m01, sp[i][0]) for i in n]
    d1 = [_dot(m01, sp[i][1]) for i in n]
    d2 = [_dot(m01, sp[i][2]) for i in n]
    return [d0[i] + (d1[i] + d2[i]) for i in n]


def _cparams(*sem):
    return pltpu.CompilerParams(dimension_semantics=sem, vmem_limit_bytes=VMEM_LIMIT)


def _ada_kernel(c_ref, w_ref, b_ref, o_ref):
    cond = _silu(c_ref[...])
    o_ref[...] = _dot(cond.astype(BF16), w_ref[...].astype(BF16)) + b_ref[...]


def _ada(c, w_ada, b_ada):
    bsz, d = c.shape
    n_out = w_ada.shape[1]
    blk = d
    return pl.pallas_call(
        _ada_kernel,
        grid=(n_out // blk,),
        in_specs=[pl.BlockSpec((bsz, d), lambda j: (0, 0)),
                  pl.BlockSpec((d, blk), lambda j: (0, j)),
                  pl.BlockSpec((1, blk), lambda j: (0, j))],
        out_specs=pl.BlockSpec((bsz, blk), lambda j: (0, j)),
        out_shape=jax.ShapeDtypeStruct((bsz, n_out), F32),
        compiler_params=_cparams("arbitrary"),
        name="ada",
    )(c, w_ada, b_ada.reshape(1, n_out))


def _modulated_norm(x, g, mod, shift_row, scale_row):
    var = jnp.mean(x * x, axis=-1, keepdims=True)
    y = x * lax.rsqrt(var + EPS) * g
    return y * (1.0 + mod[scale_row:scale_row + 1]) + mod[shift_row:shift_row + 1]


def _proj_in_kernel(x_ref, mod_ref, g_ref, w16_ref, w32_ref, o16_ref, o32_ref):
    h = _modulated_norm(x_ref[0], g_ref[...], mod_ref[0], 0, 1).astype(BF16)
    step = 512
    for c0 in range(0, o16_ref.shape[2], step):
        c1 = min(c0 + step, o16_ref.shape[2])
        o16_ref[0, :, c0:c1] = _dot(h, w16_ref[:, c0:c1]).astype(BF16)
    for c0 in range(0, o32_ref.shape[2], step):
        c1 = min(c0 + step, o32_ref.shape[2])
        o32_ref[0, :, c0:c1] = _dot(h, w32_ref[:, c0:c1])


def _proj_in(x, mod, g, w16, w32, tm=256):
    bsz, s, d = x.shape
    return pl.pallas_call(
        _proj_in_kernel,
        grid=(bsz, s // tm),
        in_specs=[pl.BlockSpec((1, tm, d), lambda b, i: (b, i, 0)),
                  pl.BlockSpec((1, 6, d), lambda b, i: (b, 0, 0)),
                  pl.BlockSpec((1, d), lambda b, i: (0, 0)),
                  pl.BlockSpec(w16.shape, lambda b, i: (0, 0)),
                  pl.BlockSpec(w32.shape, lambda b, i: (0, 0))],
        out_specs=[pl.BlockSpec((1, tm, w16.shape[1]), lambda b, i: (b, i, 0)),
                   pl.BlockSpec((1, tm, w32.shape[1]), lambda b, i: (b, i, 0))],
        out_shape=[jax.ShapeDtypeStruct((bsz, s, w16.shape[1]), BF16),
                   jax.ShapeDtypeStruct((bsz, s, w32.shape[1]), F32)],
        compiler_params=_cparams("arbitrary", "arbitrary"),
        name="proj_in",
    )(x, mod, g.reshape(1, d), w16, w32)


def _compress_kernel(k2_ref, v2_ref, posa_ref, posb_ref, ak_ref, bk_ref, w2k_ref, av_ref, bv_ref, w2v_ref,
                     kc_ref, vc_ref):
    nrow = k2_ref.shape[1]

    def one(x_ref, a_ref, b_ref, w2_ref, o_ref):
        x = x_ref[0]
        ya = _dot((x + posa_ref[...]).astype(BF16), a_ref[...])
        yb = _dot((x + posb_ref[...]).astype(BF16), b_ref[...])
        hid = _gelu_tanh(ya + pltpu.roll(yb, nrow - 1, 0))
        out = _dot(hid.astype(BF16), w2_ref[...])
        for h in range(NSA_KV_HEADS):
            o_ref[0, h] = out[:, h * HEAD_DIM:(h + 1) * HEAD_DIM]

    one(k2_ref, ak_ref, bk_ref, w2k_ref, kc_ref)
    one(v2_ref, av_ref, bv_ref, w2v_ref, vc_ref)


def _expand_cmp_weights(w1, w2):
    half = CMP_LEN // 2
    w1r = w1.reshape(2, half, HEAD_DIM, CMP_HIDDEN)
    eye = jnp.eye(NSA_KV_HEADS, dtype=w1.dtype)
    exp = jnp.einsum('pldj,hg->plhdgj', w1r, eye).reshape(2, half * NSA_KV_HEADS * HEAD_DIM,
                                                          NSA_KV_HEADS * CMP_HIDDEN)
    w2e = jnp.einsum('jd,hg->hjgd', w2, eye).reshape(NSA_KV_HEADS * CMP_HIDDEN, NSA_KV_HEADS * HEAD_DIM)
    return exp[0].astype(BF16), exp[1].astype(BF16), w2e.astype(BF16)


def _compress(k2, v2, pos_emb, w_ck1, w_ck2, w_cv1, w_cv2):
    bsz, nrow, width = k2.shape
    half = CMP_LEN // 2
    pos = jnp.broadcast_to(pos_emb.reshape(2, half, 1, HEAD_DIM), (2, half, NSA_KV_HEADS, HEAD_DIM))
    pos = pos.reshape(2, 1, width)
    ak, bk, w2k = _expand_cmp_weights(w_ck1, w_ck2)
    av, bv, w2v = _expand_cmp_weights(w_cv1, w_cv2)
    full2 = lambda a: pl.BlockSpec(a.shape, lambda b: (0, 0))
    out_sds = jax.ShapeDtypeStruct((bsz, NSA_KV_HEADS, nrow, HEAD_DIM), F32)
    out_spec = pl.BlockSpec((1, NSA_KV_HEADS, nrow, HEAD_DIM), lambda b: (b, 0, 0, 0))
    return pl.pallas_call(
        _compress_kernel,
        grid=(bsz,),
        in_specs=[pl.BlockSpec((1, nrow, width), lambda b: (b, 0, 0)),
                  pl.BlockSpec((1, nrow, width), lambda b: (b, 0, 0)),
                  full2(pos[0]), full2(pos[1]), full2(ak), full2(bk), full2(w2k), full2(av), full2(bv), full2(w2v)],
        out_specs=[out_spec, out_spec],
        out_shape=[out_sds, out_sds],
        compiler_params=_cparams("arbitrary"),
        name="compress",
    )(k2, v2, pos[0], pos[1], ak, bk, w2k, av, bv, w2v)


def _nsa_cmp_kernel(q_ref, kc_ref, vc_ref, bias_ref, ovt_ref, oc_ref, mask_ref, *, top_n):
    tq = q_ref.shape[1]
    ncmp = kc_ref.shape[2]
    nslc = ovt_ref.shape[0]
    s0 = pl.program_id(2) * tq
    kcb = kc_ref[0, 0].astype(BF16)
    vcb = vc_ref[0, 0].astype(BF16)
    t_q = s0 + lax.broadcasted_iota(jnp.int32, (tq, ncmp), 0)
    n_i = lax.broadcasted_iota(jnp.int32, (tq, ncmp), 1)
    valid = (n_i * CMP_STRIDE + (CMP_LEN - 1)) <= t_q
    ovt = ovt_ref[...]
    gs = range(NSA_GROUP)
    sc = [_dot_nt(q_ref[0, :, g * HEAD_DIM:(g + 1) * HEAD_DIM], kcb) for g in gs]
    l = [jnp.where(valid, sc[g] + bias_ref[g], NEG_INF) for g in gs]
    m = [jnp.max(l[g], axis=-1, keepdims=True) for g in gs]
    e = [jnp.where(valid, jnp.exp(l[g] - m[g]), 0.0) for g in gs]
    den = [jnp.maximum(jnp.sum(e[g], axis=-1, keepdims=True), 1e-30) for g in gs]
    pb = [(e[g] / den[g]).astype(BF16) for g in gs]
    outs = [_dot(pb[g], vcb) for g in gs]
    imps = [_dot_nt(ovt, pb[g]) for g in gs]
    imp_t = functools.reduce(lambda acc, x: acc + x, imps)
    oc_ref[0] = jnp.concatenate(outs, axis=1)

    t_l = s0 + lax.broadcasted_iota(jnp.int32, (nslc, tq), 1)
    j_s = lax.broadcasted_iota(jnp.int32, (nslc, tq), 0)
    cur = lax.shift_right_logical(t_l, int(math.log2(SLC_BLOCK)))
    forced = (j_s == 0) | (j_s == cur) | (j_s == cur - 1)
    future = (j_s * SLC_BLOCK) > t_l
    score = jnp.where(future, NEG_INF, imp_t + jnp.where(forced, FORCED_BONUS, 0.0))
    rank = jnp.zeros((nslc, tq), F32)
    for i in range(nslc):
        r = score[i:i + 1, :]
        beats = (r > score) | ((r == score) & (j_s > i))
        rank = rank + jnp.where(beats, 1.0, 0.0)
    sel_t = jnp.where(rank < float(top_n), 1.0, 0.0).astype(BF16)
    eye = (lax.broadcasted_iota(jnp.int32, (tq, tq), 0) == lax.broadcasted_iota(jnp.int32, (tq, tq), 1))
    mask_ref[0, 0] = _dot_nt(jnp.where(eye, 1.0, 0.0).astype(BF16), sel_t).astype(BF16)


def _nsa_cmp(p16, kc, vc, bias_c, ovt, top_n):
    bsz, s, _ = p16.shape
    tq = ATT_TILE
    ncmp = kc.shape[2]
    nslc = ovt.shape[0]
    qw = NSA_GROUP * HEAD_DIM
    return pl.pallas_call(
        functools.partial(_nsa_cmp_kernel, top_n=top_n),
        grid=(bsz, NSA_KV_HEADS, s // tq),
        in_specs=[pl.BlockSpec((1, tq, qw), lambda b, h, i: (b, i, h)),
                  pl.BlockSpec((1, 1, ncmp, HEAD_DIM), lambda b, h, i: (b, h, 0, 0)),
                  pl.BlockSpec((1, 1, ncmp, HEAD_DIM), lambda b, h, i: (b, h, 0, 0)),
                  pl.BlockSpec((NSA_GROUP, tq, ncmp), lambda b, h, i: (h, i, 0)),
                  pl.BlockSpec(ovt.shape, lambda b, h, i: (0, 0))],
        out_specs=[pl.BlockSpec((1, tq, qw), lambda b, h, i: (b, i, h)),
                   pl.BlockSpec((1, 1, tq, nslc), lambda b, h, i: (b, h, i, 0))],
        out_shape=[jax.ShapeDtypeStruct((bsz, s, NSA_Q_W), F32),
                   jax.ShapeDtypeStruct((bsz, NSA_KV_HEADS, s, nslc), BF16)],
        compiler_params=_cparams("arbitrary", "arbitrary", "arbitrary"),
        name="nsa_cmp",
    )(p16, kc, vc, bias_c, ovt)


def _nsa_sw_kernel(q_ref, ks_ref, vs_ref, kw_ref, vw_ref, mask_ref, exp_ref, toep_ref, oc_ref, sm_ref, o_ref):
    tq = q_ref.shape[1]
    tk = ATT_TILE
    i = pl.program_id(1)
    s0 = i * tq
    gate = _sigmoid(sm_ref[0])
    t_q = s0 + lax.broadcasted_iota(jnp.int32, (tq, tk), 0)
    k_l = lax.broadcasted_iota(jnp.int32, (tq, tk), 1)
    n_win = WINDOW // tk
    pieces = []
    lane = lax.broadcasted_iota(jnp.int32, (tq, LANES), 1)
    assert NSA_KV_HEADS * HEAD_DIM == LANES and tq == tk
    for h in range(NSA_KV_HEADS):
        in_h = (lane >= h * HEAD_DIM) & (lane < (h + 1) * HEAD_DIM)
        qs = []
        for g in range(NSA_GROUP):
            hd = h * NSA_GROUP + g
            pair = q_ref[0, :, (hd // 2) * LANES:(hd // 2 + 1) * LANES].astype(F32)
            if hd % 2 != h:
                pair = pltpu.roll(pair, HEAD_DIM, 1)
            qs.append(jnp.where(in_h, pair, 0.0).astype(BF16))
        mq = mask_ref[0, h]

        def tile_update(j, carry, k_ref, v_ref, valid_fn, lo, per):
            us = range(per)
            gs = range(NSA_GROUP)
            kt_raw = [lo + per * j + u for u in us]
            kt = [jnp.minimum(kt_raw[u], i) for u in us]
            k = [k_ref[0, pl.ds(pl.multiple_of(kt[u] * tk, tk), tk), :] for u in us]
            v = [jnp.where(in_h, v_ref[0, pl.ds(pl.multiple_of(kt[u] * tk, tk), tk), :], 1.0) for u in us]
            mm = [jnp.minimum(i - kt[u], 2) for u in us]
            valid = [valid_fn(kt[u], kt_raw[u]) for u in us]
            sc = [[_dot_nt(qs[g], k[u]) for u in us] for g in gs]
            lg = [[jnp.where(valid[u], sc[g][u] + toep_ref[h * NSA_GROUP + g, mm[u]], NEG_INF) for u in us]
                  for g in gs]
            mx = [functools.reduce(jnp.maximum, [jnp.max(lg[g][u], axis=-1, keepdims=True) for u in us])
                  for g in gs]
            m_new = [jnp.maximum(carry[g][0], mx[g]) for g in gs]
            alpha = [jnp.exp(carry[g][0] - m_new[g]) for g in gs]
            e = [[jnp.where(valid[u], jnp.exp(lg[g][u] - m_new[g]), 0.0).astype(BF16) for u in us] for g in gs]
            pv = [functools.reduce(lambda a, b: a + b, [_dot(e[g][u], v[u]) for u in us]) for g in gs]
            return tuple((m_new[g], alpha[g] * carry[g][1] + pv[g]) for g in gs)

        def sel_valid(kt, kt_raw):
            mk = _dot(mq, exp_ref[kt])
            kpos = kt_raw * tk + k_l
            return (mk > 0.5) & (kpos <= t_q)

        def win_valid(kt, kt_raw):
            rel = t_q - (kt_raw * tk + k_l)
            return (rel >= 0) & (rel < WINDOW)

        init = tuple((jnp.full((tq, 1), NEG_INF, F32), jnp.zeros((tq, LANES), F32)) for _ in range(NSA_GROUP))
        sel_trips = (i + TILES_PER_TRIP) // TILES_PER_TRIP
        sel = lax.fori_loop(0, sel_trips,
                            lambda j, c: tile_update(j, c, ks_ref, vs_ref, sel_valid, 0, TILES_PER_TRIP), init)
        win = tile_update(0, init, kw_ref, vw_ref, win_valid, jnp.maximum(i - n_win, 0), n_win + 1)
        hs = slice(h * HEAD_DIM, (h + 1) * HEAD_DIM)
        den = (1 - h) * HEAD_DIM
        for g in range(NSA_GROUP):
            hd = h * NSA_GROUP + g
            c = SM_GATES + hd * 3
            o_c = oc_ref[0, :, hd * HEAD_DIM:(hd + 1) * HEAD_DIM]
            o_s = (sel[g][1] / jnp.maximum(sel[g][1][:, den:den + 1], 1e-30))[:, hs]
            o_w = (win[g][1] / jnp.maximum(win[g][1][:, den:den + 1], 1e-30))[:, hs]
            pieces.append(gate[:, c:c + 1] * o_c + gate[:, c + 1:c + 2] * o_s + gate[:, c + 2:c + 3] * o_w)
    o_ref[0] = jnp.concatenate(pieces, axis=1).astype(o_ref.dtype)


def _nsa_sw(p16, p32, mask, expand, toep, o_c):
    bsz, s, _ = p16.shape
    tq = ATT_TILE
    nslc = mask.shape[3]
    kvb = NSA_Q_W // NSA_KV_W
    return pl.pallas_call(
        _nsa_sw_kernel,
        grid=(bsz, s // tq),
        in_specs=[pl.BlockSpec((1, tq, NSA_Q_W), lambda b, i: (b, i, 0)),
                  pl.BlockSpec((1, s, NSA_KV_W), lambda b, i: (b, 0, kvb)),
                  pl.BlockSpec((1, s, NSA_KV_W), lambda b, i: (b, 0, kvb + 1)),
                  pl.BlockSpec((1, s, NSA_KV_W), lambda b, i: (b, 0, kvb + 2)),
                  pl.BlockSpec((1, s, NSA_KV_W), lambda b, i: (b, 0, kvb + 3)),
                  pl.BlockSpec((1, NSA_KV_HEADS, tq, nslc), lambda b, i: (b, 0, i, 0)),
                  pl.BlockSpec(expand.shape, lambda b, i: (0, 0, 0)),
                  pl.BlockSpec(toep.shape, lambda b, i: (0, 0, 0, 0)),
                  pl.BlockSpec((1, tq, NSA_Q_W), lambda b, i: (b, i, 0)),
                  pl.BlockSpec((1, tq, LANES), lambda b, i: (b, i, C_SMALL // LANES))],
        out_specs=pl.BlockSpec((1, tq, NSA_Q_W), lambda b, i: (b, i, 0)),
        out_shape=jax.ShapeDtypeStruct((bsz, s, NSA_Q_W), BF16),
        compiler_params=_cparams("arbitrary", "arbitrary"),
        name="nsa_sw",
    )(p16, p16, p16, p16, p16, mask, expand, toep, o_c, p32)


def _deltanet_kernel(q_ref, k_ref, v_ref, z_ref, sm_ref, cw_ref, alog_ref, dt_ref, ng_ref, o_ref, xp_ref, st_ref):
    c = q_ref.shape[1]
    pad = 8
    ch = pl.program_id(1)

    @pl.when(ch == 0)
    def _():
        xp_ref[:, 0:pad, :] = jnp.zeros((3, pad, DN_W), F32)
        st_ref[...] = jnp.zeros(st_ref.shape, F32)

    conv = []
    for idx, r in enumerate((q_ref, k_ref, v_ref)):
        xp_ref[idx, pad:pad + c, :] = r[0]
        acc = None
        for j in range(DN_CONV):
            off = pad - (DN_CONV - 1) + j
            term = cw_ref[j:j + 1, idx * DN_W:(idx + 1) * DN_W] * xp_ref[idx, off:off + c, :]
            acc = term if acc is None else acc + term
        xp_ref[idx, 0:pad, :] = xp_ref[idx, c:c + pad, :]
        conv.append(_silu(acc))
    qa, ka, va = conv
    za = _silu(z_ref[0])

    sm = sm_ref[0]
    beta_all = _sigmoid(sm)
    g_all = -jnp.exp(alog_ref[...]) * _softplus(sm + dt_ref[...])
    row = lax.broadcasted_iota(jnp.int32, (c, c), 0)
    col = lax.broadcasted_iota(jnp.int32, (c, c), 1)
    tril = row >= col
    strict = row > col
    eye = row == col
    tril01 = jnp.where(tril, 1.0, 0.0).astype(BF16)
    ones01 = jnp.ones((c, c), BF16)
    gc_all = _dot_01(tril01, g_all)
    eg_all = jnp.exp(gc_all)
    eye_f = jnp.where(eye, 1.0, 0.0)
    ng = ng_ref[...]

    hh = range(DN_HEADS)
    hsl = [slice(h * DN_KDIM, (h + 1) * DN_KDIM) for h in hh]
    states = [st_ref[h] for h in hh]
    q = [qa[:, hsl[h]] for h in hh]
    k = [ka[:, hsl[h]] for h in hh]
    q = [q[h] * lax.rsqrt(jnp.sum(q[h] * q[h], axis=-1, keepdims=True) + EPS) * (DN_KDIM ** -0.5) for h in hh]
    k = [k[h] * lax.rsqrt(jnp.sum(k[h] * k[h], axis=-1, keepdims=True) + EPS) for h in hh]
    beta = [beta_all[:, SM_B + h:SM_B + h + 1] for h in hh]
    gcol = [gc_all[:, SM_A + h:SM_A + h + 1] for h in hh]
    egcol = [eg_all[:, SM_A + h:SM_A + h + 1] for h in hh]
    grow = _dot_01_many(ones01, [jnp.where(eye, gcol[h], 0.0) for h in hh])
    decay = [jnp.where(tril, jnp.exp(jnp.where(tril, gcol[h] - grow[h], 0.0)), 0.0) for h in hh]
    glast = [grow[h][:, c - 1:c] for h in hh]
    kb = [k[h] * beta[h] for h in hh]
    vb = [va[:, hsl[h]] * beta[h] for h in hh]
    kbf = [k[h].astype(BF16) for h in hh]
    kk = [_dot_nt(kb[h].astype(BF16), kbf[h]) for h in hh]
    a = [jnp.where(strict, kk[h] * decay[h], 0.0) for h in hh]
    t_inv = [eye_f - a[h] for h in hh]
    pw = a
    for _ in range(int(math.log2(c)) - 1):
        pw = _dot_x3_many(pw, pw)
        tp = _dot_x3_many(t_inv, pw)
        t_inv = [t_inv[h] + tp[h] for h in hh]
    rhs = [jnp.concatenate([vb[h], kb[h] * egcol[h]], axis=1) for h in hh]
    sol = _dot_x3_many(t_inv, rhs)
    qk = [_dot_nt(q[h].astype(BF16), kbf[h]) for h in hh]
    attn = [(qk[h] * decay[h]).astype(BF16) for h in hh]
    q_dec = [(q[h] * egcol[h]).astype(BF16) for h in hh]
    k_dec = [(k[h] * jnp.exp(glast[h] - gcol[h])).astype(BF16) for h in hh]
    sb = [states[h].astype(BF16) for h in hh]
    ws = [_dot(sol[h][:, DN_VDIM:].astype(BF16), sb[h]) for h in hh]
    v_new = [(sol[h][:, :DN_VDIM] - ws[h]).astype(BF16) for h in hh]
    o1 = [_dot(q_dec[h], sb[h]) for h in hh]
    o2 = [_dot(attn[h], v_new[h]) for h in hh]
    kv = [_dot_tn(k_dec[h], v_new[h]) for h in hh]
    for h in hh:
        st_ref[h] = states[h] * jnp.exp(glast[h][0:1, :]) + kv[h]
    o = [o1[h] + o2[h] for h in hh]
    outs = [o[h] * lax.rsqrt(jnp.mean(o[h] * o[h], axis=-1, keepdims=True) + EPS) * ng * za[:, hsl[h]] for h in hh]
    o_ref[0] = jnp.concatenate(outs, axis=1).astype(o_ref.dtype)


def _deltanet(p32, conv_w, alog_row, dt_row, norm_g):
    bsz, s, _ = p32.shape
    c = DN_CHUNK
    blk = lambda col: pl.BlockSpec((1, c, DN_W), lambda b, i: (b, i, col // DN_W))
    full2 = lambda a: pl.BlockSpec(a.shape, lambda b, i: (0, 0))
    ng = norm_g.reshape(1, DN_VDIM)
    return pl.pallas_call(
        _deltanet_kernel,
        grid=(bsz, s // c),
        in_specs=[blk(C_DNQ), blk(C_DNK), blk(C_DNV), blk(C_DNZ),
                  pl.BlockSpec((1, c, LANES), lambda b, i: (b, i, C_SMALL // LANES)),
                  full2(conv_w), full2(alog_row), full2(dt_row), full2(ng)],
        out_specs=pl.BlockSpec((1, c, DN_W), lambda b, i: (b, i, 0)),
        out_shape=jax.ShapeDtypeStruct((bsz, s, DN_W), BF16),
        scratch_shapes=[pltpu.VMEM((3, c + 8, DN_W), F32),
                        pltpu.VMEM((DN_HEADS, DN_KDIM, DN_VDIM), F32)],
        compiler_params=_cparams("arbitrary", "arbitrary"),
        name="deltanet",
    )(p32, p32, p32, p32, p32, conv_w, alog_row, dt_row, ng)


def _mix_out_kernel(x_ref, mod_ref, onsa_ref, odn_ref, gn_ref, gd_ref, wn_ref, wd_ref, wo_ref, o_ref):
    merged = (_sigmoid(gn_ref[0]) * _dot(onsa_ref[0], wn_ref[...])
              + _sigmoid(gd_ref[0]) * _dot(odn_ref[0], wd_ref[...]))
    y = _dot(merged.astype(BF16), wo_ref[...])
    o_ref[0] = x_ref[0] + mod_ref[0][2:3] * y


def _mix_out(x, mod, o_nsa, o_dn, p32, wn, wd, wo, tm=256):
    bsz, s, d = x.shape
    full2 = lambda a: pl.BlockSpec(a.shape, lambda b, i: (0, 0))
    return pl.pallas_call(
        _mix_out_kernel,
        grid=(bsz, s // tm),
        in_specs=[pl.BlockSpec((1, tm, d), lambda b, i: (b, i, 0)),
                  pl.BlockSpec((1, 6, d), lambda b, i: (b, 0, 0)),
                  pl.BlockSpec((1, tm, NSA_Q_W), lambda b, i: (b, i, 0)),
                  pl.BlockSpec((1, tm, DN_W), lambda b, i: (b, i, 0)),
                  pl.BlockSpec((1, tm, d), lambda b, i: (b, i, C_GNSA // d)),
                  pl.BlockSpec((1, tm, d), lambda b, i: (b, i, C_GDN // d)),
                  full2(wn), full2(wd), full2(wo)],
        out_specs=pl.BlockSpec((1, tm, d), lambda b, i: (b, i, 0)),
        out_shape=jax.ShapeDtypeStruct((bsz, s, d), F32),
        compiler_params=_cparams("arbitrary", "arbitrary"),
        name="mix_out",
    )(x, mod, o_nsa, o_dn, p32, p32, wn, wd, wo)


def _top_k_rows(s, idx, k, payload=None):
    vals, picks = [], []
    for _ in range(k):
        mx = jnp.max(s, axis=0, keepdims=True)
        am = jnp.min(jnp.where(s == mx, idx, 1e9), axis=0, keepdims=True)
        hit = idx == am
        vals.append(mx)
        if payload is None:
            picks.append(am)
        else:
            picks.append(jnp.sum(jnp.where(hit, payload, 0.0), axis=0, keepdims=True))
        s = jnp.where(hit, -jnp.inf, s)
    return jnp.concatenate(vals, axis=0), jnp.concatenate(picks, axis=0)


def _peer_route_kernel(x_ref, mod_ref, g_ref, wq_ref, k1_ref, k2_ref, h_ref, idx_ref, gate_ref):
    tm = x_ref.shape[1]
    h2 = _modulated_norm(x_ref[0], g_ref[...], mod_ref[0], 3, 4)
    hb = h2.astype(BF16)
    h_ref[0] = hb
    qry = _dot(hb, wq_ref[...])
    kidx = lax.broadcasted_iota(jnp.int32, (PEER_NKEYS, tm), 0).astype(F32)
    kk = PEER_TOPK
    hk = kk // 2
    b_iota = kidx[0:kk]
    cidx = jnp.concatenate([b_iota] + [a * float(kk) + b_iota[0:hk] for a in range(1, hk)]
                           + [(b_iota[hk:kk]) * float(kk)], axis=0)
    for h in range(PEER_HEADS):
        qh = qry[:, h * PEER_DKEY:(h + 1) * PEER_DKEY].astype(BF16)
        s1 = _dot_nt(k1_ref[h], qh[:, :PEER_HALF])
        s2 = _dot_nt(k2_ref[h], qh[:, PEER_HALF:])
        v1, i1 = _top_k_rows(s1, kidx, PEER_TOPK)
        v2, i2 = _top_k_rows(s2, kidx, PEER_TOPK)
        cand_s = jnp.concatenate([v1[0:1] + v2] + [v1[a:a + 1] + v2[0:hk] for a in range(1, hk)]
                                 + [v1[hk:kk] + v2[0:1]], axis=0)
        e1 = i1 * float(PEER_NKEYS)
        cand_i = jnp.concatenate([e1[0:1] + i2] + [e1[a:a + 1] + i2[0:hk] for a in range(1, hk)]
                                 + [e1[hk:kk] + i2[0:1]], axis=0)
        top_s, top_e = _top_k_rows(cand_s, cidx, PEER_TOPK, payload=cand_i)
        e = jnp.exp(top_s - top_s[0:1])
        gate = e / jnp.sum(e, axis=0, keepdims=True)
        idx_ref[h * PEER_TOPK:(h + 1) * PEER_TOPK, :] = (top_e * float(PACKED_ROWS)).astype(jnp.int32)
        gate_ref[h * PEER_TOPK:(h + 1) * PEER_TOPK, :] = gate


def _peer_route(x1, mod, g, wq, k1, k2, tm=128):
    bsz, s, d = x1.shape
    nt = s // tm
    ne = PEER_HEADS * PEER_TOPK
    full = lambda a: pl.BlockSpec(a.shape, lambda b, i: (0,) * a.ndim)
    return pl.pallas_call(
        _peer_route_kernel,
        grid=(bsz, nt),
        in_specs=[pl.BlockSpec((1, tm, d), lambda b, i: (b, i, 0)),
                  pl.BlockSpec((1, 6, d), lambda b, i: (b, 0, 0)),
                  pl.BlockSpec((1, d), lambda b, i: (0, 0)),
                  full(wq), full(k1), full(k2)],
        out_specs=[pl.BlockSpec((1, tm, d), lambda b, i: (b, i, 0)),
                   pl.BlockSpec((ne, tm), lambda b, i: (0, b * nt + i)),
                   pl.BlockSpec((ne, tm), lambda b, i: (0, b * nt + i))],
        out_shape=[jax.ShapeDtypeStruct((bsz, s, d), BF16),
                   jax.ShapeDtypeStruct((ne, bsz * s), jnp.int32),
                   jax.ShapeDtypeStruct((ne, bsz * s), F32)],
        compiler_params=_cparams("arbitrary", "arbitrary"),
        name="peer_route",
    )(x1, mod, g.reshape(1, d), wq, k1, k2)


PEER_TOK = 64
EXPERT_TILE_ROWS = 8
PACKED_ROWS = EXPERT_TILE_ROWS // 2


def _load_table_once(tab_hbm, tab_vmem, sem):
    @pl.when(pl.program_id(0) == 0)
    def _():
        cp = pltpu.make_async_copy(tab_hbm, tab_vmem, sem)
        cp.start()
        cp.wait()


def _pack_table(tab):
    e = tab.shape[0]
    bits = lax.bitcast_convert_type(tab.astype(BF16).reshape(e * PACKED_ROWS, 2, LANES), jnp.uint16)
    bits = bits.astype(jnp.uint32)
    return lax.bitcast_convert_type(bits[:, 0, :] | (bits[:, 1, :] << 16), jnp.int32)


def _stage_experts(idx_ref, tab_vmem, stage, nsel):
    for r in range(nsel):
        row = pl.multiple_of(idx_ref[0, 0, r], PACKED_ROWS)
        stage[r * PACKED_ROWS:(r + 1) * PACKED_ROWS, :] = tab_vmem[pl.ds(row, PACKED_ROWS), :]


def _staged_tiles(stage, t, ncol):
    return pltpu.bitcast(stage[t * (ncol // 2):(t + 1) * (ncol // 2), :], BF16)


def _diag_mask(ncol):
    row = lax.broadcasted_iota(jnp.int32, (EXPERT_TILE_ROWS, ncol), 0)
    col = lax.broadcasted_iota(jnp.int32, (EXPERT_TILE_ROWS, ncol), 1)
    return (col & (EXPERT_TILE_ROWS - 1)) == row


def _peer_act_kernel(idx_ref, tab_hbm, h_ref, gate_ref, grp_ref, w_ref, tab_vmem, stage, sem, *, rows_per_tok):
    _load_table_once(tab_hbm, tab_vmem, sem)
    _stage_experts(idx_ref, tab_vmem, stage, PEER_TOK * rows_per_tok)
    ncol = rows_per_tok * EXPERT_TILE_ROWS
    diag = _diag_mask(ncol)
    sums = []
    for t in range(PEER_TOK):
        m = _staged_tiles(stage, t, ncol)
        rt = _dot_nt(h_ref[t], m)
        sums.append(jnp.sum(jnp.where(diag, rt, 0.0), axis=0, keepdims=True))
    part = jnp.concatenate(sums, axis=0)
    hi, mid, lo = _split3(part)
    grp = grp_ref[...]
    act = _dot(hi, grp) + (_dot(mid, grp) + _dot(lo, grp))
    w_ref[...] = gate_ref[...] * _gelu_tanh(act)


def _peer_out_kernel(idx_ref, tab_hbm, w_ref, exp_ref, x_ref, mod_ref, fg_ref, o_ref, tab_vmem, stage, sem,
                     *, rows_per_tok):
    _load_table_once(tab_hbm, tab_vmem, sem)
    _stage_experts(idx_ref, tab_vmem, stage, PEER_TOK * rows_per_tok)
    ncol = rows_per_tok * EXPERT_TILE_ROWS
    diag = _diag_mask(ncol)
    wh, wl = _split2(w_ref[...])
    ex = exp_ref[...]
    weh = _dot(wh, ex)
    wel = _dot(wl, ex)
    g2 = mod_ref[0, 5]
    fg = fg_ref[...]
    nel = float(EXPERT_TILE_ROWS * LANES)
    for t in range(PEER_TOK):
        lhs = jnp.concatenate([jnp.where(diag, weh[t:t + 1], 0.0), jnp.where(diag, wel[t:t + 1], 0.0)], axis=0)
        o16 = _dot(lhs.astype(BF16), _staged_tiles(stage, t, ncol))
        y = o16[0:EXPERT_TILE_ROWS] + o16[EXPERT_TILE_ROWS:]
        xo = x_ref[t] + g2 * y
        ms = jnp.sum(jnp.sum(xo * xo, axis=1, keepdims=True), axis=0, keepdims=True) / nel
        o_ref[t] = xo * lax.rsqrt(ms + EPS) * fg


def _peer_experts(idx3, tab_u, tab_v, h3, gate, x3, mod4, fg3, s):
    n = h3.shape[0]
    ne = gate.shape[1]
    nstep = n // PEER_TOK
    nsel = PEER_TOK * ne
    ncol = ne * EXPERT_TILE_ROWS
    grp = jnp.asarray(np.arange(ncol)[:, None] // EXPERT_TILE_ROWS == np.arange(ne)[None, :], BF16)
    idx_spec = pl.BlockSpec((1, 1, nsel), lambda t: (t, 0, 0), memory_space=pltpu.SMEM)
    any_spec = pl.BlockSpec(memory_space=pl.ANY)
    tile_spec = pl.BlockSpec((PEER_TOK, EXPERT_TILE_ROWS, LANES), lambda t: (t, 0, 0))
    row_spec = pl.BlockSpec((PEER_TOK, ne), lambda t: (t, 0))
    scratch = [pltpu.VMEM(tab_u.shape, jnp.int32), pltpu.VMEM((PEER_TOK * ncol // 2, LANES), jnp.int32),
               pltpu.SemaphoreType.DMA]
    w = pl.pallas_call(
        functools.partial(_peer_act_kernel, rows_per_tok=ne),
        grid=(nstep,),
        in_specs=[idx_spec, any_spec, tile_spec, row_spec, pl.BlockSpec(grp.shape, lambda t: (0, 0))],
        out_specs=row_spec,
        out_shape=jax.ShapeDtypeStruct((n, ne), F32),
        scratch_shapes=scratch,
        compiler_params=_cparams("arbitrary"),
        name="peer_act",
    )(idx3, tab_u, h3, gate, grp)
    return pl.pallas_call(
        functools.partial(_peer_out_kernel, rows_per_tok=ne),
        grid=(nstep,),
        in_specs=[idx_spec, any_spec, row_spec, pl.BlockSpec((ne, ncol), lambda t: (0, 0)), tile_spec,
                  pl.BlockSpec((1, 6, EXPERT_TILE_ROWS, LANES), lambda t: ((t * PEER_TOK) // s, 0, 0, 0)),
                  pl.BlockSpec((EXPERT_TILE_ROWS, LANES), lambda t: (0, 0))],
        out_specs=tile_spec,
        out_shape=jax.ShapeDtypeStruct((n, EXPERT_TILE_ROWS, LANES), F32),
        scratch_shapes=scratch,
        compiler_params=_cparams("arbitrary"),
        name="peer_out",
    )(idx3, tab_v, w, grp.T, x3, mod4, fg3)


def _rel_bucket(dist):
    dist = jnp.maximum(dist, 0)
    max_exact = REL_BUCKETS // 2
    scaled = jnp.log(jnp.maximum(dist, 1).astype(F32) / max_exact) / math.log(REL_MAX_DIST / max_exact)
    large = jnp.minimum(max_exact + (scaled * (REL_BUCKETS - max_exact)).astype(jnp.int32), REL_BUCKETS - 1)
    return jnp.where(dist < max_exact, dist, large)


def _bias_tables(rel_bias, s, ncmp):
    dist_bias = rel_bias[_rel_bucket(jnp.arange(s))].T
    nq = s // ATT_TILE
    per_tile = ATT_TILE // CMP_STRIDE
    r = np.arange(ATT_TILE)[:, None]
    k0 = per_tile * (nq - 1)
    k = np.arange(k0 + ncmp)[None, :]
    dist_g = r - CMP_STRIDE * (k - k0) - (CMP_LEN - 1)
    g = dist_bias[:, np.clip(dist_g, 0, s - 1)]
    wins = jnp.stack([g[:, :, k0 - per_tile * i:k0 - per_tile * i + ncmp] for i in range(nq)], axis=1)
    bias_c = wins.reshape(rel_bias.shape[1], s, ncmp)
    ii = jnp.arange(ATT_TILE)[:, None]
    jj = jnp.arange(ATT_TILE)[None, :]
    dist = jnp.stack([m * ATT_TILE + ii - jj for m in range(3)])
    toep = dist_bias[:, jnp.clip(dist, 0, s - 1)]
    return bias_c, toep


def _check_bucket_saturation():
    d = np.arange(ATT_TILE + 1, 1 << 16)
    max_exact = REL_BUCKETS // 2
    scaled = np.log(d.astype(np.float32) / max_exact) / math.log(REL_MAX_DIST / max_exact)
    large = np.minimum(max_exact + (scaled * (REL_BUCKETS - max_exact)).astype(np.int32), REL_BUCKETS - 1)
    assert (large == REL_BUCKETS - 1).all()


_check_bucket_saturation()


def _token_mixer(x, mod, rel_bias, norm_g, w_in, cmp_pos_emb, w_ck1, w_ck2, w_cv1, w_cv2, conv_w, a_log, dt_bias,
                 dn_norm_g, w_branch_nsa, w_branch_dn, w_out):
    bsz, s, d = x.shape
    assert s % ATT_TILE == 0 and s % CMP_STRIDE == 0 and WINDOW % ATT_TILE == 0 and d == C_GDN - C_GNSA
    pts = np.cumsum(IN_SPLIT_SIZES + (d, d))[:-1]
    assert w_in.shape[1] == sum(IN_SPLIT_SIZES) + 2 * d
    (w_q, w_kc, w_vc, w_ks, w_vs, w_kw, w_vw, w_gates, w_dq, w_dk, w_dv, w_dz, w_db, w_da, w_gn, w_gd) = (
        jnp.split(w_in, [int(p) for p in pts], axis=1))
    w16 = jnp.concatenate([w_q * (HEAD_DIM ** -0.5), w_ks, w_vs, w_kw, w_vw], axis=1).astype(BF16)
    small_pad = jnp.zeros((d, LANES - NSA_HEADS * 3 - 2 * DN_HEADS), w_in.dtype)
    w32 = jnp.concatenate([w_dq, w_dk, w_dv, w_dz, w_gn, w_gd, w_kc, w_vc, w_gates, w_db, w_da, small_pad],
                          axis=1).astype(BF16)
    assert w16.shape[1] == C16 and w32.shape[1] == C32
    p16, p32 = _proj_in(x, mod, norm_g, w16, w32)

    nrow = s // CMP_STRIDE
    k2 = p32[:, :, C_KC:C_KC + NSA_KV_W].reshape(bsz, nrow, CMP_STRIDE * NSA_KV_W)
    v2 = p32[:, :, C_VC:C_VC + NSA_KV_W].reshape(bsz, nrow, CMP_STRIDE * NSA_KV_W)
    kc, vc = _compress(k2, v2, cmp_pos_emb, w_ck1, w_ck2, w_cv1, w_cv2)
    nslc = s // SLC_BLOCK
    top_n = min(SLC_TOPN, nslc)
    bias_c, toep = _bias_tables(rel_bias, s, nrow)
    cmp_start = np.arange(nrow) * CMP_STRIDE
    slc_start = np.arange(nslc) * SLC_BLOCK
    ovt = ((cmp_start[None, :] <= slc_start[:, None] + SLC_BLOCK - 1)
           & (cmp_start[None, :] + CMP_LEN - 1 >= slc_start[:, None]))
    ovt = jnp.asarray(ovt, BF16)
    o_c, mask = _nsa_cmp(p16, kc, vc, bias_c, ovt, top_n)
    nkt = s // ATT_TILE
    key_blk = (np.arange(nkt)[:, None, None] * ATT_TILE + np.arange(ATT_TILE)[None, None, :]) // SLC_BLOCK
    expand = jnp.asarray(key_blk == np.arange(nslc)[None, :, None], BF16)
    o_nsa = _nsa_sw(p16, p32, mask, expand, toep, o_c)

    lane_row = lambda vec: jnp.zeros((1, LANES), F32).at[0, SM_A:SM_A + DN_HEADS].set(vec.astype(F32))
    o_dn = _deltanet(p32, conv_w, lane_row(a_log), lane_row(dt_bias), dn_norm_g)

    return _mix_out(x, mod, o_nsa, o_dn, p32, w_branch_nsa.astype(BF16), w_branch_dn.astype(BF16),
                    w_out.astype(BF16))


def _peer_block(x1, mod, norm_g, final_g, w_query, keys1, keys2, expert_u, expert_v):
    bsz, s, d = x1.shape
    n = bsz * s
    ne = PEER_HEADS * PEER_TOPK
    h2, idx_t, gate_t = _peer_route(x1, mod, norm_g, w_query.astype(BF16), keys1.astype(BF16), keys2.astype(BF16))
    assert d == EXPERT_TILE_ROWS * LANES
    nstep = n // PEER_TOK
    idx3 = idx_t.T.reshape(nstep, 1, PEER_TOK * ne)
    tile = (EXPERT_TILE_ROWS, LANES)
    tab_u = _pack_table(expert_u)
    tab_v = _pack_table(expert_v)
    out = _peer_experts(idx3, tab_u, tab_v, h2.reshape((n,) + tile), gate_t.T, x1.reshape((n,) + tile),
                        mod.reshape((bsz, 6) + tile), final_g.reshape(tile), s)
    return out.reshape(bsz, s, d)


def kernel(x, c, rel_bias, final_g, w_ada, b_ada, norm1_g, w_in, cmp_pos_emb, w_cmp_k1, w_cmp_k2, w_cmp_v1,
           w_cmp_v2, dn_conv_w, dn_A_log, dn_dt_bias, dn_norm_g, w_branch_nsa, w_branch_dn, w_out, norm2_g,
           peer_w_query, peer_keys1, peer_keys2, peer_u, peer_v):
    depth = w_ada.shape[0]
    assert depth == 1, "the final norm is fused into the last PEER kernel; one layer is supported"
    bsz, s, d = x.shape
    mod = _ada(c, w_ada[0], b_ada[0]).reshape(bsz, 6, d)
    x1 = _token_mixer(x, mod, rel_bias, norm1_g[0], w_in[0], cmp_pos_emb[0], w_cmp_k1[0], w_cmp_k2[0], w_cmp_v1[0],
                      w_cmp_v2[0], dn_conv_w[0], dn_A_log[0], dn_dt_bias[0], dn_norm_g[0], w_branch_nsa[0],
                      w_branch_dn[0], w_out[0])
    return _peer_block(x1, mod, norm2_g[0], final_g, peer_w_query[0], peer_keys1[0], peer_keys2[0], peer_u[0],
                       peer_v[0])
```
